```python
import math
import jax, jax.numpy as jnp
from jax import lax
import numpy as np


D_MODEL = 1024
BATCH = 16
SEQ = 2048
DEPTH = 1

CHUNK = 64
Q_BLOCK = 128

A_HEADS = 4
A_HEAD_DIM = 128
A_WIDTH = A_HEADS * A_HEAD_DIM
A_KV_RANK = 128
IDX_HEADS = 8
IDX_DIM = 64
TOPK_MAX = 256

B_HEADS = 4
B_KEY_DIM = 128
B_VAL_DIM = 128
B_WIDTH = B_HEADS * B_VAL_DIM
B_FORGET = B_HEADS * B_KEY_DIM

D_MIX = A_WIDTH + B_WIDTH

REL_BUCKETS = 32
REL_MAX_DIST = 256

DEEPNORM_ALPHA = (2.0 * DEPTH) ** 0.25
DEEPNORM_BETA = (8.0 * DEPTH) ** -0.25
LN_EPS = 1e-5
RMS_EPS = 1e-6

_SPLITS = (
    ('a_q', A_WIDTH),
    ('a_ckv', A_KV_RANK),
    ('a_iq', IDX_HEADS * IDX_DIM),
    ('a_ik', IDX_DIM),
    ('a_iw', IDX_HEADS),
    ('a_gate', A_WIDTH),
    ('b_q', B_FORGET),
    ('b_f', B_FORGET),
    ('b_i', B_WIDTH),
    ('b_gate', B_WIDTH),
)
D_IN_PROJ = int(sum(n for _, n in _SPLITS))
SPLIT_POINTS = tuple(int(v) for v in np.cumsum([n for _, n in _SPLITS])[:-1])

kernel_name = 'hybrid_dsa_hgrn2_deepnorm_layer'


def _rms_norm(x, g):
    xf = x.astype(jnp.float32)
    y = xf * lax.rsqrt(jnp.mean(xf * xf, axis=-1, keepdims=True) + RMS_EPS)
    return (y * g.astype(jnp.float32)).astype(x.dtype)


def _layer_norm(x, g, b):
    xf = x.astype(jnp.float32)
    mu = jnp.mean(xf, axis=-1, keepdims=True)
    var = jnp.mean(jnp.square(xf - mu), axis=-1, keepdims=True)
    y = (xf - mu) * lax.rsqrt(var + LN_EPS) * g.astype(jnp.float32) + b.astype(jnp.float32)
    return y.astype(x.dtype)


def _t5_bucket(rel):
    nb = REL_BUCKETS // 2
    max_exact = nb // 2
    ret = jnp.where(rel > 0, nb, 0).astype(jnp.int32)
    n = jnp.abs(rel)
    nf = jnp.maximum(n, 1).astype(jnp.float32)
    large = max_exact + (jnp.log(nf / max_exact) / math.log(REL_MAX_DIST / max_exact)
                         * (nb - max_exact)).astype(jnp.int32)
    large = jnp.minimum(large, nb - 1)
    return ret + jnp.where(n < max_exact, n, large)


def _dsa_mixer(q, c, iq, ik, iw, w_uk, rel_bias):
    bsz, seq = q.shape[0], q.shape[1]
    k_sel = min(TOPK_MAX, seq // 4)
    n_blk = seq // Q_BLOCK
    q_lat = jnp.einsum('bshd,hdr->bshr', q, w_uk)
    scale = A_HEAD_DIM ** -0.5
    idx_scale = IDX_DIM ** -0.5
    w_scale = IDX_HEADS ** -0.5
    pos = jnp.arange(seq, dtype=jnp.int32)
    key_chunk = pos // CHUNK

    def blockify(t):
        return jnp.moveaxis(t.reshape(bsz, n_blk, Q_BLOCK, *t.shape[2:]), 1, 0)

    xs = (blockify(q_lat), blockify(iq), blockify(iw), pos.reshape(n_blk, Q_BLOCK))

    def one_block(args):
        ql, iqb, iwb, qpos = args
        q_chunk = qpos // CHUNK
        admissible = key_chunk[None, :] <= q_chunk[:, None]
        dots = jnp.einsum('bqhd,bsd->bqhs', iqb, ik) * idx_scale
        score = jnp.einsum('bqh,bqhs->bqs', iwb * w_scale, jax.nn.relu(dots))
        score = jnp.where(admissible[None], score.astype(jnp.float32), -jnp.inf)
        _, sel = lax.top_k(score, k_sel)
        valid = (sel // CHUNK) <= q_chunk[None, :, None]
        c_sel = jax.vmap(lambda cb, ib: cb[ib])(c, sel)
        bias = rel_bias[_t5_bucket(sel - qpos[None, :, None])]
        logits = (jnp.einsum('bqhr,bqkr->bqhk', ql, c_sel).astype(jnp.float32) * scale
                  + jnp.swapaxes(bias, -1, -2).astype(jnp.float32))
        logits = jnp.where(valid[:, :, None, :], logits, -jnp.inf)
        p = jax.nn.softmax(logits, axis=-1).astype(c.dtype)
        return jnp.einsum('bqhk,bqkr->bqhr', p, c_sel)

    o = lax.map(one_block, xs)
    return jnp.moveaxis(o, 0, 1).reshape(bsz, seq, A_HEADS, A_KV_RANK)


def _hgrn2_mixer(q, f_raw, v, lb):
    bsz, seq = q.shape[0], q.shape[1]
    n_chunk = seq // CHUNK
    q = jax.nn.silu(q.astype(jnp.float32))
    lbh = lb.reshape(B_HEADS, B_KEY_DIM).astype(jnp.float32)
    f = lbh + (1.0 - lbh) * jax.nn.sigmoid(f_raw.astype(jnp.float32))
    k = 1.0 - f
    log_f = jnp.log(f)

    def chunkify(t):
        return jnp.transpose(t.reshape(bsz, n_chunk, CHUNK, B_HEADS, t.shape[-1]), (1, 0, 3, 2, 4))

    qc, kc, vc = chunkify(q), chunkify(k), chunkify(v.astype(jnp.float32))
    bc = jnp.cumsum(chunkify(log_f), axis=3)
    causal = jnp.tril(jnp.ones((CHUNK, CHUNK), dtype=bool))

    def step(state, inp):
        qn, kn, vn, bn = inp
        diff = bn[:, :, :, None, :] - bn[:, :, None, :, :]
        decay = jnp.exp(jnp.where(causal[:, :, None], diff, -jnp.inf))
        scores = jnp.einsum('bhtk,bhsk,bhtsk->bhts', qn, kn, decay)
        o = (jnp.einsum('bhts,bhsv->bhtv', scores, vn)
             + jnp.einsum('bhtk,bhkv->bhtv', qn * jnp.exp(bn), state))
        b_last = bn[:, :, -1:, :]
        state = (jnp.exp(b_last[:, :, 0, :])[..., None] * state
                 + jnp.einsum('bhsk,bhsv->bhkv', kn * jnp.exp(b_last - bn), vn))
        return state, o

    s0 = jnp.zeros((bsz, B_HEADS, B_KEY_DIM, B_VAL_DIM), jnp.float32)
    _, o = lax.scan(step, s0, (qc, kc, vc, bc))
    return jnp.transpose(o, (1, 0, 3, 2, 4)).reshape(bsz, seq, B_HEADS, B_VAL_DIM)


def setup_inputs(seed: int = 0) -> dict:
    key = jax.random.key(seed)
    ks = jax.random.split(key, 12)
    x = jax.random.normal(ks[0], (BATCH, SEQ, D_MODEL), jnp.float32)
    col_scale = jnp.concatenate([
        jnp.full((n,), DEEPNORM_BETA if name == 'b_i' else 1.0, jnp.float32) for name, n in _SPLITS])
    w_in = jax.random.normal(ks[1], (DEPTH, D_MODEL, D_IN_PROJ), jnp.float32) * (D_MODEL ** -0.5) * col_scale
    w_uk = jax.random.normal(ks[2], (DEPTH, A_HEADS, A_HEAD_DIM, A_KV_RANK), jnp.float32) * (A_HEAD_DIM ** -0.5)
    w_uv = (jax.random.normal(ks[3], (DEPTH, A_HEADS, A_KV_RANK, A_HEAD_DIM), jnp.float32)
            * (A_KV_RANK ** -0.5) * DEEPNORM_BETA)
    kv_norm_g = 1.0 + 0.05 * jax.random.normal(ks[4], (DEPTH, A_KV_RANK), jnp.float32)
    rel_bias = 0.2 * jax.random.normal(ks[5], (REL_BUCKETS, A_HEADS), jnp.float32)
    lb_logits = 0.5 * jax.random.normal(ks[6], (DEPTH + 1, B_FORGET), jnp.float32)
    hgrn_norm_g = 1.0 + 0.05 * jax.random.normal(ks[7], (DEPTH, B_WIDTH), jnp.float32)
    w_o = jax.random.normal(ks[8], (DEPTH, D_MIX, D_MODEL), jnp.float32) * (D_MIX ** -0.5) * DEEPNORM_BETA
    ln_g = 1.0 + 0.05 * jax.random.normal(ks[9], (DEPTH, D_MODEL), jnp.float32)
    ln_b = 0.02 * jax.random.normal(ks[10], (DEPTH, D_MODEL), jnp.float32)
    return {'x': x, 'w_in': w_in, 'w_uk': w_uk, 'w_uv': w_uv, 'kv_norm_g': kv_norm_g,
            'rel_bias': rel_bias, 'lb_logits': lb_logits, 'hgrn_norm_g': hgrn_norm_g,
            'w_o': w_o, 'ln_g': ln_g, 'ln_b': ln_b}


def reference(x, w_in, w_uk, w_uv, kv_norm_g, rel_bias, lb_logits, hgrn_norm_g, w_o, ln_g, ln_b):
    bsz, seq, _ = x.shape
    lb_all = jnp.cumsum(jax.nn.softmax(lb_logits.astype(jnp.float32), axis=0), axis=0)
    for layer in range(DEPTH):
        h = jnp.einsum('bsd,dc->bsc', x, w_in[layer])
        a_q, a_ckv, a_iq, a_ik, a_iw, a_gate, b_q, b_f, b_i, b_gate = jnp.split(h, SPLIT_POINTS, axis=-1)

        c = _rms_norm(a_ckv, kv_norm_g[layer])
        o_lat = _dsa_mixer(a_q.reshape(bsz, seq, A_HEADS, A_HEAD_DIM), c,
                           a_iq.reshape(bsz, seq, IDX_HEADS, IDX_DIM), a_ik, a_iw,
                           w_uk[layer], rel_bias)
        o_a = jnp.einsum('bshr,hrv->bshv', o_lat, w_uv[layer]).reshape(bsz, seq, A_WIDTH)
        o_a = o_a * jax.nn.silu(a_gate)

        o_b = _hgrn2_mixer(b_q.reshape(bsz, seq, B_HEADS, B_KEY_DIM),
                           b_f.reshape(bsz, seq, B_HEADS, B_KEY_DIM),
                           b_i.reshape(bsz, seq, B_HEADS, B_VAL_DIM), lb_all[layer])
        o_b = _rms_norm(o_b, hgrn_norm_g[layer].reshape(B_HEADS, B_VAL_DIM)).reshape(bsz, seq, B_WIDTH)
        o_b = o_b.astype(x.dtype) * jax.nn.silu(b_gate)

        y = jnp.einsum('bsc,cd->bsd', jnp.concatenate([o_a.astype(x.dtype), o_b], axis=-1), w_o[layer])
        x = _layer_norm(DEEPNORM_ALPHA * x + y, ln_g[layer], ln_b[layer])
    return x
```

```python
import functools
import math

import jax
import jax.numpy as jnp
from jax import lax
from jax.experimental import pallas as pl
from jax.experimental.pallas import tpu as pltpu

F32 = jnp.float32
BF16 = jnp.bfloat16
I32 = jnp.int32

D_MODEL = 1024
DEPTH = 1
CHUNK = 64
Q_BLOCK = 128
A_HEADS = 4
A_HEAD_DIM = 128
A_WIDTH = A_HEADS * A_HEAD_DIM
A_KV_RANK = 128
IDX_HEADS = 8
IDX_DIM = 64
TOPK_MAX = 256
B_HEADS = 4
B_KEY_DIM = 128
B_VAL_DIM = 128
B_WIDTH = B_HEADS * B_VAL_DIM
B_FORGET = B_HEADS * B_KEY_DIM
REL_BUCKETS = 32
REL_MAX_DIST = 256
DEEPNORM_ALPHA = (2.0 * DEPTH) ** 0.25
LN_EPS = 1e-5
RMS_EPS = 1e-6

_SPLITS = (
    ('a_q', A_WIDTH), ('a_ckv', A_KV_RANK), ('a_iq', IDX_HEADS * IDX_DIM), ('a_ik', IDX_DIM),
    ('a_iw', IDX_HEADS), ('a_gate', A_WIDTH), ('b_q', B_FORGET), ('b_f', B_FORGET),
    ('b_i', B_WIDTH), ('b_gate', B_WIDTH),
)
_OFFSETS = {}
_off = 0
for _name, _n in _SPLITS:
    _OFFSETS[_name] = (_off, _n)
    _off += _n

LANES = 128
KEY_BLOCK = 128
COUNT_ROWS = 256
SUB = 16
EXP_CAP = 80.0
NEG_BIG = -1e30
INT_MIN = -2 ** 31
PROJ_TILE = 512
VMEM_LIMIT = 56 * 1024 * 1024


def _col(w, name):
    o, n = _OFFSETS[name]
    return w[:, o:o + n]


def _t5_bucket(rel):
    nb = REL_BUCKETS // 2
    max_exact = nb // 2
    ret = jnp.where(rel > 0, nb, 0).astype(I32)
    n = jnp.abs(rel)
    nf = jnp.maximum(n, 1).astype(F32)
    large = max_exact + (jnp.log(nf / max_exact) / math.log(REL_MAX_DIST / max_exact)
                         * (nb - max_exact)).astype(I32)
    large = jnp.minimum(large, nb - 1)
    return ret + jnp.where(n < max_exact, n, large)


def _bias_kernel(rb_ref, out_ref):
    sl = lax.broadcasted_iota(I32, (KEY_BLOCK, LANES), 0)
    ql = lax.broadcasted_iota(I32, (KEY_BLOCK, LANES), 1)
    far = REL_BUCKETS // 2 - 1
    for d in range(3):
        bucket = _t5_bucket(sl - ql - KEY_BLOCK * d)
        for h in range(A_HEADS):
            acc = jnp.zeros((KEY_BLOCK, LANES), F32)
            for bk in range(REL_BUCKETS):
                acc = jnp.where(bucket == bk, rb_ref[bk, h] - rb_ref[far, h], acc)
            out_ref[d, :, h * LANES:(h + 1) * LANES] = acc


def _bias_tiles(rel_bias):
    return pl.pallas_call(
        _bias_kernel,
        out_shape=jax.ShapeDtypeStruct((3, KEY_BLOCK, A_HEADS * LANES), F32),
        in_specs=[pl.BlockSpec(memory_space=pltpu.SMEM)],
        out_specs=pl.BlockSpec(memory_space=pltpu.VMEM),
        name="bias",
    )(rel_bias.astype(F32))


def _proj_kernel(x_ref, wrow_ref, wt_ref, wukt_ref, g_ref, lbl_ref,
                 ik_ref, c_ref, ga_ref, gb_ref, qs_ref, logf_ref, v_ref,
                 iqt_ref, iwt_ref, qlt_ref, ct_ref):
    tq = x_ref.shape[0]
    xb = x_ref[...].astype(BF16)

    def rowdot(lo, n):
        return jnp.dot(xb, wrow_ref[:, lo:lo + n], preferred_element_type=F32)

    h0 = rowdot(0, 256)
    a = h0[:, :A_KV_RANK]
    ms = jnp.mean(a * a, axis=-1, keepdims=True)
    c = a * lax.rsqrt(ms + RMS_EPS) * g_ref[...]
    c_ref[...] = c.astype(BF16)
    ct = c.T
    for i in range(tq // KEY_BLOCK):
        ct_ref[0, i] = ct[:, i * KEY_BLOCK:(i + 1) * KEY_BLOCK].astype(BF16)
    ik_ref[...] = h0[:, A_KV_RANK:A_KV_RANK + IDX_DIM].astype(BF16)

    ga_ref[...] = jax.nn.silu(rowdot(256, A_WIDTH)).astype(BF16)
    qs_ref[...] = jax.nn.silu(rowdot(768, B_FORGET)).astype(BF16)
    lbl = lbl_ref[...]
    e = jnp.exp(lbl - jnp.max(lbl, axis=0, keepdims=True))
    lb = e[0:1, :] / jnp.sum(e, axis=0, keepdims=True)
    f = lb + (1.0 - lb) * jax.nn.sigmoid(rowdot(1280, B_FORGET))
    logf_ref[...] = jnp.log(f)
    v_ref[...] = rowdot(1792, B_WIDTH).astype(BF16)
    gb_ref[...] = jax.nn.silu(rowdot(2304, B_WIDTH)).astype(BF16)

    ht = lax.dot_general(wt_ref[...], xb, (((1,), (1,)), ((), ())), preferred_element_type=F32)
    iqt_ref[0] = ht[A_WIDTH:A_WIDTH + IDX_HEADS * IDX_DIM].astype(BF16)
    o = A_WIDTH + IDX_HEADS * IDX_DIM
    iwt_ref[0] = ht[o:o + IDX_HEADS] * (IDX_HEADS ** -0.5 * IDX_DIM ** -0.5)
    for h in range(A_HEADS):
        aq = ht[h * A_HEAD_DIM:(h + 1) * A_HEAD_DIM].astype(BF16)
        ql = jnp.dot(wukt_ref[h], aq, preferred_element_type=F32) * (A_HEAD_DIM ** -0.5)
        qlt_ref[0, h] = ql.astype(BF16)


def _project(x2, w_in, w_uk, kv_g, lb_logits, bsz, seq):
    t = bsz * seq
    tq = min(PROJ_TILE, seq)
    per_b = seq // tq
    pad64 = jnp.zeros((D_MODEL, 64), F32)
    wrow = jnp.concatenate([_col(w_in, 'a_ckv'), _col(w_in, 'a_ik'), pad64, _col(w_in, 'a_gate'),
                            _col(w_in, 'b_q'), _col(w_in, 'b_f'), _col(w_in, 'b_i'),
                            _col(w_in, 'b_gate')], axis=1).astype(BF16)
    pad8 = jnp.zeros((D_MODEL, 8), F32)
    wt = jnp.concatenate([_col(w_in, 'a_q'), _col(w_in, 'a_iq'), _col(w_in, 'a_iw'), pad8],
                         axis=1).T.astype(BF16)
    wukt = jnp.swapaxes(w_uk, 1, 2).astype(BF16)
    nrow, nt = wrow.shape[1], wt.shape[0]
    row = lambda n: pl.BlockSpec((tq, n), lambda i: (i, 0))
    full = lambda shape: pl.BlockSpec(shape, lambda i: (0,) * len(shape))
    out_shape = (
        jax.ShapeDtypeStruct((t, IDX_DIM), BF16),
        jax.ShapeDtypeStruct((t, A_KV_RANK), BF16),
        jax.ShapeDtypeStruct((t, A_WIDTH), BF16),
        jax.ShapeDtypeStruct((t, B_WIDTH), BF16),
        jax.ShapeDtypeStruct((t, B_FORGET), BF16),
        jax.ShapeDtypeStruct((t, B_FORGET), F32),
        jax.ShapeDtypeStruct((t, B_WIDTH), BF16),
        jax.ShapeDtypeStruct((bsz, IDX_HEADS * IDX_DIM, seq), BF16),
        jax.ShapeDtypeStruct((bsz, IDX_HEADS, seq), F32),
        jax.ShapeDtypeStruct((bsz, A_HEADS, A_KV_RANK, seq), BF16),
        jax.ShapeDtypeStruct((bsz, seq // KEY_BLOCK, A_KV_RANK, KEY_BLOCK), BF16),
    )
    out_specs = (
        row(IDX_DIM), row(A_KV_RANK), row(A_WIDTH), row(B_WIDTH), row(B_FORGET), row(B_FORGET), row(B_WIDTH),
        pl.BlockSpec((1, IDX_HEADS * IDX_DIM, tq), lambda i: (i // per_b, 0, i % per_b)),
        pl.BlockSpec((1, IDX_HEADS, tq), lambda i: (i // per_b, 0, i % per_b)),
        pl.BlockSpec((1, A_HEADS, A_KV_RANK, tq), lambda i: (i // per_b, 0, 0, i % per_b)),
        pl.BlockSpec((1, tq // KEY_BLOCK, A_KV_RANK, KEY_BLOCK), lambda i: (i // per_b, i % per_b, 0, 0)),
    )
    return pl.pallas_call(
        _proj_kernel,
        grid=(t // tq,),
        in_specs=[row(D_MODEL), full((D_MODEL, nrow)), full((nt, D_MODEL)),
                  full((A_HEADS, A_KV_RANK, A_HEAD_DIM)), full((1, A_KV_RANK)), full((DEPTH + 1, B_FORGET))],
        out_specs=out_specs,
        out_shape=out_shape,
        compiler_params=pltpu.CompilerParams(dimension_semantics=("parallel",), vmem_limit_bytes=VMEM_LIMIT),
        name="proj",
    )(x2, wrow, wt, wukt, kv_g.reshape(1, A_KV_RANK).astype(F32), lb_logits.astype(F32))


def _chunk_of(pos):
    return lax.shift_right_logical(pos, int(math.log2(CHUNK)))


def _dsa_kernel(iqt_ref, iwt_ref, ik_ref, c_ref, ct_ref, qlt_ref, bias_ref, wuv_ref, o_ref,
                skey_ref, acc_ref, m_ref, l_ref, *, k_sel):
    j = pl.program_id(1)
    nblk = j + 1
    hl = A_HEADS * LANES

    iq = iqt_ref[0]
    rhs = jnp.concatenate([iq[h * IDX_DIM:(h + 1) * IDX_DIM, :] for h in range(IDX_HEADS)], axis=1)
    w = iwt_ref[0]
    row_iota = lax.broadcasted_iota(I32, (KEY_BLOCK, LANES), 0)
    lane_iota = lax.broadcasted_iota(I32, (KEY_BLOCK, LANES), 1)
    q_chunk = _chunk_of(j * Q_BLOCK + lane_iota)

    def admissible(kb):
        return _chunk_of(kb * KEY_BLOCK + row_iota) <= q_chunk

    def score_block(kb, carry):
        r0 = pl.multiple_of(kb * KEY_BLOCK, KEY_BLOCK)
        d = jnp.dot(ik_ref[0, pl.ds(r0, KEY_BLOCK), :], rhs, preferred_element_type=F32)
        sc = jnp.zeros((KEY_BLOCK, LANES), F32)
        for h in range(IDX_HEADS):
            sc = sc + w[h:h + 1, :] * jnp.maximum(d[:, h * LANES:(h + 1) * LANES], 0.0)
        bits = lax.bitcast_convert_type(sc, I32)
        key = jnp.where(bits < 0, bits ^ jnp.int32(0x7FFFFFFF), bits)
        key = jnp.where(sc == 0.0, 0, key)
        skey_ref[pl.ds(r0, KEY_BLOCK), :] = jnp.where(admissible(kb), key, INT_MIN)
        return carry

    lax.fori_loop(0, nblk, score_block, 0)

    @pl.when(nblk % 2 == 1)
    def _():
        r0 = pl.multiple_of(nblk * KEY_BLOCK, KEY_BLOCK)
        skey_ref[pl.ds(r0, KEY_BLOCK), :] = jnp.full((KEY_BLOCK, LANES), INT_MIN, I32)

    ntrip = (nblk + 1) // 2

    def count(pred):
        def body(p, acc):
            r0 = pl.multiple_of(p * COUNT_ROWS, COUNT_ROWS)
            ind = jnp.where(pred(skey_ref[pl.ds(r0, COUNT_ROWS), :]), 1, 0)
            return acc + jnp.sum(ind.reshape(COUNT_ROWS // 8, 8, LANES), axis=0)
        acc = lax.fori_loop(0, ntrip, body, jnp.zeros((8, LANES), I32))
        return jnp.sum(acc, axis=0, keepdims=True)

    def bit_step(i, t):
        cand = t + lax.shift_left(jnp.int32(1), 31 - i)
        return jnp.where(count(lambda blk: blk >= cand) >= k_sel, cand, t)

    thr = lax.fori_loop(0, 32, bit_step, jnp.full((1, LANES), INT_MIN, I32))

    need = (k_sel - count(lambda blk: blk > thr)).astype(F32)
    tri = jnp.where(row_iota >= lane_iota, 1.0, 0.0).astype(BF16)

    def select_block(kb, carry):
        r0 = pl.multiple_of(kb * KEY_BLOCK, KEY_BLOCK)
        blk = skey_ref[pl.ds(r0, KEY_BLOCK), :]
        eq = jnp.where(blk == thr, 1.0, 0.0)
        incl = jnp.dot(tri, eq.astype(BF16), preferred_element_type=F32)
        rank = incl - eq + carry
        sel = jnp.where(blk > thr, 1, jnp.where((blk == thr) & (rank < need), 1, 0))
        skey_ref[pl.ds(r0, KEY_BLOCK), :] = jnp.where(admissible(kb), sel, 0)
        return carry + incl[KEY_BLOCK - 1:KEY_BLOCK, :]

    lax.fori_loop(0, nblk, select_block, jnp.zeros((1, LANES), F32))

    qlt = jnp.concatenate([qlt_ref[0, h] for h in range(A_HEADS)], axis=1)
    m_ref[...] = jnp.full((1, hl), NEG_BIG, F32)
    l_ref[...] = jnp.zeros((1, hl), F32)
    acc_ref[...] = jnp.zeros((A_KV_RANK, hl), F32)

    def attend(kb, with_bias):
        r0 = pl.multiple_of(kb * KEY_BLOCK, KEY_BLOCK)
        s = jnp.dot(c_ref[0, pl.ds(r0, KEY_BLOCK), :], qlt, preferred_element_type=F32)
        if with_bias:
            s = s + bias_ref[j - kb]
        sel = jnp.concatenate([skey_ref[pl.ds(r0, KEY_BLOCK), :]] * A_HEADS, axis=1)
        s = jnp.where(sel != 0, s, NEG_BIG)
        m_old = m_ref[...]
        m_new = jnp.maximum(m_old, jnp.max(s, axis=0, keepdims=True))
        alpha = jnp.exp(m_old - m_new)
        p = jnp.exp(s - m_new)
        l_ref[...] = alpha * l_ref[...] + jnp.sum(p, axis=0, keepdims=True)
        pv = jnp.dot(ct_ref[0, kb], p.astype(BF16), preferred_element_type=F32)
        acc_ref[...] = acc_ref[...] * alpha + pv
        m_ref[...] = m_new

    def attend_far(kb, carry):
        attend(kb, False)
        return carry

    def attend_near(kb, carry):
        attend(kb, True)
        return carry

    near0 = jnp.maximum(j - 2, 0)
    lax.fori_loop(0, near0, attend_far, 0)
    lax.fori_loop(near0, nblk, attend_near, 0)

    ot = acc_ref[...] * (1.0 / l_ref[...])
    for h in range(A_HEADS):
        o_h = ot[:, h * LANES:(h + 1) * LANES].T
        oa = jnp.dot(o_h.astype(BF16), wuv_ref[h], preferred_element_type=F32)
        o_ref[0, :, h * A_HEAD_DIM:(h + 1) * A_HEAD_DIM] = oa.astype(BF16)


def _dsa(iqt, iwt, ik, c, ctb, qlt, bias, w_uv, bsz, seq):
    nq = seq // Q_BLOCK
    k_sel = min(TOPK_MAX, seq // 4)
    hl = A_HEADS * LANES
    return pl.pallas_call(
        functools.partial(_dsa_kernel, k_sel=k_sel),
        grid=(bsz, nq),
        in_specs=[
            pl.BlockSpec((1, IDX_HEADS * IDX_DIM, Q_BLOCK), lambda b, j: (b, 0, j)),
            pl.BlockSpec((1, IDX_HEADS, Q_BLOCK), lambda b, j: (b, 0, j)),
            pl.BlockSpec((1, seq, IDX_DIM), lambda b, j: (b, 0, 0)),
            pl.BlockSpec((1, seq, A_KV_RANK), lambda b, j: (b, 0, 0)),
            pl.BlockSpec((1, seq // KEY_BLOCK, A_KV_RANK, KEY_BLOCK), lambda b, j: (b, 0, 0, 0)),
            pl.BlockSpec((1, A_HEADS, A_KV_RANK, Q_BLOCK), lambda b, j: (b, 0, 0, j)),
            pl.BlockSpec((3, KEY_BLOCK, hl), lambda b, j: (0, 0, 0)),
            pl.BlockSpec((A_HEADS, A_KV_RANK, A_HEAD_DIM), lambda b, j: (0, 0, 0)),
        ],
        out_specs=pl.BlockSpec((1, Q_BLOCK, A_WIDTH), lambda b, j: (b, j, 0)),
        out_shape=jax.ShapeDtypeStruct((bsz, seq, A_WIDTH), BF16),
        scratch_shapes=[
            pltpu.VMEM((seq + KEY_BLOCK, LANES), I32),
            pltpu.VMEM((A_KV_RANK, hl), F32),
            pltpu.VMEM((1, hl), F32),
            pltpu.VMEM((1, hl), F32),
        ],
        compiler_params=pltpu.CompilerParams(dimension_semantics=("parallel", "arbitrary"),
                                             vmem_limit_bytes=VMEM_LIMIT),
        name="dsa",
    )(iqt, iwt, ik.reshape(bsz, seq, IDX_DIM), c.reshape(bsz, seq, A_KV_RANK), ctb, qlt, bias,
      w_uv.astype(BF16))


def _hgrn_kernel(qs_ref, logf_ref, v_ref, o_ref, st_ref):
    @pl.when(pl.program_id(1) == 0)
    def _():
        st_ref[...] = jnp.zeros(st_ref.shape, F32)

    steps = qs_ref.shape[1]
    dk, dv = B_KEY_DIM, B_VAL_DIM
    r64 = lax.broadcasted_iota(I32, (CHUNK, CHUNK), 0)
    c64 = lax.broadcasted_iota(I32, (CHUNK, CHUNK), 1)
    tri = jnp.where(r64 >= c64, 1.0, 0.0).astype(BF16)
    rt = lax.broadcasted_iota(I32, (CHUNK, LANES), 0)
    cs = lax.broadcasted_iota(I32, (CHUNK, LANES), 1)
    causal = cs <= rt
    nsub = CHUNK // SUB
    zpad_f = jnp.zeros((LANES - CHUNK, LANES), F32)

    for ci in range(steps // CHUNK):
        rows = slice(ci * CHUNK, (ci + 1) * CHUNK)
        lf_all = logf_ref[0, rows, :]
        hi = lf_all.astype(BF16)
        lo = (lf_all - hi.astype(F32)).astype(BF16)
        b_all = (jnp.dot(tri, hi, preferred_element_type=F32)
                 + jnp.dot(tri, lo, preferred_element_type=F32))
        for h in range(B_HEADS):
            cols = slice(h * dk, (h + 1) * dk)
            b = b_all[:, cols]
            kk = 1.0 - jnp.exp(lf_all[:, cols])
            q = qs_ref[0, rows, cols].astype(F32)
            v = v_ref[0, rows, h * dv:(h + 1) * dv]
            b_last = b[CHUNK - 1:CHUNK, :]
            starts = [jnp.zeros((1, dk), F32)] + [b[i * SUB - 1:i * SUB, :] for i in range(1, nsub)]
            ref_rows = jnp.concatenate([jnp.broadcast_to(s_, (SUB, dk)) for s_ in starts], axis=0)
            q_in = q * jnp.exp(b - ref_rows)
            q_st = q_in * jnp.exp(ref_rows)
            secs = []
            for i in range(nsub):
                n = SUB * (i + 1)
                e = jnp.exp(jnp.minimum(starts[i] - b[:n, :], EXP_CAP))
                ks = (kk[:n, :] * e).astype(BF16)
                secs.append(jnp.concatenate([ks, jnp.zeros((LANES - n, dk), BF16)], axis=0))
            kst = jnp.concatenate(secs, axis=0)
            r = lax.dot_general(q_in.astype(BF16), kst, (((1,), (1,)), ((), ())),
                                preferred_element_type=F32)
            a = jnp.concatenate([r[i * SUB:(i + 1) * SUB, i * LANES:(i + 1) * LANES] for i in range(nsub)],
                                axis=0)
            a = jnp.where(causal, a, 0.0).astype(BF16)
            v_pad = jnp.concatenate([v, jnp.zeros((LANES - CHUNK, dv), BF16)], axis=0)
            st = st_ref[h]
            o = (jnp.dot(a, v_pad, preferred_element_type=F32)
                 + lax.dot_general(q_st.astype(BF16), st.astype(BF16), (((1,), (1,)), ((), ())),
                                   preferred_element_type=F32))
            o_ref[0, rows, h * dv:(h + 1) * dv] = o.astype(BF16)
            k_hat = (kk * jnp.exp(b_last - b)).astype(BF16)
            k_pad = jnp.concatenate([k_hat, jnp.zeros((LANES - CHUNK, dk), BF16)], axis=0)
            v_t = jnp.concatenate([v.astype(F32), zpad_f], axis=0).T.astype(BF16)
            st_ref[h] = st * jnp.exp(b_last) + jnp.dot(v_t, k_pad, preferred_element_type=F32)


def _hgrn(qs, logf, v, bsz, seq):
    steps = min(2 * CHUNK, seq)
    spec = lambda n: pl.BlockSpec((1, steps, n), lambda b, s: (b, s, 0))
    return pl.pallas_call(
        _hgrn_kernel,
        grid=(bsz, seq // steps),
        in_specs=[spec(B_FORGET), spec(B_FORGET), spec(B_WIDTH)],
        out_specs=spec(B_WIDTH),
        out_shape=jax.ShapeDtypeStruct((bsz, seq, B_WIDTH), BF16),
        scratch_shapes=[pltpu.VMEM((B_HEADS, B_VAL_DIM, B_KEY_DIM), F32)],
        compiler_params=pltpu.CompilerParams(dimension_semantics=("parallel", "arbitrary"),
                                             vmem_limit_bytes=VMEM_LIMIT),
        name="hgrn",
    )(qs.reshape(bsz, seq, B_FORGET), logf.reshape(bsz, seq, B_FORGET), v.reshape(bsz, seq, B_WIDTH))


def _out_kernel(oa_ref, ga_ref, ob_ref, gb_ref, x_ref, wo_ref, hg_ref, lng_ref, lnb_ref, out_ref):
    oa = oa_ref[...].astype(F32) * ga_ref[...].astype(F32)
    ob = ob_ref[...].astype(F32)
    parts = []
    for h in range(B_HEADS):
        oh = ob[:, h * B_VAL_DIM:(h + 1) * B_VAL_DIM]
        ms = jnp.mean(oh * oh, axis=-1, keepdims=True)
        parts.append(oh * lax.rsqrt(ms + RMS_EPS) * hg_ref[:, h * B_VAL_DIM:(h + 1) * B_VAL_DIM])
    obn = jnp.concatenate(parts, axis=1) * gb_ref[...].astype(F32)
    z = jnp.concatenate([oa, obn], axis=1).astype(BF16)
    y = jnp.dot(z, wo_ref[...], preferred_element_type=F32)
    r = DEEPNORM_ALPHA * x_ref[...] + y
    mu = jnp.mean(r, axis=-1, keepdims=True)
    rc = r - mu
    var = jnp.mean(rc * rc, axis=-1, keepdims=True)
    out_ref[...] = rc * lax.rsqrt(var + LN_EPS) * lng_ref[...] + lnb_ref[...]


def _output(oa, ga, ob, gb, x2, w_o, hg, ln_g, ln_b):
    t = x2.shape[0]
    tq = min(PROJ_TILE, t)
    row = lambda n: pl.BlockSpec((tq, n), lambda i: (i, 0))
    full = lambda shape: pl.BlockSpec(shape, lambda i: (0,) * len(shape))
    d_mix = A_WIDTH + B_WIDTH
    return pl.pallas_call(
        _out_kernel,
        grid=(t // tq,),
        in_specs=[row(A_WIDTH), row(A_WIDTH), row(B_WIDTH), row(B_WIDTH), row(D_MODEL),
                  full((d_mix, D_MODEL)), full((1, B_WIDTH)), full((1, D_MODEL)), full((1, D_MODEL))],
        out_specs=row(D_MODEL),
        out_shape=jax.ShapeDtypeStruct((t, D_MODEL), F32),
        compiler_params=pltpu.CompilerParams(dimension_semantics=("parallel",), vmem_limit_bytes=VMEM_LIMIT),
        name="out",
    )(oa, ga, ob, gb, x2, w_o.astype(BF16), hg.reshape(1, B_WIDTH).astype(F32),
      ln_g.reshape(1, D_MODEL).astype(F32), ln_b.reshape(1, D_MODEL).astype(F32))


def kernel(x, w_in, w_uk, w_uv, kv_norm_g, rel_bias, lb_logits, hgrn_norm_g, w_o, ln_g, ln_b):
    bsz, seq, d = x.shape
    assert d == D_MODEL and seq % Q_BLOCK == 0 and DEPTH == 1
    t = bsz * seq
    x2 = x.reshape(t, d)
    bias = _bias_tiles(rel_bias)
    ik, c, ga, gb, qs, logf, v, iqt, iwt, qlt, ctb = _project(
        x2, w_in[0], w_uk[0], kv_norm_g[0], lb_logits, bsz, seq)
    oa = _dsa(iqt, iwt, ik, c, ctb, qlt, bias, w_uv[0], bsz, seq)
    ob = _hgrn(qs, logf, v, bsz, seq)
    out = _output(oa.reshape(t, A_WIDTH), ga, ob.reshape(t, B_WIDTH), gb, x2, w_o[0],
                  hgrn_norm_g[0], ln_g[0], ln_b[0])
    return out.reshape(bsz, seq, d)
```

```python
import functools
import math

import jax
import jax.numpy as jnp
from jax import lax
from jax.experimental import pallas as pl
from jax.experimental.pallas import tpu as pltpu

F32 = jnp.float32
BF16 = jnp.bfloat16
I32 = jnp.int32
I16 = jnp.int16

D_MODEL = 1024
DEPTH = 1
CHUNK = 64
Q_BLOCK = 128
A_HEADS = 4
A_HEAD_DIM = 128
A_WIDTH = A_HEADS * A_HEAD_DIM
A_KV_RANK = 128
IDX_HEADS = 8
IDX_DIM = 64
TOPK_MAX = 256
B_HEADS = 4
B_KEY_DIM = 128
B_VAL_DIM = 128
B_WIDTH = B_HEADS * B_VAL_DIM
B_FORGET = B_HEADS * B_KEY_DIM
REL_BUCKETS = 32
REL_MAX_DIST = 256
DEEPNORM_ALPHA = (2.0 * DEPTH) ** 0.25
LN_EPS = 1e-5
RMS_EPS = 1e-6

_SPLITS = (
    ('a_q', A_WIDTH), ('a_ckv', A_KV_RANK), ('a_iq', IDX_HEADS * IDX_DIM), ('a_ik', IDX_DIM),
    ('a_iw', IDX_HEADS), ('a_gate', A_WIDTH), ('b_q', B_FORGET), ('b_f', B_FORGET),
    ('b_i', B_WIDTH), ('b_gate', B_WIDTH),
)
_OFFSETS = {}
_off = 0
for _name, _n in _SPLITS:
    _OFFSETS[_name] = (_off, _n)
    _off += _n

LANES = 128
QW = 256
KG = 256
DSA_NB = 2
I16_ROWS = 16
SUB = 16
EXP_CAP = 80.0
NEG_BIG = -1e30
INT_MIN = -2 ** 31
PROJ_TILE = 512
VMEM_LIMIT = 56 * 1024 * 1024


def _col(w, name):
    o, n = _OFFSETS[name]
    return w[:, o:o + n]


def _t5_bucket(rel):
    nb = REL_BUCKETS // 2
    max_exact = nb // 2
    ret = jnp.where(rel > 0, nb, 0).astype(I32)
    n = jnp.abs(rel)
    nf = jnp.maximum(n, 1).astype(F32)
    large = max_exact + (jnp.log(nf / max_exact) / math.log(REL_MAX_DIST / max_exact)
                         * (nb - max_exact)).astype(I32)
    large = jnp.minimum(large, nb - 1)
    return ret + jnp.where(n < max_exact, n, large)


def _bias_kernel(rb_ref, out_ref):
    sl = lax.broadcasted_iota(I32, (KG, QW), 0)
    ql = lax.broadcasted_iota(I32, (KG, QW), 1)
    far = REL_BUCKETS // 2 - 1
    for d in range(2):
        bucket = _t5_bucket(sl - ql - KG * d)
        for h in range(A_HEADS):
            acc = jnp.zeros((KG, QW), F32)
            for bk in range(REL_BUCKETS):
                acc = jnp.where(bucket == bk, rb_ref[bk, h] - rb_ref[far, h], acc)
            out_ref[d, :, h * QW:(h + 1) * QW] = acc


def _bias_tiles(rel_bias):
    assert KG == QW and KG >= REL_MAX_DIST
    return pl.pallas_call(
        _bias_kernel,
        out_shape=jax.ShapeDtypeStruct((2, KG, A_HEADS * QW), F32),
        in_specs=[pl.BlockSpec(memory_space=pltpu.SMEM)],
        out_specs=pl.BlockSpec(memory_space=pltpu.VMEM),
        name="bias",
    )(rel_bias.astype(F32))


def _proj_kernel(x_ref, wrow_ref, wt_ref, wukt_ref, g_ref, lbl_ref,
                 ik_ref, c_ref, ga_ref, gb_ref, qs_ref, logf_ref, v_ref,
                 iqt_ref, iwt_ref, qlt_ref, ct_ref):
    tq = x_ref.shape[0]
    xb = x_ref[...].astype(BF16)

    def rowdot(lo, n):
        return jnp.dot(xb, wrow_ref[:, lo:lo + n], preferred_element_type=F32)

    h0 = rowdot(0, 256)
    a = h0[:, :A_KV_RANK]
    ms = jnp.mean(a * a, axis=-1, keepdims=True)
    c = a * lax.rsqrt(ms + RMS_EPS) * g_ref[...]
    c_ref[...] = c.astype(BF16)
    ct = c.T
    for i in range(tq // KG):
        ct_ref[0, i] = ct[:, i * KG:(i + 1) * KG].astype(BF16)
    ik_ref[...] = h0[:, A_KV_RANK:A_KV_RANK + IDX_DIM].astype(BF16)

    ga_ref[...] = jax.nn.silu(rowdot(256, A_WIDTH)).astype(BF16)
    qs_ref[...] = jax.nn.silu(rowdot(768, B_FORGET)).astype(BF16)
    lbl = lbl_ref[...]
    e = jnp.exp(lbl - jnp.max(lbl, axis=0, keepdims=True))
    lb = e[0:1, :] / jnp.sum(e, axis=0, keepdims=True)
    f = lb + (1.0 - lb) * jax.nn.sigmoid(rowdot(1280, B_FORGET))
    logf_ref[...] = jnp.log(f)
    v_ref[...] = rowdot(1792, B_WIDTH).astype(BF16)
    gb_ref[...] = jax.nn.silu(rowdot(2304, B_WIDTH)).astype(BF16)

    ht = lax.dot_general(wt_ref[...], xb, (((1,), (1,)), ((), ())), preferred_element_type=F32)
    iqt_ref[0] = ht[A_WIDTH:A_WIDTH + IDX_HEADS * IDX_DIM].astype(BF16)
    o = A_WIDTH + IDX_HEADS * IDX_DIM
    iwt_ref[0] = ht[o:o + IDX_HEADS] * (IDX_HEADS ** -0.5 * IDX_DIM ** -0.5)
    for h in range(A_HEADS):
        aq = ht[h * A_HEAD_DIM:(h + 1) * A_HEAD_DIM].astype(BF16)
        ql = jnp.dot(wukt_ref[h], aq, preferred_element_type=F32) * (A_HEAD_DIM ** -0.5)
        qlt_ref[0, h] = ql.astype(BF16)


def _project(x2, w_in, w_uk, kv_g, lb_logits, bsz, seq):
    t = bsz * seq
    tq = min(PROJ_TILE, seq)
    per_b = seq // tq
    pad64 = jnp.zeros((D_MODEL, 64), F32)
    wrow = jnp.concatenate([_col(w_in, 'a_ckv'), _col(w_in, 'a_ik'), pad64, _col(w_in, 'a_gate'),
                            _col(w_in, 'b_q'), _col(w_in, 'b_f'), _col(w_in, 'b_i'),
                            _col(w_in, 'b_gate')], axis=1).astype(BF16)
    pad8 = jnp.zeros((D_MODEL, 8), F32)
    wt = jnp.concatenate([_col(w_in, 'a_q'), _col(w_in, 'a_iq'), _col(w_in, 'a_iw'), pad8],
                         axis=1).T.astype(BF16)
    wukt = jnp.swapaxes(w_uk, 1, 2).astype(BF16)
    nrow, nt = wrow.shape[1], wt.shape[0]
    row = lambda n: pl.BlockSpec((tq, n), lambda i: (i, 0))
    full = lambda shape: pl.BlockSpec(shape, lambda i: (0,) * len(shape))
    out_shape = (
        jax.ShapeDtypeStruct((t, IDX_DIM), BF16),
        jax.ShapeDtypeStruct((t, A_KV_RANK), BF16),
        jax.ShapeDtypeStruct((t, A_WIDTH), BF16),
        jax.ShapeDtypeStruct((t, B_WIDTH), BF16),
        jax.ShapeDtypeStruct((t, B_FORGET), BF16),
        jax.ShapeDtypeStruct((t, B_FORGET), F32),
        jax.ShapeDtypeStruct((t, B_WIDTH), BF16),
        jax.ShapeDtypeStruct((bsz, IDX_HEADS * IDX_DIM, seq), BF16),
        jax.ShapeDtypeStruct((bsz, IDX_HEADS, seq), F32),
        jax.ShapeDtypeStruct((bsz, A_HEADS, A_KV_RANK, seq), BF16),
        jax.ShapeDtypeStruct((bsz, seq // KG, A_KV_RANK, KG), BF16),
    )
    out_specs = (
        row(IDX_DIM), row(A_KV_RANK), row(A_WIDTH), row(B_WIDTH), row(B_FORGET), row(B_FORGET), row(B_WIDTH),
        pl.BlockSpec((1, IDX_HEADS * IDX_DIM, tq), lambda i: (i // per_b, 0, i % per_b)),
        pl.BlockSpec((1, IDX_HEADS, tq), lambda i: (i // per_b, 0, i % per_b)),
        pl.BlockSpec((1, A_HEADS, A_KV_RANK, tq), lambda i: (i // per_b, 0, 0, i % per_b)),
        pl.BlockSpec((1, tq // KG, A_KV_RANK, KG), lambda i: (i // per_b, i % per_b, 0, 0)),
    )
    return pl.pallas_call(
        _proj_kernel,
        grid=(t // tq,),
        in_specs=[row(D_MODEL), full((D_MODEL, nrow)), full((nt, D_MODEL)),
                  full((A_HEADS, A_KV_RANK, A_HEAD_DIM)), full((1, A_KV_RANK)), full((DEPTH + 1, B_FORGET))],
        out_specs=out_specs,
        out_shape=out_shape,
        compiler_params=pltpu.CompilerParams(dimension_semantics=("parallel",), vmem_limit_bytes=VMEM_LIMIT),
        name="proj",
    )(x2, wrow, wt, wukt, kv_g.reshape(1, A_KV_RANK).astype(F32), lb_logits.astype(F32))


def _chunk_of(pos):
    return lax.shift_right_logical(pos, int(math.log2(CHUNK)))


def _tree_sum(parts):
    while len(parts) > 1:
        parts = [a + b for a, b in zip(parts[::2], parts[1::2])] + ([parts[-1]] if len(parts) % 2 else [])
    return parts[0]


def _dsa_kernel(iqt_ref, iwt_ref, ik_ref, c_ref, ct_ref, qlt_ref, bias_ref, wuv_ref, o_ref,
                key_ref, hi_ref, lo_ref, acc_ref, m_ref, l_ref, *, k_sel, nb):
    j = pl.program_id(1)
    ngrp = j + 1
    hq = A_HEADS * QW
    batches = range(nb)
    row_iota = lax.broadcasted_iota(I32, (KG, QW), 0)
    lane_iota = lax.broadcasted_iota(I32, (KG, QW), 1)
    q_chunk = _chunk_of(j * QW + lane_iota)

    def rows_of(g):
        return pl.ds(pl.multiple_of(g * KG, KG), KG)

    def score_group(g, carry):
        rows = rows_of(g)
        adm = _chunk_of(g * KG + row_iota) <= q_chunk
        for bi in batches:
            rhs = jnp.concatenate([iqt_ref[bi, h * IDX_DIM:(h + 1) * IDX_DIM, :] for h in range(IDX_HEADS)],
                                  axis=1)
            d = jnp.dot(ik_ref[bi, rows, :], rhs, preferred_element_type=F32)
            sc = jnp.zeros((KG, QW), F32)
            for h in range(IDX_HEADS):
                sc = sc + iwt_ref[bi, h:h + 1, :] * jnp.maximum(d[:, h * QW:(h + 1) * QW], 0.0)
            bits = lax.bitcast_convert_type(sc, I32)
            key = jnp.where(bits < 0, bits ^ jnp.int32(0x7FFFFFFF), bits)
            key = jnp.where(sc == 0.0, 0, key)
            key = jnp.where(adm, key, INT_MIN)
            key_ref[bi, rows, :] = key
            hi_ref[bi, rows, :] = (key >> 16).astype(I16)
            lo_ref[bi, rows, :] = ((key & 0xFFFF) - 32768).astype(I16)
        return carry

    lax.fori_loop(0, ngrp, score_group, 0)

    def count16(ref, cands):
        c16 = [c.astype(I16) for c in cands]

        def body(g, accs):
            rows = rows_of(g)
            out = []
            for bi in batches:
                blk = ref[bi, rows, :]
                ind = jnp.where(blk >= c16[bi], jnp.int16(1), jnp.int16(0))
                out.append(accs[bi] + _tree_sum([ind[r:r + I16_ROWS] for r in range(0, KG, I16_ROWS)]))
            return tuple(out)

        accs = lax.fori_loop(0, ngrp, body, tuple(jnp.zeros((I16_ROWS, QW), I16) for _ in batches))
        return [jnp.sum(a.astype(I32), axis=0, keepdims=True) for a in accs]

    def bisect16(ref, want, n_all):
        def step(i, carry):
            ts, n_ge, n_gt = carry
            cands = [t + lax.shift_left(jnp.int32(1), 15 - i) for t in ts]
            cnts = count16(ref, cands)
            ok = [cnts[bi] >= want[bi] for bi in batches]
            return (tuple(jnp.where(ok[bi], cands[bi], ts[bi]) for bi in batches),
                    tuple(jnp.where(ok[bi], cnts[bi], n_ge[bi]) for bi in batches),
                    tuple(jnp.where(ok[bi], n_gt[bi], cnts[bi]) for bi in batches))
        init = (tuple(jnp.full((1, QW), -32768, I32) for _ in batches), tuple(n_all),
                tuple(jnp.zeros((1, QW), I32) for _ in batches))
        return lax.fori_loop(0, 16, step, init)

    want_hi = [jnp.full((1, QW), k_sel, I32) for _ in batches]
    n_rows = [jnp.zeros((1, QW), I32) + ngrp * KG for _ in batches]
    p_hi, n_ge_hi, n_gt_hi = bisect16(hi_ref, want_hi, n_rows)

    def keep_ties(g, carry):
        rows = rows_of(g)
        for bi in batches:
            lo_ref[bi, rows, :] = jnp.where(hi_ref[bi, rows, :] == p_hi[bi].astype(I16), lo_ref[bi, rows, :],
                                            jnp.int16(-32768))
        return carry

    lax.fori_loop(0, ngrp, keep_ties, 0)
    want_lo = [k_sel - n_gt_hi[bi] for bi in batches]
    n_eq_hi = [n_ge_hi[bi] - n_gt_hi[bi] for bi in batches]
    p_lo, n_ge_lo, n_gt_lo = bisect16(lo_ref, want_lo, n_eq_hi)
    thr = [lax.shift_left(p_hi[bi], 16) + (p_lo[bi] + 32768) for bi in batches]

    excess = [(n_gt_hi[bi] + n_ge_lo[bi] > k_sel) & (thr[bi] > INT_MIN) for bi in batches]
    any_excess = jnp.max(_tree_sum([jnp.where(e, 1, 0) for e in excess]))

    @pl.when(any_excess > 0)
    def _():
        tri = jnp.where(row_iota >= lane_iota, 1.0, 0.0).astype(BF16)
        need = [(k_sel - n_gt_hi[bi] - n_gt_lo[bi]).astype(F32) for bi in batches]

        def demote(g, carry):
            rows = rows_of(g)
            out = []
            for bi in batches:
                blk = key_ref[bi, rows, :]
                eq = jnp.where(blk == thr[bi], 1.0, 0.0)
                incl = jnp.dot(tri, eq.astype(BF16), preferred_element_type=F32)
                rank = incl - eq + carry[bi]
                key_ref[bi, rows, :] = jnp.where((blk == thr[bi]) & (rank >= need[bi]), INT_MIN, blk)
                out.append(carry[bi] + incl[KG - 1:KG, :])
            return tuple(out)

        lax.fori_loop(0, ngrp, demote, tuple(jnp.zeros((1, QW), F32) for _ in batches))

    thr_sel = [jnp.maximum(t, INT_MIN + 1) for t in thr]
    for bi in batches:
        m_ref[bi] = jnp.full((1, hq), NEG_BIG, F32)
        l_ref[bi] = jnp.zeros((1, hq), F32)
        acc_ref[bi] = jnp.zeros((A_KV_RANK, hq), F32)

    def attend(g, with_bias):
        rows = rows_of(g)
        for bi in batches:
            qlt = jnp.concatenate([qlt_ref[bi, h] for h in range(A_HEADS)], axis=1)
            s = jnp.dot(c_ref[bi, rows, :], qlt, preferred_element_type=F32)
            if with_bias:
                s = s + bias_ref[j - g]
            sel = jnp.concatenate([key_ref[bi, rows, :]] * A_HEADS, axis=1)
            thr4 = jnp.concatenate([thr_sel[bi]] * A_HEADS, axis=1)
            s = jnp.where(sel >= thr4, s, NEG_BIG)
            m_old = m_ref[bi]
            m_new = jnp.maximum(m_old, jnp.max(s, axis=0, keepdims=True))
            alpha = jnp.exp(m_old - m_new)
            p = jnp.exp(s - m_new)
            l_ref[bi] = alpha * l_ref[bi] + jnp.sum(p, axis=0, keepdims=True)
            pv = jnp.dot(ct_ref[bi, g], p.astype(BF16), preferred_element_type=F32)
            acc_ref[bi] = acc_ref[bi] * alpha + pv
            m_ref[bi] = m_new

    def attend_far(g, carry):
        attend(g, False)
        return carry

    def attend_near(g, carry):
        attend(g, True)
        return carry

    near0 = jnp.maximum(j - 1, 0)
    lax.fori_loop(0, near0, attend_far, 0)
    lax.fori_loop(near0, ngrp, attend_near, 0)

    for bi in batches:
        ot = acc_ref[bi] * (1.0 / l_ref[bi])
        for h in range(A_HEADS):
            o_h = ot[:, h * QW:(h + 1) * QW].T
            oa = jnp.dot(o_h.astype(BF16), wuv_ref[h], preferred_element_type=F32)
            o_ref[bi, :, h * A_HEAD_DIM:(h + 1) * A_HEAD_DIM] = oa.astype(BF16)


def _dsa(iqt, iwt, ik, c, ctb, qlt, bias, w_uv, bsz, seq):
    assert seq % QW == 0 and QW % Q_BLOCK == 0
    nb = DSA_NB if bsz % DSA_NB == 0 else 1
    k_sel = min(TOPK_MAX, seq // 4)
    hq = A_HEADS * QW
    return pl.pallas_call(
        functools.partial(_dsa_kernel, k_sel=k_sel, nb=nb),
        grid=(bsz // nb, seq // QW),
        in_specs=[
            pl.BlockSpec((nb, IDX_HEADS * IDX_DIM, QW), lambda b, j: (b, 0, j)),
            pl.BlockSpec((nb, IDX_HEADS, QW), lambda b, j: (b, 0, j)),
            pl.BlockSpec((nb, seq, IDX_DIM), lambda b, j: (b, 0, 0)),
            pl.BlockSpec((nb, seq, A_KV_RANK), lambda b, j: (b, 0, 0)),
            pl.BlockSpec((nb, seq // KG, A_KV_RANK, KG), lambda b, j: (b, 0, 0, 0)),
            pl.BlockSpec((nb, A_HEADS, A_KV_RANK, QW), lambda b, j: (b, 0, 0, j)),
            pl.BlockSpec((2, KG, hq), lambda b, j: (0, 0, 0)),
            pl.BlockSpec((A_HEADS, A_KV_RANK, A_HEAD_DIM), lambda b, j: (0, 0, 0)),
        ],
        out_specs=pl.BlockSpec((nb, QW, A_WIDTH), lambda b, j: (b, j, 0)),
        out_shape=jax.ShapeDtypeStruct((bsz, seq, A_WIDTH), BF16),
        scratch_shapes=[
            pltpu.VMEM((nb, seq, QW), I32),
            pltpu.VMEM((nb, seq, QW), I16),
            pltpu.VMEM((nb, seq, QW), I16),
            pltpu.VMEM((nb, A_KV_RANK, hq), F32),
            pltpu.VMEM((nb, 1, hq), F32),
            pltpu.VMEM((nb, 1, hq), F32),
        ],
        compiler_params=pltpu.CompilerParams(dimension_semantics=("parallel", "arbitrary"),
                                             vmem_limit_bytes=VMEM_LIMIT),
        name="dsa",
    )(iqt, iwt, ik.reshape(bsz, seq, IDX_DIM), c.reshape(bsz, seq, A_KV_RANK), ctb, qlt, bias,
      w_uv.astype(BF16))


def _hgrn_kernel(qs_ref, logf_ref, v_ref, o_ref, st_ref):
    @pl.when(pl.program_id(1) == 0)
    def _():
        st_ref[...] = jnp.zeros(st_ref.shape, F32)

    steps = qs_ref.shape[1]
    dk, dv = B_KEY_DIM, B_VAL_DIM
    r64 = lax.broadcasted_iota(I32, (CHUNK, CHUNK), 0)
    c64 = lax.broadcasted_iota(I32, (CHUNK, CHUNK), 1)
    tri = jnp.where(r64 >= c64, 1.0, 0.0).astype(BF16)
    rt = lax.broadcasted_iota(I32, (CHUNK, LANES), 0)
    cs = lax.broadcasted_iota(I32, (CHUNK, LANES), 1)
    causal = cs <= rt
    nsub = CHUNK // SUB
    zpad_f = jnp.zeros((LANES - CHUNK, LANES), F32)

    for ci in range(steps // CHUNK):
        rows = slice(ci * CHUNK, (ci + 1) * CHUNK)
        lf_all = logf_ref[0, rows, :]
        hi = lf_all.astype(BF16)
        lo = (lf_all - hi.astype(F32)).astype(BF16)
        b_all = (jnp.dot(tri, hi, preferred_element_type=F32)
                 + jnp.dot(tri, lo, preferred_element_type=F32))
        for h in range(B_HEADS):
            cols = slice(h * dk, (h + 1) * dk)
            b = b_all[:, cols]
            kk = 1.0 - jnp.exp(lf_all[:, cols])
            q = qs_ref[0, rows, cols].astype(F32)
            v = v_ref[0, rows, h * dv:(h + 1) * dv]
            b_last = b[CHUNK - 1:CHUNK, :]
            starts = [jnp.zeros((1, dk), F32)] + [b[i * SUB - 1:i * SUB, :] for i in range(1, nsub)]
            ref_rows = jnp.concatenate([jnp.broadcast_to(s_, (SUB, dk)) for s_ in starts], axis=0)
            q_in = q * jnp.exp(b - ref_rows)
            q_st = q_in * jnp.exp(ref_rows)
            secs = []
            for i in range(nsub):
                n = SUB * (i + 1)
                e = jnp.exp(jnp.minimum(starts[i] - b[:n, :], EXP_CAP))
                ks = (kk[:n, :] * e).astype(BF16)
                secs.append(jnp.concatenate([ks, jnp.zeros((LANES - n, dk), BF16)], axis=0))
            kst = jnp.concatenate(secs, axis=0)
            r = lax.dot_general(q_in.astype(BF16), kst, (((1,), (1,)), ((), ())),
                                preferred_element_type=F32)
            a = jnp.concatenate([r[i * SUB:(i + 1) * SUB, i * LANES:(i + 1) * LANES] for i in range(nsub)],
                                axis=0)
            a = jnp.where(causal, a, 0.0).astype(BF16)
            v_pad = jnp.concatenate([v, jnp.zeros((LANES - CHUNK, dv), BF16)], axis=0)
            st = st_ref[h]
            o = (jnp.dot(a, v_pad, preferred_element_type=F32)
                 + lax.dot_general(q_st.astype(BF16), st.astype(BF16), (((1,), (1,)), ((), ())),
                                   preferred_element_type=F32))
            o_ref[0, rows, h * dv:(h + 1) * dv] = o.astype(BF16)
            k_hat = (kk * jnp.exp(b_last - b)).astype(BF16)
            k_pad = jnp.concatenate([k_hat, jnp.zeros((LANES - CHUNK, dk), BF16)], axis=0)
            v_t = jnp.concatenate([v.astype(F32), zpad_f], axis=0).T.astype(BF16)
            st_ref[h] = st * jnp.exp(b_last) + jnp.dot(v_t, k_pad, preferred_element_type=F32)


def _hgrn(qs, logf, v, bsz, seq):
    steps = min(2 * CHUNK, seq)
    spec = lambda n: pl.BlockSpec((1, steps, n), lambda b, s: (b, s, 0))
    return pl.pallas_call(
        _hgrn_kernel,
        grid=(bsz, seq // steps),
        in_specs=[spec(B_FORGET), spec(B_FORGET), spec(B_WIDTH)],
        out_specs=spec(B_WIDTH),
        out_shape=jax.ShapeDtypeStruct((bsz, seq, B_WIDTH), BF16),
        scratch_shapes=[pltpu.VMEM((B_HEADS, B_VAL_DIM, B_KEY_DIM), F32)],
        compiler_params=pltpu.CompilerParams(dimension_semantics=("parallel", "arbitrary"),
                                             vmem_limit_bytes=VMEM_LIMIT),
        name="hgrn",
    )(qs.reshape(bsz, seq, B_FORGET), logf.reshape(bsz, seq, B_FORGET), v.reshape(bsz, seq, B_WIDTH))


def _out_kernel(oa_ref, ga_ref, ob_ref, gb_ref, x_ref, wo_ref, hg_ref, lng_ref, lnb_ref, out_ref):
    oa = oa_ref[...].astype(F32) * ga_ref[...].astype(F32)
    ob = ob_ref[...].astype(F32)
    parts = []
    for h in range(B_HEADS):
        oh = ob[:, h * B_VAL_DIM:(h + 1) * B_VAL_DIM]
        ms = jnp.mean(oh * oh, axis=-1, keepdims=True)
        parts.append(oh * lax.rsqrt(ms + RMS_EPS) * hg_ref[:, h * B_VAL_DIM:(h + 1) * B_VAL_DIM])
    obn = jnp.concatenate(parts, axis=1) * gb_ref[...].astype(F32)
    z = jnp.concatenate([oa, obn], axis=1).astype(BF16)
    y = jnp.dot(z, wo_ref[...], preferred_element_type=F32)
    r = DEEPNORM_ALPHA * x_ref[...] + y
    mu = jnp.mean(r, axis=-1, keepdims=True)
    rc = r - mu
    var = jnp.mean(rc * rc, axis=-1, keepdims=True)
    out_ref[...] = rc * lax.rsqrt(var + LN_EPS) * lng_ref[...] + lnb_ref[...]


def _output(oa, ga, ob, gb, x2, w_o, hg, ln_g, ln_b):
    t = x2.shape[0]
    tq = min(PROJ_TILE, t)
    row = lambda n: pl.BlockSpec((tq, n), lambda i: (i, 0))
    full = lambda shape: pl.BlockSpec(shape, lambda i: (0,) * len(shape))
    d_mix = A_WIDTH + B_WIDTH
    return pl.pallas_call(
        _out_kernel,
        grid=(t // tq,),
        in_specs=[row(A_WIDTH), row(A_WIDTH), row(B_WIDTH), row(B_WIDTH), row(D_MODEL),
                  full((d_mix, D_MODEL)), full((1, B_WIDTH)), full((1, D_MODEL)), full((1, D_MODEL))],
        out_specs=row(D_MODEL),
        out_shape=jax.ShapeDtypeStruct((t, D_MODEL), F32),
        compiler_params=pltpu.CompilerParams(dimension_semantics=("parallel",), vmem_limit_bytes=VMEM_LIMIT),
        name="out",
    )(oa, ga, ob, gb, x2, w_o.astype(BF16), hg.reshape(1, B_WIDTH).astype(F32),
      ln_g.reshape(1, D_MODEL).astype(F32), ln_b.reshape(1, D_MODEL).astype(F32))


def kernel(x, w_in, w_uk, w_uv, kv_norm_g, rel_bias, lb_logits, hgrn_norm_g, w_o, ln_g, ln_b):
    bsz, seq, d = x.shape
    assert d == D_MODEL and seq % Q_BLOCK == 0 and DEPTH == 1
    t = bsz * seq
    x2 = x.reshape(t, d)
    bias = _bias_tiles(rel_bias)
    ik, c, ga, gb, qs, logf, v, iqt, iwt, qlt, ctb = _project(
        x2, w_in[0], w_uk[0], kv_norm_g[0], lb_logits, bsz, seq)
    oa = _dsa(iqt, iwt, ik, c, ctb, qlt, bias, w_uv[0], bsz, seq)
    ob = _hgrn(qs, logf, v, bsz, seq)
    out = _output(oa.reshape(t, A_WIDTH), ga, ob.reshape(t, B_WIDTH), gb, x2, w_o[0],
                  hgrn_norm_g[0], ln_g[0], ln_b[0])
    return out.reshape(bsz, seq, d)
```

```python
import functools
import math

import jax
import jax.numpy as jnp
from jax import lax
from jax.experimental import pallas as pl
from jax.experimental.pallas import tpu as pltpu

F32 = jnp.float32
BF16 = jnp.bfloat16
I32 = jnp.int32
I16 = jnp.int16

D_MODEL = 1024
DEPTH = 1
CHUNK = 64
Q_BLOCK = 128
A_HEADS = 4
A_HEAD_DIM = 128
A_WIDTH = A_HEADS * A_HEAD_DIM
A_KV_RANK = 128
IDX_HEADS = 8
IDX_DIM = 64
TOPK_MAX = 256
B_HEADS = 4
B_KEY_DIM = 128
B_VAL_DIM = 128
B_WIDTH = B_HEADS * B_VAL_DIM
B_FORGET = B_HEADS * B_KEY_DIM
REL_BUCKETS = 32
REL_MAX_DIST = 256
DEEPNORM_ALPHA = (2.0 * DEPTH) ** 0.25
LN_EPS = 1e-5
RMS_EPS = 1e-6

_SPLITS = (
    ('a_q', A_WIDTH), ('a_ckv', A_KV_RANK), ('a_iq', IDX_HEADS * IDX_DIM), ('a_ik', IDX_DIM),
    ('a_iw', IDX_HEADS), ('a_gate', A_WIDTH), ('b_q', B_FORGET), ('b_f', B_FORGET),
    ('b_i', B_WIDTH), ('b_gate', B_WIDTH),
)
_OFFSETS = {}
_off = 0
for _name, _n in _SPLITS:
    _OFFSETS[_name] = (_off, _n)
    _off += _n

LANES = 128
QW = 256
KG = 256
DSA_NB = 2
I16_ROWS = 16
SUB = 16
HGRN_STEPS = 128
HGRN_NB = 2
EXP_CAP = 80.0
NEG_BIG = -1e30
LOG2_E = math.log2(math.e)
INT_MIN = -2 ** 31
PROJ_TILE = 512
VMEM_LIMIT = 56 * 1024 * 1024


def _col(w, name):
    o, n = _OFFSETS[name]
    return w[:, o:o + n]


def _t5_bucket(rel):
    nb = REL_BUCKETS // 2
    max_exact = nb // 2
    ret = jnp.where(rel > 0, nb, 0).astype(I32)
    n = jnp.abs(rel)
    nf = jnp.maximum(n, max_exact).astype(F32)
    large = max_exact + jnp.floor(jnp.log(nf / max_exact) / math.log(REL_MAX_DIST / max_exact)
                                  * (nb - max_exact)).astype(I32)
    large = jnp.minimum(large, nb - 1)
    return ret + jnp.where(n < max_exact, n, large)


def _bias_kernel(rb_ref, out_ref):
    sl = lax.broadcasted_iota(I32, (KG, QW), 0)
    ql = lax.broadcasted_iota(I32, (KG, QW), 1)
    far = REL_BUCKETS // 2 - 1
    for d in range(2):
        bucket = _t5_bucket(sl - ql - KG * d)
        for h in range(A_HEADS):
            acc = jnp.zeros((KG, QW), F32)
            for bk in range(REL_BUCKETS):
                acc = jnp.where(bucket == bk, (rb_ref[bk, h] - rb_ref[far, h]) * LOG2_E, acc)
            out_ref[d, :, h * QW:(h + 1) * QW] = acc


def _bias_tiles(rel_bias):
    assert KG == QW and KG >= REL_MAX_DIST
    return pl.pallas_call(
        _bias_kernel,
        out_shape=jax.ShapeDtypeStruct((2, KG, A_HEADS * QW), F32),
        in_specs=[pl.BlockSpec(memory_space=pltpu.SMEM)],
        out_specs=pl.BlockSpec(memory_space=pltpu.VMEM),
        name="bias",
    )(rel_bias.astype(F32))


def _proj_kernel(x_ref, wrow_ref, wt_ref, wukt_ref, g_ref, lbl_ref,
                 ik_ref, c_ref, ga_ref, gb_ref, qs_ref, logf_ref, v_ref,
                 iqt_ref, iwt_ref, qlt_ref, ct_ref):
    tq = x_ref.shape[0]
    xb = x_ref[...].astype(BF16)

    def rowdot(lo, n):
        return jnp.dot(xb, wrow_ref[:, lo:lo + n], preferred_element_type=F32)

    h0 = rowdot(0, 256)
    a = h0[:, :A_KV_RANK]
    ms = jnp.mean(a * a, axis=-1, keepdims=True)
    c = a * lax.rsqrt(ms + RMS_EPS) * g_ref[...]
    c_ref[...] = c.astype(BF16)
    ct = c.T
    for i in range(tq // KG):
        ct_ref[0, i] = ct[:, i * KG:(i + 1) * KG].astype(BF16)
    ik_ref[...] = h0[:, A_KV_RANK:A_KV_RANK + IDX_DIM].astype(BF16)

    ga_ref[...] = jax.nn.silu(rowdot(256, A_WIDTH)).astype(BF16)
    qs_ref[...] = jax.nn.silu(rowdot(768, B_FORGET)).astype(BF16)
    lbl = lbl_ref[...]
    e = jnp.exp(lbl - jnp.max(lbl, axis=0, keepdims=True))
    lb = e[0:1, :] / jnp.sum(e, axis=0, keepdims=True)
    f = lb + (1.0 - lb) * jax.nn.sigmoid(rowdot(1280, B_FORGET))
    logf_ref[...] = jnp.log(f)
    v_ref[...] = rowdot(1792, B_WIDTH).astype(BF16)
    gb_ref[...] = jax.nn.silu(rowdot(2304, B_WIDTH)).astype(BF16)

    ht = lax.dot_general(wt_ref[...], xb, (((1,), (1,)), ((), ())), preferred_element_type=F32)
    iqt_ref[0] = ht[A_WIDTH:A_WIDTH + IDX_HEADS * IDX_DIM].astype(BF16)
    o = A_WIDTH + IDX_HEADS * IDX_DIM
    iwt_ref[0] = ht[o:o + IDX_HEADS] * (IDX_HEADS ** -0.5 * IDX_DIM ** -0.5)
    for h in range(A_HEADS):
        aq = ht[h * A_HEAD_DIM:(h + 1) * A_HEAD_DIM].astype(BF16)
        ql = jnp.dot(wukt_ref[h], aq, preferred_element_type=F32) * (A_HEAD_DIM ** -0.5 * LOG2_E)
        qlt_ref[0, h] = ql.astype(BF16)


def _project(x2, w_in, w_uk, kv_g, lb_logits, bsz, seq):
    t = bsz * seq
    tq = min(PROJ_TILE, seq)
    per_b = seq // tq
    pad64 = jnp.zeros((D_MODEL, 64), F32)
    wrow = jnp.concatenate([_col(w_in, 'a_ckv'), _col(w_in, 'a_ik'), pad64, _col(w_in, 'a_gate'),
                            _col(w_in, 'b_q'), _col(w_in, 'b_f'), _col(w_in, 'b_i'),
                            _col(w_in, 'b_gate')], axis=1).astype(BF16)
    pad8 = jnp.zeros((D_MODEL, 8), F32)
    wt = jnp.concatenate([_col(w_in, 'a_q'), _col(w_in, 'a_iq'), _col(w_in, 'a_iw'), pad8],
                         axis=1).T.astype(BF16)
    wukt = jnp.swapaxes(w_uk, 1, 2).astype(BF16)
    nrow, nt = wrow.shape[1], wt.shape[0]
    row = lambda n: pl.BlockSpec((tq, n), lambda i: (i, 0))
    full = lambda shape: pl.BlockSpec(shape, lambda i: (0,) * len(shape))
    out_shape = (
        jax.ShapeDtypeStruct((t, IDX_DIM), BF16),
        jax.ShapeDtypeStruct((t, A_KV_RANK), BF16),
        jax.ShapeDtypeStruct((t, A_WIDTH), BF16),
        jax.ShapeDtypeStruct((t, B_WIDTH), BF16),
        jax.ShapeDtypeStruct((t, B_FORGET), BF16),
        jax.ShapeDtypeStruct((t, B_FORGET), F32),
        jax.ShapeDtypeStruct((t, B_WIDTH), BF16),
        jax.ShapeDtypeStruct((bsz, IDX_HEADS * IDX_DIM, seq), BF16),
        jax.ShapeDtypeStruct((bsz, IDX_HEADS, seq), F32),
        jax.ShapeDtypeStruct((bsz, A_HEADS, A_KV_RANK, seq), BF16),
        jax.ShapeDtypeStruct((bsz, seq // KG, A_KV_RANK, KG), BF16),
    )
    out_specs = (
        row(IDX_DIM), row(A_KV_RANK), row(A_WIDTH), row(B_WIDTH), row(B_FORGET), row(B_FORGET), row(B_WIDTH),
        pl.BlockSpec((1, IDX_HEADS * IDX_DIM, tq), lambda i: (i // per_b, 0, i % per_b)),
        pl.BlockSpec((1, IDX_HEADS, tq), lambda i: (i // per_b, 0, i % per_b)),
        pl.BlockSpec((1, A_HEADS, A_KV_RANK, tq), lambda i: (i // per_b, 0, 0, i % per_b)),
        pl.BlockSpec((1, tq // KG, A_KV_RANK, KG), lambda i: (i // per_b, i % per_b, 0, 0)),
    )
    return pl.pallas_call(
        _proj_kernel,
        grid=(t // tq,),
        in_specs=[row(D_MODEL), full((D_MODEL, nrow)), full((nt, D_MODEL)),
                  full((A_HEADS, A_KV_RANK, A_HEAD_DIM)), full((1, A_KV_RANK)), full((DEPTH + 1, B_FORGET))],
        out_specs=out_specs,
        out_shape=out_shape,
        compiler_params=pltpu.CompilerParams(dimension_semantics=("parallel",), vmem_limit_bytes=VMEM_LIMIT),
        name="proj",
    )(x2, wrow, wt, wukt, kv_g.reshape(1, A_KV_RANK).astype(F32), lb_logits.astype(F32))


def _chunk_of(pos):
    return lax.shift_right_logical(pos, int(math.log2(CHUNK)))


def _tree_sum(parts):
    while len(parts) > 1:
        parts = [a + b for a, b in zip(parts[::2], parts[1::2])] + ([parts[-1]] if len(parts) % 2 else [])
    return parts[0]


def _order_key(bits):
    return jnp.where(bits < 0, jnp.int32(INT_MIN) - bits, bits)


def _dsa_kernel(iqt_ref, iwt_ref, ik_ref, c_ref, ct_ref, qlt_ref, bias_ref, wuv_ref, o_ref,
                sc_ref, hi_ref, lo_ref, acc_ref, m_ref, l_ref, thr_ref, nge_ref, ngt_ref, *, k_sel, nb):
    j = pl.program_id(1)
    ngrp = j + 1
    hq = A_HEADS * QW
    batches = range(nb)
    row_iota = lax.broadcasted_iota(I32, (KG, QW), 0)
    lane_iota = lax.broadcasted_iota(I32, (KG, QW), 1)
    q_chunk = _chunk_of(j * QW + lane_iota)

    def rows_of(g):
        return pl.ds(pl.multiple_of(g * KG, KG), KG)

    def score_group(g, diagonal):
        rows = rows_of(g)
        for bi in batches:
            rhs = jnp.concatenate([iqt_ref[bi, h * IDX_DIM:(h + 1) * IDX_DIM, :] for h in range(IDX_HEADS)],
                                  axis=1)
            d = jnp.dot(ik_ref[bi, rows, :], rhs, preferred_element_type=F32)
            sc = iwt_ref[bi, 0:1, :] * jnp.maximum(d[:, :QW], 0.0)
            for h in range(1, IDX_HEADS):
                sc = sc + iwt_ref[bi, h:h + 1, :] * jnp.maximum(d[:, h * QW:(h + 1) * QW], 0.0)
            if diagonal:
                sc = jnp.where(_chunk_of(g * KG + row_iota) <= q_chunk, sc, -jnp.inf)
            sc_ref[bi, rows, :] = sc
            key = _order_key(lax.bitcast_convert_type(sc, I32))
            hi_ref[bi, rows, :] = (key >> 16).astype(I16)
            lo_ref[bi, rows, :] = (key ^ 0x8000).astype(I16)

    def score_inner(g, carry):
        score_group(g, False)
        return carry

    lax.fori_loop(0, j, score_inner, 0)
    score_group(j, True)

    def count16(ref, cands):
        c16 = [c.astype(I16) for c in cands]

        def body(g, accs):
            rows = rows_of(g)
            out = []
            for bi in batches:
                blk = ref[bi, rows, :]
                ind = jnp.where(blk >= c16[bi], jnp.int16(1), jnp.int16(0))
                out.append(accs[bi] + _tree_sum([ind[r:r + I16_ROWS] for r in range(0, KG, I16_ROWS)]))
            return tuple(out)

        accs = lax.fori_loop(0, ngrp, body, tuple(jnp.zeros((I16_ROWS, QW), I16) for _ in batches))
        return [jnp.sum(a.astype(I32), axis=0, keepdims=True) for a in accs]

    def bisect16(ref, want, n_all):
        def step(i, carry):
            ts, n_ge, n_gt = carry
            cands = [t + lax.shift_left(jnp.int32(1), 15 - i) for t in ts]
            cnts = count16(ref, cands)
            ok = [cnts[bi] >= want[bi] for bi in batches]
            return (tuple(jnp.where(ok[bi], cands[bi], ts[bi]) for bi in batches),
                    tuple(jnp.where(ok[bi], cnts[bi], n_ge[bi]) for bi in batches),
                    tuple(jnp.where(ok[bi], n_gt[bi], cnts[bi]) for bi in batches))
        init = (tuple(jnp.full((1, QW), -32768, I32) for _ in batches), tuple(n_all),
                tuple(jnp.zeros((1, QW), I32) for _ in batches))
        return lax.fori_loop(0, 16, step, init)

    want_hi = [jnp.full((1, QW), k_sel, I32) for _ in batches]
    n_rows = [jnp.zeros((1, QW), I32) + ngrp * KG for _ in batches]
    p_hi, n_ge_hi, n_gt_hi = bisect16(hi_ref, want_hi, n_rows)

    def keep_ties(g, carry):
        rows = rows_of(g)
        for bi in batches:
            lo_ref[bi, rows, :] = jnp.where(hi_ref[bi, rows, :] == p_hi[bi].astype(I16), lo_ref[bi, rows, :],
                                            jnp.int16(-32768))
        return carry

    lax.fori_loop(0, ngrp, keep_ties, 0)
    want_lo = [k_sel - n_gt_hi[bi] for bi in batches]
    n_eq_hi = [n_ge_hi[bi] - n_gt_hi[bi] for bi in batches]
    p_lo, n_ge_lo, n_gt_lo = bisect16(lo_ref, want_lo, n_eq_hi)
    for bi in batches:
        thr_ref[bi] = lax.shift_left(p_hi[bi], 16) + (p_lo[bi] + 32768)
        nge_ref[bi] = n_gt_hi[bi] + n_ge_lo[bi]
        ngt_ref[bi] = n_gt_hi[bi] + n_gt_lo[bi]

    def as_float(key):
        return lax.bitcast_convert_type(_order_key(key), F32)

    def count32(cands):
        def body(g, accs):
            rows = rows_of(g)
            out = []
            for bi in batches:
                ind = jnp.where(sc_ref[bi, rows, :] >= cands[bi], 1, 0)
                out.append(accs[bi] + jnp.sum(ind.reshape(KG // 8, 8, QW), axis=0))
            return tuple(out)

        accs = lax.fori_loop(0, ngrp, body, tuple(jnp.zeros((8, QW), I32) for _ in batches))
        return [jnp.sum(a, axis=0, keepdims=True) for a in accs]

    n_ge_f = count32([as_float(thr_ref[bi]) for bi in batches])
    differs = _tree_sum([jnp.where(n_ge_f[bi] != nge_ref[bi], 1, 0) for bi in batches])

    @pl.when(jnp.max(differs) > 0)
    def _():
        def step(i, carry):
            ts, n_ge, n_gt = carry
            cands = [t + lax.shift_left(jnp.int32(1), 31 - i) for t in ts]
            cnts = count32([as_float(c) for c in cands])
            ok = [cnts[bi] >= k_sel for bi in batches]
            return (tuple(jnp.where(ok[bi], cands[bi], ts[bi]) for bi in batches),
                    tuple(jnp.where(ok[bi], cnts[bi], n_ge[bi]) for bi in batches),
                    tuple(jnp.where(ok[bi], n_gt[bi], cnts[bi]) for bi in batches))
        init = (tuple(jnp.full((1, QW), INT_MIN, I32) for _ in batches), tuple(n_rows),
                tuple(jnp.zeros((1, QW), I32) for _ in batches))
        ts, n_ge, n_gt = lax.fori_loop(0, 32, step, init)
        for bi in batches:
            thr_ref[bi] = ts[bi]
            nge_ref[bi] = n_ge[bi]
            ngt_ref[bi] = n_gt[bi]

    thr = [as_float(thr_ref[bi]) for bi in batches]

    excess = _tree_sum([jnp.where((nge_ref[bi] > k_sel) & (thr[bi] > -jnp.inf), 1, 0) for bi in batches])

    @pl.when(jnp.max(excess) > 0)
    def _():
        tri = jnp.where(row_iota >= lane_iota, 1.0, 0.0).astype(BF16)
        need = [(k_sel - ngt_ref[bi]).astype(F32) for bi in batches]

        def demote(g, carry):
            rows = rows_of(g)
            out = []
            for bi in batches:
                blk = sc_ref[bi, rows, :]
                eq = jnp.where(blk == thr[bi], 1.0, 0.0)
                incl = jnp.dot(tri, eq.astype(BF16), preferred_element_type=F32)
                rank = incl - eq + carry[bi]
                sc_ref[bi, rows, :] = jnp.where((blk == thr[bi]) & (rank >= need[bi]), -jnp.inf, blk)
                out.append(carry[bi] + incl[KG - 1:KG, :])
            return tuple(out)

        lax.fori_loop(0, ngrp, demote, tuple(jnp.zeros((1, QW), F32) for _ in batches))

    thr_sel = [jnp.maximum(t, jnp.finfo(F32).min) for t in thr]
    for bi in batches:
        m_ref[bi] = jnp.full((1, hq), NEG_BIG, F32)
        l_ref[bi] = jnp.zeros((1, hq), F32)
        acc_ref[bi] = jnp.zeros((A_KV_RANK, hq), F32)

    def attend(g, with_bias):
        half = KG // 2
        chains = [(bi, h) for bi in batches for h in range(A_HEADS)]
        sel_h, c_h = {}, {}
        for bi in batches:
            for u in range(2):
                rows = pl.ds(pl.multiple_of(g * KG + u * half, half), half)
                sel_h[bi, u] = jnp.where(sc_ref[bi, rows, :] >= thr_sel[bi], 0.0, NEG_BIG)
                c_h[bi, u] = c_ref[bi, rows, :]
        s_all = {}
        for bi, h in chains:
            for u in range(2):
                s_all[bi, h, u] = jnp.dot(c_h[bi, u], qlt_ref[bi, h], preferred_element_type=F32)
        p_all = {}
        alpha_all = {}
        for bi, h in chains:
            cols = slice(h * QW, (h + 1) * QW)
            m_loc, l_loc, p_loc = [], [], []
            for u in range(2):
                s = s_all[bi, h, u]
                if with_bias:
                    s = s + bias_ref[j - g, u * half:(u + 1) * half, cols]
                s = s + sel_h[bi, u]
                m_u = jnp.max(s, axis=0, keepdims=True)
                p = jnp.exp2(s - m_u)
                m_loc.append(m_u)
                l_loc.append(jnp.sum(p, axis=0, keepdims=True))
                p_loc.append(p.astype(BF16))
            m_old = m_ref[bi, :, cols]
            m_new = jnp.maximum(m_old, jnp.maximum(m_loc[0], m_loc[1]))
            alpha = jnp.exp2(m_old - m_new)
            f = [jnp.exp2(m_u - m_new) for m_u in m_loc]
            p_all[bi, h] = jnp.concatenate([p_loc[u] * f[u].astype(BF16) for u in range(2)], axis=0)
            alpha_all[bi, h] = alpha
            l_ref[bi, :, cols] = alpha * l_ref[bi, :, cols] + l_loc[0] * f[0] + l_loc[1] * f[1]
            m_ref[bi, :, cols] = m_new
        for bi, h in chains:
            cols = slice(h * QW, (h + 1) * QW)
            pv = jnp.dot(ct_ref[bi, g], p_all[bi, h], preferred_element_type=F32)
            acc_ref[bi, :, cols] = acc_ref[bi, :, cols] * alpha_all[bi, h] + pv

    def attend_far(g, carry):
        attend(g, False)
        return carry

    def attend_near(g, carry):
        attend(g, True)
        return carry

    near0 = jnp.maximum(j - 1, 0)
    lax.fori_loop(0, near0, attend_far, 0)
    lax.fori_loop(near0, ngrp, attend_near, 0)

    for bi in batches:
        ot = acc_ref[bi] * (1.0 / l_ref[bi])
        for h in range(A_HEADS):
            o_h = ot[:, h * QW:(h + 1) * QW].T
            oa = jnp.dot(o_h.astype(BF16), wuv_ref[h], preferred_element_type=F32)
            o_ref[bi, :, h * A_HEAD_DIM:(h + 1) * A_HEAD_DIM] = oa.astype(BF16)


def _dsa(iqt, iwt, ik, c, ctb, qlt, bias, w_uv, bsz, seq):
    assert seq % QW == 0 and QW % Q_BLOCK == 0
    nb = DSA_NB if bsz % DSA_NB == 0 else 1
    k_sel = min(TOPK_MAX, seq // 4)
    hq = A_HEADS * QW
    return pl.pallas_call(
        functools.partial(_dsa_kernel, k_sel=k_sel, nb=nb),
        grid=(bsz // nb, seq // QW),
        in_specs=[
            pl.BlockSpec((nb, IDX_HEADS * IDX_DIM, QW), lambda b, j: (b, 0, j)),
            pl.BlockSpec((nb, IDX_HEADS, QW), lambda b, j: (b, 0, j)),
            pl.BlockSpec((nb, seq, IDX_DIM), lambda b, j: (b, 0, 0)),
            pl.BlockSpec((nb, seq, A_KV_RANK), lambda b, j: (b, 0, 0)),
            pl.BlockSpec((nb, seq // KG, A_KV_RANK, KG), lambda b, j: (b, 0, 0, 0)),
            pl.BlockSpec((nb, A_HEADS, A_KV_RANK, QW), lambda b, j: (b, 0, 0, j)),
            pl.BlockSpec((2, KG, hq), lambda b, j: (0, 0, 0)),
            pl.BlockSpec((A_HEADS, A_KV_RANK, A_HEAD_DIM), lambda b, j: (0, 0, 0)),
        ],
        out_specs=pl.BlockSpec((nb, QW, A_WIDTH), lambda b, j: (b, j, 0)),
        out_shape=jax.ShapeDtypeStruct((bsz, seq, A_WIDTH), BF16),
        scratch_shapes=[
            pltpu.VMEM((nb, seq, QW), F32),
            pltpu.VMEM((nb, seq, QW), I16),
            pltpu.VMEM((nb, seq, QW), I16),
            pltpu.VMEM((nb, A_KV_RANK, hq), F32),
            pltpu.VMEM((nb, 1, hq), F32),
            pltpu.VMEM((nb, 1, hq), F32),
            pltpu.VMEM((nb, 1, QW), I32),
            pltpu.VMEM((nb, 1, QW), I32),
            pltpu.VMEM((nb, 1, QW), I32),
        ],
        compiler_params=pltpu.CompilerParams(dimension_semantics=("parallel", "arbitrary"),
                                             vmem_limit_bytes=VMEM_LIMIT),
        name="dsa",
    )(iqt, iwt, ik.reshape(bsz, seq, IDX_DIM), c.reshape(bsz, seq, A_KV_RANK), ctb, qlt, bias,
      w_uv.astype(BF16))


def _hgrn_kernel(qs_ref, logf_ref, v_ref, o_ref, st_ref, *, nb):
    @pl.when(pl.program_id(1) == 0)
    def _():
        st_ref[...] = jnp.zeros(st_ref.shape, F32)

    steps = qs_ref.shape[1]
    nchunk = steps // CHUNK
    dk, dv = B_KEY_DIM, B_VAL_DIM
    r64 = lax.broadcasted_iota(I32, (CHUNK, CHUNK), 0)
    c64 = lax.broadcasted_iota(I32, (CHUNK, CHUNK), 1)
    tri = jnp.where(r64 >= c64, 1.0, 0.0).astype(BF16)
    rt = lax.broadcasted_iota(I32, (CHUNK, LANES), 0)
    cs = lax.broadcasted_iota(I32, (CHUNK, LANES), 1)
    causal = cs <= rt
    nsub = CHUNK // SUB
    zpad_f = jnp.zeros((LANES - CHUNK, LANES), F32)
    nt = (((1,), (1,)), ((), ()))
    units = [(bi, ci, h) for bi in range(nb) for ci in range(nchunk) for h in range(B_HEADS)]

    lf_all, b_all = {}, {}
    for bi in range(nb):
        for ci in range(nchunk):
            lf = logf_ref[bi, ci * CHUNK:(ci + 1) * CHUNK, :]
            hi = lf.astype(BF16)
            lo = (lf - hi.astype(F32)).astype(BF16)
            lf_all[bi, ci] = lf
            b_all[bi, ci] = (jnp.dot(tri, hi, preferred_element_type=F32)
                             + jnp.dot(tri, lo, preferred_element_type=F32))

    q_in, q_st, kst, k_pad, v_t, e_last = {}, {}, {}, {}, {}, {}
    r_all, lhs, upd = {}, {}, {}

    def operands(u):
        bi, ci, h = u
        rows = slice(ci * CHUNK, (ci + 1) * CHUNK)
        cols = slice(h * dk, (h + 1) * dk)
        b = b_all[bi, ci][:, cols]
        kk = 1.0 - jnp.exp(lf_all[bi, ci][:, cols])
        q = qs_ref[bi, rows, cols].astype(F32)
        b_last = b[CHUNK - 1:CHUNK, :]
        starts = [jnp.zeros((1, dk), F32)] + [b[i * SUB - 1:i * SUB, :] for i in range(1, nsub)]
        ref_rows = jnp.concatenate([jnp.broadcast_to(s_, (SUB, dk)) for s_ in starts], axis=0)
        qi = q * jnp.exp(b - ref_rows)
        q_in[bi, ci, h] = qi.astype(BF16)
        q_st[bi, ci, h] = (qi * jnp.exp(ref_rows)).astype(BF16)
        secs = []
        for i in range(nsub):
            n = SUB * (i + 1)
            e = jnp.exp(jnp.minimum(starts[i] - b[:n, :], EXP_CAP))
            ks = (kk[:n, :] * e).astype(BF16)
            secs.append(jnp.concatenate([ks, jnp.zeros((LANES - n, dk), BF16)], axis=0))
        kst[bi, ci, h] = jnp.concatenate(secs, axis=0)
        k_hat = (kk * jnp.exp(b_last - b)).astype(BF16)
        k_pad[bi, ci, h] = jnp.concatenate([k_hat, jnp.zeros((LANES - CHUNK, dk), BF16)], axis=0)
        v = v_ref[bi, rows, h * dv:(h + 1) * dv]
        v_t[bi, ci, h] = jnp.concatenate([v.astype(F32), zpad_f], axis=0).T.astype(BF16)
        e_last[bi, ci, h] = jnp.exp(b_last)

    def matmuls(u):
        r_all[u] = lax.dot_general(q_in[u], kst[u], nt, preferred_element_type=F32)
        upd[u] = jnp.dot(v_t[u], k_pad[u], preferred_element_type=F32)

    def scores(u):
        r = r_all[u]
        a = jnp.concatenate([r[i * SUB:(i + 1) * SUB, i * LANES:(i + 1) * LANES] for i in range(nsub)],
                            axis=0)
        a = jnp.where(causal, a, 0.0).astype(BF16)
        lhs[u] = jnp.concatenate([a, q_st[u]], axis=1)

    lead = 2
    for u in units[:lead]:
        operands(u)
    for i, u in enumerate(units):
        matmuls(u)
        if i + lead < len(units):
            operands(units[i + lead])
        scores(u)

    chains = [(bi, h) for bi in range(nb) for h in range(B_HEADS)]
    st = {ch: st_ref[ch[0], ch[1]] for ch in chains}
    for ci in range(nchunk):
        for bi, h in chains:
            u = (bi, ci, h)
            w = jnp.concatenate([v_t[u], st[bi, h].astype(BF16)], axis=1)
            o = lax.dot_general(lhs[u], w, nt, preferred_element_type=F32)
            o_ref[bi, ci * CHUNK:(ci + 1) * CHUNK, h * dv:(h + 1) * dv] = o.astype(BF16)
        for bi, h in chains:
            st[bi, h] = st[bi, h] * e_last[bi, ci, h] + upd[bi, ci, h]
    for bi, h in chains:
        st_ref[bi, h] = st[bi, h]


def _hgrn(qs, logf, v, bsz, seq):
    steps = min(HGRN_STEPS, seq)
    nb = HGRN_NB if bsz % HGRN_NB == 0 else 1
    spec = lambda n: pl.BlockSpec((nb, steps, n), lambda b, s: (b, s, 0))
    return pl.pallas_call(
        functools.partial(_hgrn_kernel, nb=nb),
        grid=(bsz // nb, seq // steps),
        in_specs=[spec(B_FORGET), spec(B_FORGET), spec(B_WIDTH)],
        out_specs=spec(B_WIDTH),
        out_shape=jax.ShapeDtypeStruct((bsz, seq, B_WIDTH), BF16),
        scratch_shapes=[pltpu.VMEM((nb, B_HEADS, B_VAL_DIM, B_KEY_DIM), F32)],
        compiler_params=pltpu.CompilerParams(dimension_semantics=("parallel", "arbitrary"),
                                             vmem_limit_bytes=VMEM_LIMIT),
        name="hgrn",
    )(qs.reshape(bsz, seq, B_FORGET), logf.reshape(bsz, seq, B_FORGET), v.reshape(bsz, seq, B_WIDTH))


def _out_kernel(oa_ref, ga_ref, ob_ref, gb_ref, x_ref, wo_ref, hg_ref, lng_ref, lnb_ref, out_ref):
    oa = oa_ref[...].astype(F32) * ga_ref[...].astype(F32)
    ob = ob_ref[...].astype(F32)
    parts = []
    for h in range(B_HEADS):
        oh = ob[:, h * B_VAL_DIM:(h + 1) * B_VAL_DIM]
        ms = jnp.mean(oh * oh, axis=-1, keepdims=True)
        parts.append(oh * lax.rsqrt(ms + RMS_EPS) * hg_ref[:, h * B_VAL_DIM:(h + 1) * B_VAL_DIM])
    obn = jnp.concatenate(parts, axis=1) * gb_ref[...].astype(F32)
    z = jnp.concatenate([oa, obn], axis=1).astype(BF16)
    y = jnp.dot(z, wo_ref[...], preferred_element_type=F32)
    r = DEEPNORM_ALPHA * x_ref[...] + y
    mu = jnp.mean(r, axis=-1, keepdims=True)
    rc = r - mu
    var = jnp.mean(rc * rc, axis=-1, keepdims=True)
    out_ref[...] = rc * lax.rsqrt(var + LN_EPS) * lng_ref[...] + lnb_ref[...]


def _output(oa, ga, ob, gb, x2, w_o, hg, ln_g, ln_b):
    t = x2.shape[0]
    tq = min(PROJ_TILE, t)
    row = lambda n: pl.BlockSpec((tq, n), lambda i: (i, 0))
    full = lambda shape: pl.BlockSpec(shape, lambda i: (0,) * len(shape))
    d_mix = A_WIDTH + B_WIDTH
    return pl.pallas_call(
        _out_kernel,
        grid=(t // tq,),
        in_specs=[row(A_WIDTH), row(A_WIDTH), row(B_WIDTH), row(B_WIDTH), row(D_MODEL),
                  full((d_mix, D_MODEL)), full((1, B_WIDTH)), full((1, D_MODEL)), full((1, D_MODEL))],
        out_specs=row(D_MODEL),
        out_shape=jax.ShapeDtypeStruct((t, D_MODEL), F32),
        compiler_params=pltpu.CompilerParams(dimension_semantics=("parallel",), vmem_limit_bytes=VMEM_LIMIT),
        name="out",
    )(oa, ga, ob, gb, x2, w_o.astype(BF16), hg.reshape(1, B_WIDTH).astype(F32),
      ln_g.reshape(1, D_MODEL).astype(F32), ln_b.reshape(1, D_MODEL).astype(F32))


def kernel(x, w_in, w_uk, w_uv, kv_norm_g, rel_bias, lb_logits, hgrn_norm_g, w_o, ln_g, ln_b):
    bsz, seq, d = x.shape
    assert d == D_MODEL and seq % Q_BLOCK == 0 and DEPTH == 1
    t = bsz * seq
    x2 = x.reshape(t, d)
    bias = _bias_tiles(rel_bias)
    ik, c, ga, gb, qs, logf, v, iqt, iwt, qlt, ctb = _project(
        x2, w_in[0], w_uk[0], kv_norm_g[0], lb_logits, bsz, seq)
    oa = _dsa(iqt, iwt, ik, c, ctb, qlt, bias, w_uv[0], bsz, seq)
    ob = _hgrn(qs, logf, v, bsz, seq)
    out = _output(oa.reshape(t, A_WIDTH), ga, ob.reshape(t, B_WIDTH), gb, x2, w_o[0],
                  hgrn_norm_g[0], ln_g[0], ln_b[0])
    return out.reshape(bsz, seq, d)
```

```python
import functools
import math

import jax
import jax.numpy as jnp
from jax import lax
from jax.experimental import pallas as pl
from jax.experimental.pallas import tpu as pltpu

F32 = jnp.float32
BF16 = jnp.bfloat16
I32 = jnp.int32
I16 = jnp.int16

D_MODEL = 1024
DEPTH = 1
CHUNK = 64
Q_BLOCK = 128
A_HEADS = 4
A_HEAD_DIM = 128
A_WIDTH = A_HEADS * A_HEAD_DIM
A_KV_RANK = 128
IDX_HEADS = 8
IDX_DIM = 64
TOPK_MAX = 256
B_HEADS = 4
B_KEY_DIM = 128
B_VAL_DIM = 128
B_WIDTH = B_HEADS * B_VAL_DIM
B_FORGET = B_HEADS * B_KEY_DIM
REL_BUCKETS = 32
REL_MAX_DIST = 256
DEEPNORM_ALPHA = (2.0 * DEPTH) ** 0.25
LN_EPS = 1e-5
RMS_EPS = 1e-6

_SPLITS = (
    ('a_q', A_WIDTH), ('a_ckv', A_KV_RANK), ('a_iq', IDX_HEADS * IDX_DIM), ('a_ik', IDX_DIM),
    ('a_iw', IDX_HEADS), ('a_gate', A_WIDTH), ('b_q', B_FORGET), ('b_f', B_FORGET),
    ('b_i', B_WIDTH), ('b_gate', B_WIDTH),
)
_OFFSETS = {}
_off = 0
for _name, _n in _SPLITS:
    _OFFSETS[_name] = (_off, _n)
    _off += _n

LANES = 128
QW = 256
KG = 256
DSA_NB = 4
I16_ROWS = 16
SUB = 16
HGRN_STEPS = 256
HGRN_NB = 2
EXP_CAP = 80.0
NEG_BIG = -1e30
LOG2_E = math.log2(math.e)
INT_MIN = -2 ** 31
PROJ_TILE = 512
OUT_TILE = 1024
VMEM_LIMIT = 56 * 1024 * 1024


def _col(w, name):
    o, n = _OFFSETS[name]
    return w[:, o:o + n]


def _t5_bucket(rel):
    nb = REL_BUCKETS // 2
    max_exact = nb // 2
    ret = jnp.where(rel > 0, nb, 0).astype(I32)
    n = jnp.abs(rel)
    nf = jnp.maximum(n, max_exact).astype(F32)
    large = max_exact + jnp.floor(jnp.log(nf / max_exact) / math.log(REL_MAX_DIST / max_exact)
                                  * (nb - max_exact)).astype(I32)
    large = jnp.minimum(large, nb - 1)
    return ret + jnp.where(n < max_exact, n, large)


def _bias_kernel(rb_ref, out_ref):
    sl = lax.broadcasted_iota(I32, (KG, QW), 0)
    ql = lax.broadcasted_iota(I32, (KG, QW), 1)
    far = REL_BUCKETS // 2 - 1
    for d in range(2):
        bucket = _t5_bucket(sl - ql - KG * d)
        for h in range(A_HEADS):
            acc = jnp.zeros((KG, QW), F32)
            for bk in range(REL_BUCKETS):
                acc = jnp.where(bucket == bk, (rb_ref[bk, h] - rb_ref[far, h]) * LOG2_E, acc)
            out_ref[d, :, h * QW:(h + 1) * QW] = acc


def _bias_tiles(rel_bias):
    assert KG == QW and KG >= REL_MAX_DIST
    return pl.pallas_call(
        _bias_kernel,
        out_shape=jax.ShapeDtypeStruct((2, KG, A_HEADS * QW), F32),
        in_specs=[pl.BlockSpec(memory_space=pltpu.SMEM)],
        out_specs=pl.BlockSpec(memory_space=pltpu.VMEM),
        name="bias",
    )(rel_bias.astype(F32))


def _proj_kernel(x_ref, wrow_ref, wt_ref, wukt_ref, g_ref, lbl_ref,
                 ik_ref, c_ref, ga_ref, gb_ref, qs_ref, logf_ref, v_ref,
                 iqt_ref, iwt_ref, qlt_ref, ct_ref):
    tq = x_ref.shape[0]
    xb = x_ref[...].astype(BF16)

    def rowdot(lo, n):
        return jnp.dot(xb, wrow_ref[:, lo:lo + n], preferred_element_type=F32)

    h0 = rowdot(0, 256)
    a = h0[:, :A_KV_RANK]
    ms = jnp.mean(a * a, axis=-1, keepdims=True)
    c = a * lax.rsqrt(ms + RMS_EPS) * g_ref[...]
    c_ref[...] = c.astype(BF16)
    ct = c.T
    for i in range(tq // KG):
        ct_ref[0, i] = ct[:, i * KG:(i + 1) * KG].astype(BF16)
    ik_ref[...] = h0[:, A_KV_RANK:A_KV_RANK + IDX_DIM].astype(BF16)

    ga_ref[...] = jax.nn.silu(rowdot(256, A_WIDTH)).astype(BF16)
    qs_ref[...] = jax.nn.silu(rowdot(768, B_FORGET)).astype(BF16)
    lbl = lbl_ref[...]
    e = jnp.exp(lbl - jnp.max(lbl, axis=0, keepdims=True))
    lb = e[0:1, :] / jnp.sum(e, axis=0, keepdims=True)
    f = lb + (1.0 - lb) * jax.nn.sigmoid(rowdot(1280, B_FORGET))
    logf_ref[...] = jnp.log(f)
    v_ref[...] = rowdot(1792, B_WIDTH).astype(BF16)
    gb_ref[...] = jax.nn.silu(rowdot(2304, B_WIDTH)).astype(BF16)

    ht = lax.dot_general(wt_ref[...], xb, (((1,), (1,)), ((), ())), preferred_element_type=F32)
    iqt_ref[0] = ht[A_WIDTH:A_WIDTH + IDX_HEADS * IDX_DIM].astype(BF16)
    o = A_WIDTH + IDX_HEADS * IDX_DIM
    iwt_ref[0] = ht[o:o + IDX_HEADS] * (IDX_HEADS ** -0.5 * IDX_DIM ** -0.5)
    for h in range(A_HEADS):
        aq = ht[h * A_HEAD_DIM:(h + 1) * A_HEAD_DIM].astype(BF16)
        ql = jnp.dot(wukt_ref[h], aq, preferred_element_type=F32) * (A_HEAD_DIM ** -0.5 * LOG2_E)
        qlt_ref[0, h] = ql.astype(BF16)


def _project(x2, w_in, w_uk, kv_g, lb_logits, bsz, seq):
    t = bsz * seq
    tq = min(PROJ_TILE, seq)
    per_b = seq // tq
    pad64 = jnp.zeros((D_MODEL, 64), F32)
    wrow = jnp.concatenate([_col(w_in, 'a_ckv'), _col(w_in, 'a_ik'), pad64, _col(w_in, 'a_gate'),
                            _col(w_in, 'b_q'), _col(w_in, 'b_f'), _col(w_in, 'b_i'),
                            _col(w_in, 'b_gate')], axis=1).astype(BF16)
    pad8 = jnp.zeros((D_MODEL, 8), F32)
    wt = jnp.concatenate([_col(w_in, 'a_q'), _col(w_in, 'a_iq'), _col(w_in, 'a_iw'), pad8],
                         axis=1).T.astype(BF16)
    wukt = jnp.swapaxes(w_uk, 1, 2).astype(BF16)
    nrow, nt = wrow.shape[1], wt.shape[0]
    row = lambda n: pl.BlockSpec((tq, n), lambda i: (i, 0))
    full = lambda shape: pl.BlockSpec(shape, lambda i: (0,) * len(shape))
    out_shape = (
        jax.ShapeDtypeStruct((t, IDX_DIM), BF16),
        jax.ShapeDtypeStruct((t, A_KV_RANK), BF16),
        jax.ShapeDtypeStruct((t, A_WIDTH), BF16),
        jax.ShapeDtypeStruct((t, B_WIDTH), BF16),
        jax.ShapeDtypeStruct((t, B_FORGET), BF16),
        jax.ShapeDtypeStruct((t, B_FORGET), F32),
        jax.ShapeDtypeStruct((t, B_WIDTH), BF16),
        jax.ShapeDtypeStruct((bsz, IDX_HEADS * IDX_DIM, seq), BF16),
        jax.ShapeDtypeStruct((bsz, IDX_HEADS, seq), F32),
        jax.ShapeDtypeStruct((bsz, A_HEADS, A_KV_RANK, seq), BF16),
        jax.ShapeDtypeStruct((bsz, seq // KG, A_KV_RANK, KG), BF16),
    )
    out_specs = (
        row(IDX_DIM), row(A_KV_RANK), row(A_WIDTH), row(B_WIDTH), row(B_FORGET), row(B_FORGET), row(B_WIDTH),
        pl.BlockSpec((1, IDX_HEADS * IDX_DIM, tq), lambda i: (i // per_b, 0, i % per_b)),
        pl.BlockSpec((1, IDX_HEADS, tq), lambda i: (i // per_b, 0, i % per_b)),
        pl.BlockSpec((1, A_HEADS, A_KV_RANK, tq), lambda i: (i // per_b, 0, 0, i % per_b)),
        pl.BlockSpec((1, tq // KG, A_KV_RANK, KG), lambda i: (i // per_b, i % per_b, 0, 0)),
    )
    return pl.pallas_call(
        _proj_kernel,
        grid=(t // tq,),
        in_specs=[row(D_MODEL), full((D_MODEL, nrow)), full((nt, D_MODEL)),
                  full((A_HEADS, A_KV_RANK, A_HEAD_DIM)), full((1, A_KV_RANK)), full((DEPTH + 1, B_FORGET))],
        out_specs=out_specs,
        out_shape=out_shape,
        compiler_params=pltpu.CompilerParams(dimension_semantics=("parallel",), vmem_limit_bytes=VMEM_LIMIT),
        name="proj",
    )(x2, wrow, wt, wukt, kv_g.reshape(1, A_KV_RANK).astype(F32), lb_logits.astype(F32))


def _chunk_of(pos):
    return lax.shift_right_logical(pos, int(math.log2(CHUNK)))


def _tree_sum(parts):
    while len(parts) > 1:
        parts = [a + b for a, b in zip(parts[::2], parts[1::2])] + ([parts[-1]] if len(parts) % 2 else [])
    return parts[0]


def _order_key(bits):
    return jnp.where(bits < 0, jnp.int32(INT_MIN) - bits, bits)


def _dsa_kernel(iqt_ref, iwt_ref, ik_ref, c_ref, ct_ref, qlt_ref, bias_ref, wuv_ref, o_ref,
                sc_ref, hi_ref, lo_ref, acc_ref, m_ref, l_ref, thr_ref, nge_ref, ngt_ref, *, k_sel, nb):
    j = pl.program_id(1)
    ngrp = j + 1
    hq = A_HEADS * QW
    batches = range(nb)
    row_iota = lax.broadcasted_iota(I32, (KG, QW), 0)
    lane_iota = lax.broadcasted_iota(I32, (KG, QW), 1)
    q_chunk = _chunk_of(j * QW + lane_iota)

    def rows_of(g):
        return pl.ds(pl.multiple_of(g * KG, KG), KG)

    def score_group(g, diagonal):
        rows = rows_of(g)
        for bi in batches:
            rhs = jnp.concatenate([iqt_ref[bi, h * IDX_DIM:(h + 1) * IDX_DIM, :] for h in range(IDX_HEADS)],
                                  axis=1)
            d = jnp.dot(ik_ref[bi, rows, :], rhs, preferred_element_type=F32)
            sc = iwt_ref[bi, 0:1, :] * jnp.maximum(d[:, :QW], 0.0)
            for h in range(1, IDX_HEADS):
                sc = sc + iwt_ref[bi, h:h + 1, :] * jnp.maximum(d[:, h * QW:(h + 1) * QW], 0.0)
            if diagonal:
                sc = jnp.where(_chunk_of(g * KG + row_iota) <= q_chunk, sc, -jnp.inf)
            sc_ref[bi, rows, :] = sc
            key = _order_key(lax.bitcast_convert_type(sc, I32))
            hi_ref[bi, rows, :] = (key >> 16).astype(I16)
            lo_ref[bi, rows, :] = (key ^ 0x8000).astype(I16)

    def score_inner(g, carry):
        score_group(g, False)
        return carry

    lax.fori_loop(0, j, score_inner, 0)
    score_group(j, True)

    def count16(ref, cands):
        c16 = [c.astype(I16) for c in cands]

        def one(g, accs):
            rows = rows_of(g)
            out = []
            for bi in batches:
                blk = ref[bi, rows, :]
                ind = jnp.where(blk >= c16[bi], jnp.int16(1), jnp.int16(0))
                out.append(accs[bi] + _tree_sum([ind[r:r + I16_ROWS] for r in range(0, KG, I16_ROWS)]))
            return tuple(out)

        def two(p, accs):
            return one(2 * p + 1, one(2 * p, accs))

        accs = tuple(jnp.zeros((I16_ROWS, QW), I16) for _ in batches)
        accs = lax.fori_loop(0, ngrp // 2, two, accs)
        accs = lax.fori_loop(2 * (ngrp // 2), ngrp, one, accs)
        return [jnp.sum(a.astype(I32), axis=0, keepdims=True) for a in accs]

    def bisect16(ref, want, n_all):
        def step(i, carry):
            ts, n_ge, n_gt = carry
            cands = [t + lax.shift_left(jnp.int32(1), 15 - i) for t in ts]
            cnts = count16(ref, cands)
            ok = [cnts[bi] >= want[bi] for bi in batches]
            return (tuple(jnp.where(ok[bi], cands[bi], ts[bi]) for bi in batches),
                    tuple(jnp.where(ok[bi], cnts[bi], n_ge[bi]) for bi in batches),
                    tuple(jnp.where(ok[bi], n_gt[bi], cnts[bi]) for bi in batches))
        init = (tuple(jnp.full((1, QW), -32768, I32) for _ in batches), tuple(n_all),
                tuple(jnp.zeros((1, QW), I32) for _ in batches))
        return lax.fori_loop(0, 16, step, init)

    want_hi = [jnp.full((1, QW), k_sel, I32) for _ in batches]
    n_rows = [jnp.zeros((1, QW), I32) + ngrp * KG for _ in batches]
    p_hi, n_ge_hi, n_gt_hi = bisect16(hi_ref, want_hi, n_rows)

    def keep_ties(g, carry):
        rows = rows_of(g)
        for bi in batches:
            lo_ref[bi, rows, :] = jnp.where(hi_ref[bi, rows, :] == p_hi[bi].astype(I16), lo_ref[bi, rows, :],
                                            jnp.int16(-32768))
        return carry

    lax.fori_loop(0, ngrp, keep_ties, 0)
    want_lo = [k_sel - n_gt_hi[bi] for bi in batches]
    n_eq_hi = [n_ge_hi[bi] - n_gt_hi[bi] for bi in batches]
    p_lo, n_ge_lo, n_gt_lo = bisect16(lo_ref, want_lo, n_eq_hi)
    for bi in batches:
        thr_ref[bi] = lax.shift_left(p_hi[bi], 16) + (p_lo[bi] + 32768)
        nge_ref[bi] = n_gt_hi[bi] + n_ge_lo[bi]
        ngt_ref[bi] = n_gt_hi[bi] + n_gt_lo[bi]

    def as_float(key):
        return lax.bitcast_convert_type(_order_key(key), F32)

    def count32(cands):
        def body(g, accs):
            rows = rows_of(g)
            out = []
            for bi in batches:
                ind = jnp.where(sc_ref[bi, rows, :] >= cands[bi], 1, 0)
                out.append(accs[bi] + jnp.sum(ind.reshape(KG // 8, 8, QW), axis=0))
            return tuple(out)

        accs = lax.fori_loop(0, ngrp, body, tuple(jnp.zeros((8, QW), I32) for _ in batches))
        return [jnp.sum(a, axis=0, keepdims=True) for a in accs]

    n_ge_f = count32([as_float(thr_ref[bi]) for bi in batches])
    differs = _tree_sum([jnp.where(n_ge_f[bi] != nge_ref[bi], 1, 0) for bi in batches])

    @pl.when(jnp.max(differs) > 0)
    def _():
        def step(i, carry):
            ts, n_ge, n_gt = carry
            cands = [t + lax.shift_left(jnp.int32(1), 31 - i) for t in ts]
            cnts = count32([as_float(c) for c in cands])
            ok = [cnts[bi] >= k_sel for bi in batches]
            return (tuple(jnp.where(ok[bi], cands[bi], ts[bi]) for bi in batches),
                    tuple(jnp.where(ok[bi], cnts[bi], n_ge[bi]) for bi in batches),
                    tuple(jnp.where(ok[bi], n_gt[bi], cnts[bi]) for bi in batches))
        init = (tuple(jnp.full((1, QW), INT_MIN, I32) for _ in batches), tuple(n_rows),
                tuple(jnp.zeros((1, QW), I32) for _ in batches))
        ts, n_ge, n_gt = lax.fori_loop(0, 32, step, init)
        for bi in batches:
            thr_ref[bi] = ts[bi]
            nge_ref[bi] = n_ge[bi]
            ngt_ref[bi] = n_gt[bi]

    thr = [as_float(thr_ref[bi]) for bi in batches]

    excess = _tree_sum([jnp.where((nge_ref[bi] > k_sel) & (thr[bi] > -jnp.inf), 1, 0) for bi in batches])

    @pl.when(jnp.max(excess) > 0)
    def _():
        tri = jnp.where(row_iota >= lane_iota, 1.0, 0.0).astype(BF16)
        need = [(k_sel - ngt_ref[bi]).astype(F32) for bi in batches]

        def demote(g, carry):
            rows = rows_of(g)
            out = []
            for bi in batches:
                blk = sc_ref[bi, rows, :]
                eq = jnp.where(blk == thr[bi], 1.0, 0.0)
                incl = jnp.dot(tri, eq.astype(BF16), preferred_element_type=F32)
                rank = incl - eq + carry[bi]
                sc_ref[bi, rows, :] = jnp.where((blk == thr[bi]) & (rank >= need[bi]), -jnp.inf, blk)
                out.append(carry[bi] + incl[KG - 1:KG, :])
            return tuple(out)

        lax.fori_loop(0, ngrp, demote, tuple(jnp.zeros((1, QW), F32) for _ in batches))

    thr_sel = [jnp.maximum(t, jnp.finfo(F32).min) for t in thr]
    for bi in batches:
        m_ref[bi] = jnp.full((1, hq), NEG_BIG, F32)
        l_ref[bi] = jnp.zeros((1, hq), F32)
        acc_ref[bi] = jnp.zeros((A_KV_RANK, hq), F32)

    def attend(g, with_bias):
        half = KG // 2
        chains = [(bi, h) for bi in batches for h in range(A_HEADS)]
        sel_h, c_h = {}, {}
        for bi in batches:
            for u in range(2):
                rows = pl.ds(pl.multiple_of(g * KG + u * half, half), half)
                sel_h[bi, u] = jnp.where(sc_ref[bi, rows, :] >= thr_sel[bi], 0.0, NEG_BIG)
                c_h[bi, u] = c_ref[bi, rows, :]
        s_all = {}
        for bi, h in chains:
            for u in range(2):
                s_all[bi, h, u] = jnp.dot(c_h[bi, u], qlt_ref[bi, h], preferred_element_type=F32)
        p_all = {}
        alpha_all = {}
        for bi, h in chains:
            cols = slice(h * QW, (h + 1) * QW)
            m_loc, l_loc, p_loc = [], [], []
            for u in range(2):
                s = s_all[bi, h, u]
                if with_bias:
                    s = s + bias_ref[j - g, u * half:(u + 1) * half, cols]
                s = s + sel_h[bi, u]
                m_u = jnp.max(s, axis=0, keepdims=True)
                p = jnp.exp2(s - m_u)
                m_loc.append(m_u)
                l_loc.append(jnp.sum(p, axis=0, keepdims=True))
                p_loc.append(p.astype(BF16))
            m_old = m_ref[bi, :, cols]
            m_new = jnp.maximum(m_old, jnp.maximum(m_loc[0], m_loc[1]))
            alpha = jnp.exp2(m_old - m_new)
            f = [jnp.exp2(m_u - m_new) for m_u in m_loc]
            p_all[bi, h] = jnp.concatenate([p_loc[u] * f[u].astype(BF16) for u in range(2)], axis=0)
            alpha_all[bi, h] = alpha
            l_ref[bi, :, cols] = alpha * l_ref[bi, :, cols] + l_loc[0] * f[0] + l_loc[1] * f[1]
            m_ref[bi, :, cols] = m_new
        for bi, h in chains:
            cols = slice(h * QW, (h + 1) * QW)
            pv = jnp.dot(ct_ref[bi, g], p_all[bi, h], preferred_element_type=F32)
            acc_ref[bi, :, cols] = acc_ref[bi, :, cols] * alpha_all[bi, h] + pv

    def attend_far(g, carry):
        attend(g, False)
        return carry

    def attend_near(g, carry):
        attend(g, True)
        return carry

    near0 = jnp.maximum(j - 1, 0)
    lax.fori_loop(0, near0, attend_far, 0)
    lax.fori_loop(near0, ngrp, attend_near, 0)

    for bi in batches:
        ot = acc_ref[bi] * (1.0 / l_ref[bi])
        for h in range(A_HEADS):
            o_h = ot[:, h * QW:(h + 1) * QW].T
            oa = jnp.dot(o_h.astype(BF16), wuv_ref[h], preferred_element_type=F32)
            o_ref[bi, :, h * A_HEAD_DIM:(h + 1) * A_HEAD_DIM] = oa.astype(BF16)


def _dsa(iqt, iwt, ik, c, ctb, qlt, bias, w_uv, bsz, seq):
    assert seq % QW == 0 and QW % Q_BLOCK == 0
    nb = DSA_NB if bsz % DSA_NB == 0 else 1
    k_sel = min(TOPK_MAX, seq // 4)
    hq = A_HEADS * QW
    return pl.pallas_call(
        functools.partial(_dsa_kernel, k_sel=k_sel, nb=nb),
        grid=(bsz // nb, seq // QW),
        in_specs=[
            pl.BlockSpec((nb, IDX_HEADS * IDX_DIM, QW), lambda b, j: (b, 0, j)),
            pl.BlockSpec((nb, IDX_HEADS, QW), lambda b, j: (b, 0, j)),
            pl.BlockSpec((nb, seq, IDX_DIM), lambda b, j: (b, 0, 0)),
            pl.BlockSpec((nb, seq, A_KV_RANK), lambda b, j: (b, 0, 0)),
            pl.BlockSpec((nb, seq // KG, A_KV_RANK, KG), lambda b, j: (b, 0, 0, 0)),
            pl.BlockSpec((nb, A_HEADS, A_KV_RANK, QW), lambda b, j: (b, 0, 0, j)),
            pl.BlockSpec((2, KG, hq), lambda b, j: (0, 0, 0)),
            pl.BlockSpec((A_HEADS, A_KV_RANK, A_HEAD_DIM), lambda b, j: (0, 0, 0)),
        ],
        out_specs=pl.BlockSpec((nb, QW, A_WIDTH), lambda b, j: (b, j, 0)),
        out_shape=jax.ShapeDtypeStruct((bsz, seq, A_WIDTH), BF16),
        scratch_shapes=[
            pltpu.VMEM((nb, seq, QW), F32),
            pltpu.VMEM((nb, seq, QW), I16),
            pltpu.VMEM((nb, seq, QW), I16),
            pltpu.VMEM((nb, A_KV_RANK, hq), F32),
            pltpu.VMEM((nb, 1, hq), F32),
            pltpu.VMEM((nb, 1, hq), F32),
            pltpu.VMEM((nb, 1, QW), I32),
            pltpu.VMEM((nb, 1, QW), I32),
            pltpu.VMEM((nb, 1, QW), I32),
        ],
        compiler_params=pltpu.CompilerParams(dimension_semantics=("parallel", "arbitrary"),
                                             vmem_limit_bytes=VMEM_LIMIT),
        name="dsa",
    )(iqt, iwt, ik.reshape(bsz, seq, IDX_DIM), c.reshape(bsz, seq, A_KV_RANK), ctb, qlt, bias,
      w_uv.astype(BF16))


def _hgrn_kernel(qs_ref, logf_ref, v_ref, o_ref, st_ref, *, nb):
    @pl.when(pl.program_id(1) == 0)
    def _():
        st_ref[...] = jnp.zeros(st_ref.shape, F32)

    steps = qs_ref.shape[1]
    nchunk = steps // CHUNK
    dk, dv = B_KEY_DIM, B_VAL_DIM
    r64 = lax.broadcasted_iota(I32, (CHUNK, CHUNK), 0)
    c64 = lax.broadcasted_iota(I32, (CHUNK, CHUNK), 1)
    tri = jnp.where(r64 >= c64, 1.0, 0.0).astype(BF16)
    rt = lax.broadcasted_iota(I32, (CHUNK, LANES), 0)
    cs = lax.broadcasted_iota(I32, (CHUNK, LANES), 1)
    causal = cs <= rt
    nsub = CHUNK // SUB
    zpad_f = jnp.zeros((LANES - CHUNK, LANES), F32)
    nt = (((1,), (1,)), ((), ()))
    units = [(bi, ci, h) for bi in range(nb) for ci in range(nchunk) for h in range(B_HEADS)]

    lf_all, b_all = {}, {}
    for bi in range(nb):
        for ci in range(nchunk):
            lf = logf_ref[bi, ci * CHUNK:(ci + 1) * CHUNK, :]
            hi = lf.astype(BF16)
            lo = (lf - hi.astype(F32)).astype(BF16)
            lf_all[bi, ci] = lf
            b_all[bi, ci] = (jnp.dot(tri, hi, preferred_element_type=F32)
                             + jnp.dot(tri, lo, preferred_element_type=F32))

    q_in, q_st, kst, k_pad, v_t, e_last = {}, {}, {}, {}, {}, {}
    r_all, lhs, upd = {}, {}, {}

    def operands(u):
        bi, ci, h = u
        rows = slice(ci * CHUNK, (ci + 1) * CHUNK)
        cols = slice(h * dk, (h + 1) * dk)
        b = b_all[bi, ci][:, cols]
        kk = 1.0 - jnp.exp(lf_all[bi, ci][:, cols])
        q = qs_ref[bi, rows, cols].astype(F32)
        b_last = b[CHUNK - 1:CHUNK, :]
        starts = [jnp.zeros((1, dk), F32)] + [b[i * SUB - 1:i * SUB, :] for i in range(1, nsub)]
        ref_rows = jnp.concatenate([jnp.broadcast_to(s_, (SUB, dk)) for s_ in starts], axis=0)
        qi = q * jnp.exp(b - ref_rows)
        q_in[bi, ci, h] = qi.astype(BF16)
        q_st[bi, ci, h] = (qi * jnp.exp(ref_rows)).astype(BF16)
        secs = []
        for i in range(nsub):
            n = SUB * (i + 1)
            e = jnp.exp(jnp.minimum(starts[i] - b[:n, :], EXP_CAP))
            ks = (kk[:n, :] * e).astype(BF16)
            secs.append(ks if n == CHUNK else jnp.concatenate([ks, jnp.zeros((CHUNK - n, dk), BF16)], axis=0))
        kst[bi, ci, h] = jnp.concatenate(secs, axis=0)
        k_hat = (kk * jnp.exp(b_last - b)).astype(BF16)
        k_pad[bi, ci, h] = jnp.concatenate([k_hat, jnp.zeros((LANES - CHUNK, dk), BF16)], axis=0)
        v = v_ref[bi, rows, h * dv:(h + 1) * dv]
        v_t[bi, ci, h] = jnp.concatenate([v.astype(F32), zpad_f], axis=0).T.astype(BF16)
        e_last[bi, ci, h] = jnp.exp(b_last)

    def matmuls(u):
        r_all[u] = lax.dot_general(q_in[u], kst[u], nt, preferred_element_type=F32)
        upd[u] = jnp.dot(v_t[u], k_pad[u], preferred_element_type=F32)

    def scores(u):
        r = r_all[u]
        slabs = []
        for i in range(nsub):
            lane0 = (i * CHUNK // LANES) * LANES
            slab = r[i * SUB:(i + 1) * SUB, lane0:lane0 + LANES]
            if (i * CHUNK) % LANES:
                slab = pltpu.roll(slab, LANES - (i * CHUNK) % LANES, 1)
            slabs.append(slab)
        a = jnp.concatenate(slabs, axis=0)
        a = jnp.where(causal, a, 0.0).astype(BF16)
        lhs[u] = jnp.concatenate([a, q_st[u]], axis=1)

    lead = 2
    for u in units[:lead]:
        operands(u)
    for i, u in enumerate(units):
        matmuls(u)
        if i + lead < len(units):
            operands(units[i + lead])
        scores(u)

    chains = [(bi, h) for bi in range(nb) for h in range(B_HEADS)]
    st = {ch: st_ref[ch[0], ch[1]] for ch in chains}
    for ci in range(nchunk):
        for bi, h in chains:
            u = (bi, ci, h)
            w = jnp.concatenate([v_t[u], st[bi, h].astype(BF16)], axis=1)
            o = lax.dot_general(lhs[u], w, nt, preferred_element_type=F32)
            o_ref[bi, ci * CHUNK:(ci + 1) * CHUNK, h * dv:(h + 1) * dv] = o.astype(BF16)
        for bi, h in chains:
            st[bi, h] = st[bi, h] * e_last[bi, ci, h] + upd[bi, ci, h]
    for bi, h in chains:
        st_ref[bi, h] = st[bi, h]


def _hgrn(qs, logf, v, bsz, seq):
    steps = min(HGRN_STEPS, seq)
    nb = HGRN_NB if bsz % HGRN_NB == 0 else 1
    spec = lambda n: pl.BlockSpec((nb, steps, n), lambda b, s: (b, s, 0))
    return pl.pallas_call(
        functools.partial(_hgrn_kernel, nb=nb),
        grid=(bsz // nb, seq // steps),
        in_specs=[spec(B_FORGET), spec(B_FORGET), spec(B_WIDTH)],
        out_specs=spec(B_WIDTH),
        out_shape=jax.ShapeDtypeStruct((bsz, seq, B_WIDTH), BF16),
        scratch_shapes=[pltpu.VMEM((nb, B_HEADS, B_VAL_DIM, B_KEY_DIM), F32)],
        compiler_params=pltpu.CompilerParams(dimension_semantics=("parallel", "arbitrary"),
                                             vmem_limit_bytes=VMEM_LIMIT),
        name="hgrn",
    )(qs.reshape(bsz, seq, B_FORGET), logf.reshape(bsz, seq, B_FORGET), v.reshape(bsz, seq, B_WIDTH))


def _out_kernel(oa_ref, ga_ref, ob_ref, gb_ref, x_ref, wo_ref, hg_ref, lng_ref, lnb_ref, out_ref):
    oa = oa_ref[...].astype(F32) * ga_ref[...].astype(F32)
    ob = ob_ref[...].astype(F32)
    parts = []
    for h in range(B_HEADS):
        oh = ob[:, h * B_VAL_DIM:(h + 1) * B_VAL_DIM]
        ms = jnp.mean(oh * oh, axis=-1, keepdims=True)
        parts.append(oh * lax.rsqrt(ms + RMS_EPS) * hg_ref[:, h * B_VAL_DIM:(h + 1) * B_VAL_DIM])
    obn = jnp.concatenate(parts, axis=1) * gb_ref[...].astype(F32)
    z = jnp.concatenate([oa, obn], axis=1).astype(BF16)
    y = jnp.dot(z, wo_ref[...], preferred_element_type=F32)
    r = DEEPNORM_ALPHA * x_ref[...] + y
    mu = jnp.mean(r, axis=-1, keepdims=True)
    rc = r - mu
    var = jnp.mean(rc * rc, axis=-1, keepdims=True)
    out_ref[...] = rc * lax.rsqrt(var + LN_EPS) * lng_ref[...] + lnb_ref[...]


def _output(oa, ga, ob, gb, x2, w_o, hg, ln_g, ln_b):
    t = x2.shape[0]
    tq = min(OUT_TILE, t)
    row = lambda n: pl.BlockSpec((tq, n), lambda i: (i, 0))
    full = lambda shape: pl.BlockSpec(shape, lambda i: (0,) * len(shape))
    d_mix = A_WIDTH + B_WIDTH
    return pl.pallas_call(
        _out_kernel,
        grid=(t // tq,),
        in_specs=[row(A_WIDTH), row(A_WIDTH), row(B_WIDTH), row(B_WIDTH), row(D_MODEL),
                  full((d_mix, D_MODEL)), full((1, B_WIDTH)), full((1, D_MODEL)), full((1, D_MODEL))],
        out_specs=row(D_MODEL),
        out_shape=jax.ShapeDtypeStruct((t, D_MODEL), F32),
        compiler_params=pltpu.CompilerParams(dimension_semantics=("parallel",), vmem_limit_bytes=VMEM_LIMIT),
        name="out",
    )(oa, ga, ob, gb, x2, w_o.astype(BF16), hg.reshape(1, B_WIDTH).astype(F32),
      ln_g.reshape(1, D_MODEL).astype(F32), ln_b.reshape(1, D_MODEL).astype(F32))


def kernel(x, w_in, w_uk, w_uv, kv_norm_g, rel_bias, lb_logits, hgrn_norm_g, w_o, ln_g, ln_b):
    bsz, seq, d = x.shape
    assert d == D_MODEL and seq % Q_BLOCK == 0 and DEPTH == 1
    t = bsz * seq
    x2 = x.reshape(t, d)
    bias = _bias_tiles(rel_bias)
    ik, c, ga, gb, qs, logf, v, iqt, iwt, qlt, ctb = _project(
        x2, w_in[0], w_uk[0], kv_norm_g[0], lb_logits, bsz, seq)
    oa = _dsa(iqt, iwt, ik, c, ctb, qlt, bias, w_uv[0], bsz, seq)
    ob = _hgrn(qs, logf, v, bsz, seq)
    out = _output(oa.reshape(t, A_WIDTH), ga, ob.reshape(t, B_WIDTH), gb, x2, w_o[0],
                  hgrn_norm_g[0], ln_g[0], ln_b[0])
    return out.reshape(bsz, seq, d)
```

```python
import functools
import math

import jax
import jax.numpy as jnp
from jax import lax
from jax.experimental import pallas as pl
from jax.experimental.pallas import tpu as pltpu

F32 = jnp.float32
BF16 = jnp.bfloat16
I32 = jnp.int32
I16 = jnp.int16

D_MODEL = 1024
DEPTH = 1
CHUNK = 64
Q_BLOCK = 128
A_HEADS = 4
A_HEAD_DIM = 128
A_WIDTH = A_HEADS * A_HEAD_DIM
A_KV_RANK = 128
IDX_HEADS = 8
IDX_DIM = 64
TOPK_MAX = 256
B_HEADS = 4
B_KEY_DIM = 128
B_VAL_DIM = 128
B_WIDTH = B_HEADS * B_VAL_DIM
B_FORGET = B_HEADS * B_KEY_DIM
REL_BUCKETS = 32
REL_MAX_DIST = 256
DEEPNORM_ALPHA = (2.0 * DEPTH) ** 0.25
LN_EPS = 1e-5
RMS_EPS = 1e-6

_SPLITS = (
    ('a_q', A_WIDTH), ('a_ckv', A_KV_RANK), ('a_iq', IDX_HEADS * IDX_DIM), ('a_ik', IDX_DIM),
    ('a_iw', IDX_HEADS), ('a_gate', A_WIDTH), ('b_q', B_FORGET), ('b_f', B_FORGET),
    ('b_i', B_WIDTH), ('b_gate', B_WIDTH),
)
_OFFSETS = {}
_off = 0
for _name, _n in _SPLITS:
    _OFFSETS[_name] = (_off, _n)
    _off += _n

LANES = 128
QW = 256
KG = 256
DSA_NB = 4
I16_ROWS = 16
CT_ROWS = A_KV_RANK + I16_ROWS
SUB = 16
HGRN_STEPS = 256
HGRN_NB = 2
EXP_CAP = 80.0
NEG_BIG = -1e30
LOG2_E = math.log2(math.e)
INT_MIN = -2 ** 31
PROJ_TILE = 512
OUT_TILE = 1024
VMEM_LIMIT = 56 * 1024 * 1024


def _col(w, name):
    o, n = _OFFSETS[name]
    return w[:, o:o + n]


def _t5_bucket(rel):
    nb = REL_BUCKETS // 2
    max_exact = nb // 2
    ret = jnp.where(rel > 0, nb, 0).astype(I32)
    n = jnp.abs(rel)
    nf = jnp.maximum(n, max_exact).astype(F32)
    large = max_exact + jnp.floor(jnp.log(nf / max_exact) / math.log(REL_MAX_DIST / max_exact)
                                  * (nb - max_exact)).astype(I32)
    large = jnp.minimum(large, nb - 1)
    return ret + jnp.where(n < max_exact, n, large)


def _bias_kernel(rb_ref, out_ref):
    sl = lax.broadcasted_iota(I32, (KG, QW), 0)
    ql = lax.broadcasted_iota(I32, (KG, QW), 1)
    far = REL_BUCKETS // 2 - 1
    for d in range(2):
        bucket = _t5_bucket(sl - ql - KG * d)
        for h in range(A_HEADS):
            acc = jnp.zeros((KG, QW), F32)
            for bk in range(REL_BUCKETS):
                acc = jnp.where(bucket == bk, (rb_ref[bk, h] - rb_ref[far, h]) * LOG2_E, acc)
            out_ref[d, :, h * QW:(h + 1) * QW] = acc


def _bias_tiles(rel_bias):
    assert KG == QW and KG >= REL_MAX_DIST
    return pl.pallas_call(
        _bias_kernel,
        out_shape=jax.ShapeDtypeStruct((2, KG, A_HEADS * QW), F32),
        in_specs=[pl.BlockSpec(memory_space=pltpu.SMEM)],
        out_specs=pl.BlockSpec(memory_space=pltpu.VMEM),
        name="bias",
    )(rel_bias.astype(F32))


def _proj_kernel(x_ref, wrow_ref, wt_ref, wukt_ref, g_ref, lbl_ref,
                 ik_ref, c_ref, ga_ref, gb_ref, qs_ref, logf_ref, v_ref,
                 iqt_ref, iwt_ref, qlt_ref, ct_ref):
    tq = x_ref.shape[0]
    xb = x_ref[...].astype(BF16)

    def rowdot(lo, n):
        return jnp.dot(xb, wrow_ref[:, lo:lo + n], preferred_element_type=F32)

    h0 = rowdot(0, 256)
    a = h0[:, :A_KV_RANK]
    ms = jnp.mean(a * a, axis=-1, keepdims=True)
    c = a * lax.rsqrt(ms + RMS_EPS) * g_ref[...]
    c_ref[...] = c.astype(BF16)
    ct = c.T
    ones = jnp.ones((CT_ROWS - A_KV_RANK, KG), BF16)
    for i in range(tq // KG):
        ct_ref[0, i] = jnp.concatenate([ct[:, i * KG:(i + 1) * KG].astype(BF16), ones], axis=0)
    ik_ref[...] = h0[:, A_KV_RANK:A_KV_RANK + IDX_DIM].astype(BF16)

    ga_ref[...] = jax.nn.silu(rowdot(256, A_WIDTH)).astype(BF16)
    qs_ref[...] = jax.nn.silu(rowdot(768, B_FORGET)).astype(BF16)
    lbl = lbl_ref[...]
    e = jnp.exp(lbl - jnp.max(lbl, axis=0, keepdims=True))
    lb = e[0:1, :] / jnp.sum(e, axis=0, keepdims=True)
    f = lb + (1.0 - lb) * jax.nn.sigmoid(rowdot(1280, B_FORGET))
    logf_ref[...] = jnp.log(f)
    v_ref[...] = rowdot(1792, B_WIDTH).astype(BF16)
    gb_ref[...] = jax.nn.silu(rowdot(2304, B_WIDTH)).astype(BF16)

    ht = lax.dot_general(wt_ref[...], xb, (((1,), (1,)), ((), ())), preferred_element_type=F32)
    iqt_ref[0] = ht[A_WIDTH:A_WIDTH + IDX_HEADS * IDX_DIM].astype(BF16)
    o = A_WIDTH + IDX_HEADS * IDX_DIM
    iwt_ref[0] = ht[o:o + IDX_HEADS] * (IDX_HEADS ** -0.5 * IDX_DIM ** -0.5)
    for h in range(A_HEADS):
        aq = ht[h * A_HEAD_DIM:(h + 1) * A_HEAD_DIM].astype(BF16)
        ql = jnp.dot(wukt_ref[h], aq, preferred_element_type=F32) * (A_HEAD_DIM ** -0.5 * LOG2_E)
        qlt_ref[0, h] = ql.astype(BF16)


def _project(x2, w_in, w_uk, kv_g, lb_logits, bsz, seq):
    t = bsz * seq
    tq = min(PROJ_TILE, seq)
    per_b = seq // tq
    pad64 = jnp.zeros((D_MODEL, 64), F32)
    wrow = jnp.concatenate([_col(w_in, 'a_ckv'), _col(w_in, 'a_ik'), pad64, _col(w_in, 'a_gate'),
                            _col(w_in, 'b_q'), _col(w_in, 'b_f'), _col(w_in, 'b_i'),
                            _col(w_in, 'b_gate')], axis=1).astype(BF16)
    pad8 = jnp.zeros((D_MODEL, 8), F32)
    wt = jnp.concatenate([_col(w_in, 'a_q'), _col(w_in, 'a_iq'), _col(w_in, 'a_iw'), pad8],
                         axis=1).T.astype(BF16)
    wukt = jnp.swapaxes(w_uk, 1, 2).astype(BF16)
    nrow, nt = wrow.shape[1], wt.shape[0]
    row = lambda n: pl.BlockSpec((tq, n), lambda i: (i, 0))
    full = lambda shape: pl.BlockSpec(shape, lambda i: (0,) * len(shape))
    out_shape = (
        jax.ShapeDtypeStruct((t, IDX_DIM), BF16),
        jax.ShapeDtypeStruct((t, A_KV_RANK), BF16),
        jax.ShapeDtypeStruct((t, A_WIDTH), BF16),
        jax.ShapeDtypeStruct((t, B_WIDTH), BF16),
        jax.ShapeDtypeStruct((t, B_FORGET), BF16),
        jax.ShapeDtypeStruct((t, B_FORGET), F32),
        jax.ShapeDtypeStruct((t, B_WIDTH), BF16),
        jax.ShapeDtypeStruct((bsz, IDX_HEADS * IDX_DIM, seq), BF16),
        jax.ShapeDtypeStruct((bsz, IDX_HEADS, seq), F32),
        jax.ShapeDtypeStruct((bsz, A_HEADS, A_KV_RANK, seq), BF16),
        jax.ShapeDtypeStruct((bsz, seq // KG, CT_ROWS, KG), BF16),
    )
    out_specs = (
        row(IDX_DIM), row(A_KV_RANK), row(A_WIDTH), row(B_WIDTH), row(B_FORGET), row(B_FORGET), row(B_WIDTH),
        pl.BlockSpec((1, IDX_HEADS * IDX_DIM, tq), lambda i: (i // per_b, 0, i % per_b)),
        pl.BlockSpec((1, IDX_HEADS, tq), lambda i: (i // per_b, 0, i % per_b)),
        pl.BlockSpec((1, A_HEADS, A_KV_RANK, tq), lambda i: (i // per_b, 0, 0, i % per_b)),
        pl.BlockSpec((1, tq // KG, CT_ROWS, KG), lambda i: (i // per_b, i % per_b, 0, 0)),
    )
    return pl.pallas_call(
        _proj_kernel,
        grid=(t // tq,),
        in_specs=[row(D_MODEL), full((D_MODEL, nrow)), full((nt, D_MODEL)),
                  full((A_HEADS, A_KV_RANK, A_HEAD_DIM)), full((1, A_KV_RANK)), full((DEPTH + 1, B_FORGET))],
        out_specs=out_specs,
        out_shape=out_shape,
        compiler_params=pltpu.CompilerParams(dimension_semantics=("parallel",), vmem_limit_bytes=VMEM_LIMIT),
        name="proj",
    )(x2, wrow, wt, wukt, kv_g.reshape(1, A_KV_RANK).astype(F32), lb_logits.astype(F32))


def _chunk_of(pos):
    return lax.shift_right_logical(pos, int(math.log2(CHUNK)))


def _tree_sum(parts):
    while len(parts) > 1:
        parts = [a + b for a, b in zip(parts[::2], parts[1::2])] + ([parts[-1]] if len(parts) % 2 else [])
    return parts[0]


def _order_key(bits):
    return jnp.where(bits < 0, jnp.int32(INT_MIN) - bits, bits)


def _dsa_kernel(iqt_ref, iwt_ref, ik_ref, c_ref, ct_ref, qlt_ref, bias_ref, wuv_ref, o_ref,
                sc_ref, hi_ref, lo_ref, acc_ref, m_ref, thr_ref, nge_ref, ngt_ref, *, k_sel, nb):
    j = pl.program_id(1)
    ngrp = j + 1
    hq = A_HEADS * QW
    batches = range(nb)
    row_iota = lax.broadcasted_iota(I32, (KG, QW), 0)
    lane_iota = lax.broadcasted_iota(I32, (KG, QW), 1)
    q_chunk = _chunk_of(j * QW + lane_iota)

    def rows_of(g):
        return pl.ds(pl.multiple_of(g * KG, KG), KG)

    def score_group(g, diagonal):
        rows = rows_of(g)
        for bi in batches:
            rhs = jnp.concatenate([iqt_ref[bi, h * IDX_DIM:(h + 1) * IDX_DIM, :] for h in range(IDX_HEADS)],
                                  axis=1)
            d = jnp.dot(ik_ref[bi, rows, :], rhs, preferred_element_type=F32)
            sc = iwt_ref[bi, 0:1, :] * jnp.maximum(d[:, :QW], 0.0)
            for h in range(1, IDX_HEADS):
                sc = sc + iwt_ref[bi, h:h + 1, :] * jnp.maximum(d[:, h * QW:(h + 1) * QW], 0.0)
            if diagonal:
                sc = jnp.where(_chunk_of(g * KG + row_iota) <= q_chunk, sc, -jnp.inf)
            sc_ref[bi, rows, :] = sc
            key = _order_key(lax.bitcast_convert_type(sc, I32))
            hi_ref[bi, rows, :] = (key >> 16).astype(I16)
            lo_ref[bi, rows, :] = (key ^ 0x8000).astype(I16)

    def score_inner(g, carry):
        score_group(g, False)
        return carry

    lax.fori_loop(0, j, score_inner, 0)
    score_group(j, True)

    def count16(ref, cands):
        c16 = [c.astype(I16) for c in cands]

        def one(g, accs):
            rows = rows_of(g)
            out = []
            for bi in batches:
                blk = ref[bi, rows, :]
                ind = jnp.where(blk >= c16[bi], jnp.int16(1), jnp.int16(0))
                out.append(accs[bi] + _tree_sum([ind[r:r + I16_ROWS] for r in range(0, KG, I16_ROWS)]))
            return tuple(out)

        def two(p, accs):
            return one(2 * p + 1, one(2 * p, accs))

        accs = tuple(jnp.zeros((I16_ROWS, QW), I16) for _ in batches)
        accs = lax.fori_loop(0, ngrp // 2, two, accs)
        accs = lax.fori_loop(2 * (ngrp // 2), ngrp, one, accs)
        return [jnp.sum(a.astype(I32), axis=0, keepdims=True) for a in accs]

    def bisect16(ref, want, n_all):
        def step(i, carry):
            ts, n_ge, n_gt = carry
            cands = [t + lax.shift_left(jnp.int32(1), 15 - i) for t in ts]
            cnts = count16(ref, cands)
            ok = [cnts[bi] >= want[bi] for bi in batches]
            return (tuple(jnp.where(ok[bi], cands[bi], ts[bi]) for bi in batches),
                    tuple(jnp.where(ok[bi], cnts[bi], n_ge[bi]) for bi in batches),
                    tuple(jnp.where(ok[bi], n_gt[bi], cnts[bi]) for bi in batches))
        init = (tuple(jnp.full((1, QW), -32768, I32) for _ in batches), tuple(n_all),
                tuple(jnp.zeros((1, QW), I32) for _ in batches))
        return lax.fori_loop(0, 16, step, init)

    want_hi = [jnp.full((1, QW), k_sel, I32) for _ in batches]
    n_rows = [jnp.zeros((1, QW), I32) + ngrp * KG for _ in batches]
    p_hi, n_ge_hi, n_gt_hi = bisect16(hi_ref, want_hi, n_rows)

    def keep_ties(g, carry):
        rows = rows_of(g)
        for bi in batches:
            lo_ref[bi, rows, :] = jnp.where(hi_ref[bi, rows, :] == p_hi[bi].astype(I16), lo_ref[bi, rows, :],
                                            jnp.int16(-32768))
        return carry

    lax.fori_loop(0, ngrp, keep_ties, 0)
    want_lo = [k_sel - n_gt_hi[bi] for bi in batches]
    n_eq_hi = [n_ge_hi[bi] - n_gt_hi[bi] for bi in batches]
    p_lo, n_ge_lo, n_gt_lo = bisect16(lo_ref, want_lo, n_eq_hi)
    for bi in batches:
        thr_ref[bi] = lax.shift_left(p_hi[bi], 16) + (p_lo[bi] + 32768)
        nge_ref[bi] = n_gt_hi[bi] + n_ge_lo[bi]
        ngt_ref[bi] = n_gt_hi[bi] + n_gt_lo[bi]

    def as_float(key):
        return lax.bitcast_convert_type(_order_key(key), F32)

    def count32(cands):
        def body(g, accs):
            rows = rows_of(g)
            out = []
            for bi in batches:
                ind = jnp.where(sc_ref[bi, rows, :] >= cands[bi], 1, 0)
                out.append(accs[bi] + jnp.sum(ind.reshape(KG // 8, 8, QW), axis=0))
            return tuple(out)

        accs = lax.fori_loop(0, ngrp, body, tuple(jnp.zeros((8, QW), I32) for _ in batches))
        return [jnp.sum(a, axis=0, keepdims=True) for a in accs]

    n_ge_f = count32([as_float(thr_ref[bi]) for bi in batches])
    differs = _tree_sum([jnp.where(n_ge_f[bi] != nge_ref[bi], 1, 0) for bi in batches])

    @pl.when(jnp.max(differs) > 0)
    def _():
        def step(i, carry):
            ts, n_ge, n_gt = carry
            cands = [t + lax.shift_left(jnp.int32(1), 31 - i) for t in ts]
            cnts = count32([as_float(c) for c in cands])
            ok = [cnts[bi] >= k_sel for bi in batches]
            return (tuple(jnp.where(ok[bi], cands[bi], ts[bi]) for bi in batches),
                    tuple(jnp.where(ok[bi], cnts[bi], n_ge[bi]) for bi in batches),
                    tuple(jnp.where(ok[bi], n_gt[bi], cnts[bi]) for bi in batches))
        init = (tuple(jnp.full((1, QW), INT_MIN, I32) for _ in batches), tuple(n_rows),
                tuple(jnp.zeros((1, QW), I32) for _ in batches))
        ts, n_ge, n_gt = lax.fori_loop(0, 32, step, init)
        for bi in batches:
            thr_ref[bi] = ts[bi]
            nge_ref[bi] = n_ge[bi]
            ngt_ref[bi] = n_gt[bi]

    thr = [as_float(thr_ref[bi]) for bi in batches]

    excess = _tree_sum([jnp.where((nge_ref[bi] > k_sel) & (thr[bi] > -jnp.inf), 1, 0) for bi in batches])

    @pl.when(jnp.max(excess) > 0)
    def _():
        tri = jnp.where(row_iota >= lane_iota, 1.0, 0.0).astype(BF16)
        need = [(k_sel - ngt_ref[bi]).astype(F32) for bi in batches]

        def demote(g, carry):
            rows = rows_of(g)
            out = []
            for bi in batches:
                blk = sc_ref[bi, rows, :]
                eq = jnp.where(blk == thr[bi], 1.0, 0.0)
                incl = jnp.dot(tri, eq.astype(BF16), preferred_element_type=F32)
                rank = incl - eq + carry[bi]
                sc_ref[bi, rows, :] = jnp.where((blk == thr[bi]) & (rank >= need[bi]), -jnp.inf, blk)
                out.append(carry[bi] + incl[KG - 1:KG, :])
            return tuple(out)

        lax.fori_loop(0, ngrp, demote, tuple(jnp.zeros((1, QW), F32) for _ in batches))

    thr_sel = [jnp.maximum(t, jnp.finfo(F32).min) for t in thr]
    for bi in batches:
        m_ref[bi] = jnp.full((1, hq), NEG_BIG, F32)
        acc_ref[bi] = jnp.zeros((CT_ROWS, hq), F32)

    def attend(g, with_bias):
        half = KG // 2
        chains = [(bi, h) for bi in batches for h in range(A_HEADS)]
        sel_h, c_h = {}, {}
        for bi in batches:
            for u in range(2):
                rows = pl.ds(pl.multiple_of(g * KG + u * half, half), half)
                sel_h[bi, u] = jnp.where(sc_ref[bi, rows, :] >= thr_sel[bi], 0.0, NEG_BIG)
                c_h[bi, u] = c_ref[bi, rows, :]
        s_all = {}
        for bi, h in chains:
            for u in range(2):
                s_all[bi, h, u] = jnp.dot(c_h[bi, u], qlt_ref[bi, h], preferred_element_type=F32)
        p_all = {}
        alpha_all = {}
        for bi, h in chains:
            cols = slice(h * QW, (h + 1) * QW)
            m_loc, p_loc = [], []
            for u in range(2):
                s = s_all[bi, h, u]
                if with_bias:
                    s = s + bias_ref[j - g, u * half:(u + 1) * half, cols]
                s = s + sel_h[bi, u]
                m_u = jnp.max(s, axis=0, keepdims=True)
                m_loc.append(m_u)
                p_loc.append(jnp.exp2(s - m_u).astype(BF16))
            m_old = m_ref[bi, :, cols]
            m_new = jnp.maximum(m_old, jnp.maximum(m_loc[0], m_loc[1]))
            alpha_all[bi, h] = jnp.exp2(m_old - m_new)
            p_all[bi, h] = jnp.concatenate([p_loc[u] * jnp.exp2(m_loc[u] - m_new).astype(BF16) for u in range(2)],
                                           axis=0)
            m_ref[bi, :, cols] = m_new
        for bi, h in chains:
            cols = slice(h * QW, (h + 1) * QW)
            pv = jnp.dot(ct_ref[bi, g], p_all[bi, h], preferred_element_type=F32)
            acc_ref[bi, :, cols] = acc_ref[bi, :, cols] * alpha_all[bi, h] + pv

    def attend_far(g, carry):
        attend(g, False)
        return carry

    def attend_near(g, carry):
        attend(g, True)
        return carry

    near0 = jnp.maximum(j - 1, 0)
    lax.fori_loop(0, near0, attend_far, 0)
    lax.fori_loop(near0, ngrp, attend_near, 0)

    for bi in batches:
        ot = acc_ref[bi, :A_KV_RANK, :] * (1.0 / acc_ref[bi, A_KV_RANK:A_KV_RANK + 1, :])
        for h in range(A_HEADS):
            o_h = ot[:, h * QW:(h + 1) * QW].T
            oa = jnp.dot(o_h.astype(BF16), wuv_ref[h], preferred_element_type=F32)
            o_ref[bi, :, h * A_HEAD_DIM:(h + 1) * A_HEAD_DIM] = oa.astype(BF16)


def _dsa(iqt, iwt, ik, c, ctb, qlt, bias, w_uv, bsz, seq):
    assert seq % QW == 0 and QW % Q_BLOCK == 0
    nb = DSA_NB if bsz % DSA_NB == 0 else 1
    k_sel = min(TOPK_MAX, seq // 4)
    hq = A_HEADS * QW
    return pl.pallas_call(
        functools.partial(_dsa_kernel, k_sel=k_sel, nb=nb),
        grid=(bsz // nb, seq // QW),
        in_specs=[
            pl.BlockSpec((nb, IDX_HEADS * IDX_DIM, QW), lambda b, j: (b, 0, j)),
            pl.BlockSpec((nb, IDX_HEADS, QW), lambda b, j: (b, 0, j)),
            pl.BlockSpec((nb, seq, IDX_DIM), lambda b, j: (b, 0, 0)),
            pl.BlockSpec((nb, seq, A_KV_RANK), lambda b, j: (b, 0, 0)),
            pl.BlockSpec((nb, seq // KG, CT_ROWS, KG), lambda b, j: (b, 0, 0, 0)),
            pl.BlockSpec((nb, A_HEADS, A_KV_RANK, QW), lambda b, j: (b, 0, 0, j)),
            pl.BlockSpec((2, KG, hq), lambda b, j: (0, 0, 0)),
            pl.BlockSpec((A_HEADS, A_KV_RANK, A_HEAD_DIM), lambda b, j: (0, 0, 0)),
        ],
        out_specs=pl.BlockSpec((nb, QW, A_WIDTH), lambda b, j: (b, j, 0)),
        out_shape=jax.ShapeDtypeStruct((bsz, seq, A_WIDTH), BF16),
        scratch_shapes=[
            pltpu.VMEM((nb, seq, QW), F32),
            pltpu.VMEM((nb, seq, QW), I16),
            pltpu.VMEM((nb, seq, QW), I16),
            pltpu.VMEM((nb, CT_ROWS, hq), F32),
            pltpu.VMEM((nb, 1, hq), F32),
            pltpu.VMEM((nb, 1, QW), I32),
            pltpu.VMEM((nb, 1, QW), I32),
            pltpu.VMEM((nb, 1, QW), I32),
        ],
        compiler_params=pltpu.CompilerParams(dimension_semantics=("parallel", "arbitrary"),
                                             vmem_limit_bytes=VMEM_LIMIT),
        name="dsa",
    )(iqt, iwt, ik.reshape(bsz, seq, IDX_DIM), c.reshape(bsz, seq, A_KV_RANK), ctb, qlt, bias,
      w_uv.astype(BF16))


def _hgrn_kernel(qs_ref, logf_ref, v_ref, o_ref, st_ref, *, nb):
    @pl.when(pl.program_id(1) == 0)
    def _():
        st_ref[...] = jnp.zeros(st_ref.shape, F32)

    steps = qs_ref.shape[1]
    nchunk = steps // CHUNK
    dk, dv = B_KEY_DIM, B_VAL_DIM
    r64 = lax.broadcasted_iota(I32, (CHUNK, CHUNK), 0)
    c64 = lax.broadcasted_iota(I32, (CHUNK, CHUNK), 1)
    tri = jnp.where(r64 >= c64, 1.0, 0.0).astype(BF16)
    rt = lax.broadcasted_iota(I32, (CHUNK, LANES), 0)
    cs = lax.broadcasted_iota(I32, (CHUNK, LANES), 1)
    causal = cs <= rt
    nsub = CHUNK // SUB
    zpad_f = jnp.zeros((LANES - CHUNK, LANES), F32)
    nt = (((1,), (1,)), ((), ()))
    units = [(bi, ci, h) for bi in range(nb) for ci in range(nchunk) for h in range(B_HEADS)]

    lf_all, b_all = {}, {}
    for bi in range(nb):
        for ci in range(nchunk):
            lf = logf_ref[bi, ci * CHUNK:(ci + 1) * CHUNK, :]
            hi = lf.astype(BF16)
            lo = (lf - hi.astype(F32)).astype(BF16)
            lf_all[bi, ci] = lf
            b_all[bi, ci] = (jnp.dot(tri, hi, preferred_element_type=F32)
                             + jnp.dot(tri, lo, preferred_element_type=F32))

    q_in, q_st, kst, k_pad, v_t, e_last = {}, {}, {}, {}, {}, {}
    r_all, lhs, upd = {}, {}, {}

    def operands(u):
        bi, ci, h = u
        rows = slice(ci * CHUNK, (ci + 1) * CHUNK)
        cols = slice(h * dk, (h + 1) * dk)
        b = b_all[bi, ci][:, cols]
        kk = 1.0 - jnp.exp(lf_all[bi, ci][:, cols])
        q = qs_ref[bi, rows, cols].astype(F32)
        b_last = b[CHUNK - 1:CHUNK, :]
        starts = [jnp.zeros((1, dk), F32)] + [b[i * SUB - 1:i * SUB, :] for i in range(1, nsub)]
        ref_rows = jnp.concatenate([jnp.broadcast_to(s_, (SUB, dk)) for s_ in starts], axis=0)
        qi = q * jnp.exp(b - ref_rows)
        q_in[bi, ci, h] = qi.astype(BF16)
        q_st[bi, ci, h] = (qi * jnp.exp(ref_rows)).astype(BF16)
        secs = []
        for i in range(nsub):
            n = SUB * (i + 1)
            e = jnp.exp(jnp.minimum(starts[i] - b[:n, :], EXP_CAP))
            ks = (kk[:n, :] * e).astype(BF16)
            secs.append(ks if n == CHUNK else jnp.concatenate([ks, jnp.zeros((CHUNK - n, dk), BF16)], axis=0))
        kst[bi, ci, h] = jnp.concatenate(secs, axis=0)
        k_hat = (kk * jnp.exp(b_last - b)).astype(BF16)
        k_pad[bi, ci, h] = jnp.concatenate([k_hat, jnp.zeros((LANES - CHUNK, dk), BF16)], axis=0)
        v = v_ref[bi, rows, h * dv:(h + 1) * dv]
        v_t[bi, ci, h] = jnp.concatenate([v.astype(F32), zpad_f], axis=0).T.astype(BF16)
        e_last[bi, ci, h] = jnp.exp(b_last)

    def matmuls(u):
        r_all[u] = lax.dot_general(q_in[u], kst[u], nt, preferred_element_type=F32)
        upd[u] = jnp.dot(v_t[u], k_pad[u], preferred_element_type=F32)

    def scores(u):
        r = r_all[u]
        slabs = []
        for i in range(nsub):
            lane0 = (i * CHUNK // LANES) * LANES
            slab = r[i * SUB:(i + 1) * SUB, lane0:lane0 + LANES]
            if (i * CHUNK) % LANES:
                slab = pltpu.roll(slab, LANES - (i * CHUNK) % LANES, 1)
            slabs.append(slab)
        a = jnp.concatenate(slabs, axis=0)
        a = jnp.where(causal, a, 0.0).astype(BF16)
        lhs[u] = jnp.concatenate([a, q_st[u]], axis=1)

    lead = 2
    for u in units[:lead]:
        operands(u)
    for i, u in enumerate(units):
        matmuls(u)
        if i + lead < len(units):
            operands(units[i + lead])
        scores(u)

    chains = [(bi, h) for bi in range(nb) for h in range(B_HEADS)]
    st = {ch: st_ref[ch[0], ch[1]] for ch in chains}
    for ci in range(nchunk):
        for bi, h in chains:
            u = (bi, ci, h)
            w = jnp.concatenate([v_t[u], st[bi, h].astype(BF16)], axis=1)
            o = lax.dot_general(lhs[u], w, nt, preferred_element_type=F32)
            o_ref[bi, ci * CHUNK:(ci + 1) * CHUNK, h * dv:(h + 1) * dv] = o.astype(BF16)
        for bi, h in chains:
            st[bi, h] = st[bi, h] * e_last[bi, ci, h] + upd[bi, ci, h]
    for bi, h in chains:
        st_ref[bi, h] = st[bi, h]


def _hgrn(qs, logf, v, bsz, seq):
    steps = min(HGRN_STEPS, seq)
    nb = HGRN_NB if bsz % HGRN_NB == 0 else 1
    spec = lambda n: pl.BlockSpec((nb, steps, n), lambda b, s: (b, s, 0))
    return pl.pallas_call(
        functools.partial(_hgrn_kernel, nb=nb),
        grid=(bsz // nb, seq // steps),
        in_specs=[spec(B_FORGET), spec(B_FORGET), spec(B_WIDTH)],
        out_specs=spec(B_WIDTH),
        out_shape=jax.ShapeDtypeStruct((bsz, seq, B_WIDTH), BF16),
        scratch_shapes=[pltpu.VMEM((nb, B_HEADS, B_VAL_DIM, B_KEY_DIM), F32)],
        compiler_params=pltpu.CompilerParams(dimension_semantics=("parallel", "arbitrary"),
                                             vmem_limit_bytes=VMEM_LIMIT),
        name="hgrn",
    )(qs.reshape(bsz, seq, B_FORGET), logf.reshape(bsz, seq, B_FORGET), v.reshape(bsz, seq, B_WIDTH))


def _out_kernel(oa_ref, ga_ref, ob_ref, gb_ref, x_ref, wo_ref, hg_ref, lng_ref, lnb_ref, out_ref):
    oa = oa_ref[...].astype(F32) * ga_ref[...].astype(F32)
    ob = ob_ref[...].astype(F32)
    parts = []
    for h in range(B_HEADS):
        oh = ob[:, h * B_VAL_DIM:(h + 1) * B_VAL_DIM]
        ms = jnp.mean(oh * oh, axis=-1, keepdims=True)
        parts.append(oh * lax.rsqrt(ms + RMS_EPS) * hg_ref[:, h * B_VAL_DIM:(h + 1) * B_VAL_DIM])
    obn = jnp.concatenate(parts, axis=1) * gb_ref[...].astype(F32)
    z = jnp.concatenate([oa, obn], axis=1).astype(BF16)
    y = jnp.dot(z, wo_ref[...], preferred_element_type=F32)
    r = DEEPNORM_ALPHA * x_ref[...] + y
    mu = jnp.mean(r, axis=-1, keepdims=True)
    rc = r - mu
    var = jnp.mean(rc * rc, axis=-1, keepdims=True)
    out_ref[...] = rc * lax.rsqrt(var + LN_EPS) * lng_ref[...] + lnb_ref[...]


def _output(oa, ga, ob, gb, x2, w_o, hg, ln_g, ln_b):
    t = x2.shape[0]
    tq = min(OUT_TILE, t)
    row = lambda n: pl.BlockSpec((tq, n), lambda i: (i, 0))
    full = lambda shape: pl.BlockSpec(shape, lambda i: (0,) * len(shape))
    d_mix = A_WIDTH + B_WIDTH
    return pl.pallas_call(
        _out_kernel,
        grid=(t // tq,),
        in_specs=[row(A_WIDTH), row(A_WIDTH), row(B_WIDTH), row(B_WIDTH), row(D_MODEL),
                  full((d_mix, D_MODEL)), full((1, B_WIDTH)), full((1, D_MODEL)), full((1, D_MODEL))],
        out_specs=row(D_MODEL),
        out_shape=jax.ShapeDtypeStruct((t, D_MODEL), F32),
        compiler_params=pltpu.CompilerParams(dimension_semantics=("parallel",), vmem_limit_bytes=VMEM_LIMIT),
        name="out",
    )(oa, ga, ob, gb, x2, w_o.astype(BF16), hg.reshape(1, B_WIDTH).astype(F32),
      ln_g.reshape(1, D_MODEL).astype(F32), ln_b.reshape(1, D_MODEL).astype(F32))


def kernel(x, w_in, w_uk, w_uv, kv_norm_g, rel_bias, lb_logits, hgrn_norm_g, w_o, ln_g, ln_b):
    bsz, seq, d = x.shape
    assert d == D_MODEL and seq % Q_BLOCK == 0 and DEPTH == 1
    t = bsz * seq
    x2 = x.reshape(t, d)
    bias = _bias_tiles(rel_bias)
    ik, c, ga, gb, qs, logf, v, iqt, iwt, qlt, ctb = _project(
        x2, w_in[0], w_uk[0], kv_norm_g[0], lb_logits, bsz, seq)
    oa = _dsa(iqt, iwt, ik, c, ctb, qlt, bias, w_uv[0], bsz, seq)
    ob = _hgrn(qs, logf, v, bsz, seq)
    out = _output(oa.reshape(t, A_WIDTH), ga, ob.reshape(t, B_WIDTH), gb, x2, w_o[0],
                  hgrn_norm_g[0], ln_g[0], ln_b[0])
    return out.reshape(bsz, seq, d)
```

```python
import functools
import math

import jax
import jax.numpy as jnp
from jax import lax
from jax.experimental import pallas as pl
from jax.experimental.pallas import tpu as pltpu

F32 = jnp.float32
BF16 = jnp.bfloat16
I32 = jnp.int32
I16 = jnp.int16

D_MODEL = 1024
DEPTH = 1
CHUNK = 64
Q_BLOCK = 128
A_HEADS = 4
A_HEAD_DIM = 128
A_WIDTH = A_HEADS * A_HEAD_DIM
A_KV_RANK = 128
IDX_HEADS = 8
IDX_DIM = 64
TOPK_MAX = 256
B_HEADS = 4
B_KEY_DIM = 128
B_VAL_DIM = 128
B_WIDTH = B_HEADS * B_VAL_DIM
B_FORGET = B_HEADS * B_KEY_DIM
REL_BUCKETS = 32
REL_MAX_DIST = 256
DEEPNORM_ALPHA = (2.0 * DEPTH) ** 0.25
LN_EPS = 1e-5
RMS_EPS = 1e-6

_SPLITS = (
    ('a_q', A_WIDTH), ('a_ckv', A_KV_RANK), ('a_iq', IDX_HEADS * IDX_DIM), ('a_ik', IDX_DIM),
    ('a_iw', IDX_HEADS), ('a_gate', A_WIDTH), ('b_q', B_FORGET), ('b_f', B_FORGET),
    ('b_i', B_WIDTH), ('b_gate', B_WIDTH),
)
_OFFSETS = {}
_off = 0
for _name, _n in _SPLITS:
    _OFFSETS[_name] = (_off, _n)
    _off += _n

LANES = 128
QW = 256
KG = 256
DSA_NB = 4
LO_LAZY_BITS = 4
I16_ROWS = 16
CT_ROWS = A_KV_RANK + I16_ROWS
SUB = 16
HGRN_STEPS = 512
HGRN_NB = 2
EXP_CAP = 80.0
NEG_BIG = -1e30
LOG2_E = math.log2(math.e)
INT_MIN = -2 ** 31
PROJ_TILE = 512
OUT_TILE = 1024
VMEM_LIMIT = 56 * 1024 * 1024


def _col(w, name):
    o, n = _OFFSETS[name]
    return w[:, o:o + n]


def _t5_bucket(rel):
    nb = REL_BUCKETS // 2
    max_exact = nb // 2
    ret = jnp.where(rel > 0, nb, 0).astype(I32)
    n = jnp.abs(rel)
    nf = jnp.maximum(n, max_exact).astype(F32)
    large = max_exact + jnp.floor(jnp.log(nf / max_exact) / math.log(REL_MAX_DIST / max_exact)
                                  * (nb - max_exact)).astype(I32)
    large = jnp.minimum(large, nb - 1)
    return ret + jnp.where(n < max_exact, n, large)


def _bias_kernel(rb_ref, out_ref):
    sl = lax.broadcasted_iota(I32, (KG, QW), 0)
    ql = lax.broadcasted_iota(I32, (KG, QW), 1)
    far = REL_BUCKETS // 2 - 1
    for d in range(2):
        bucket = _t5_bucket(sl - ql - KG * d)
        for h in range(A_HEADS):
            acc = jnp.zeros((KG, QW), F32)
            for bk in range(REL_BUCKETS):
                acc = jnp.where(bucket == bk, (rb_ref[bk, h] - rb_ref[far, h]) * LOG2_E, acc)
            out_ref[d, :, h * QW:(h + 1) * QW] = acc


def _bias_tiles(rel_bias):
    assert KG == QW and KG >= REL_MAX_DIST
    return pl.pallas_call(
        _bias_kernel,
        out_shape=jax.ShapeDtypeStruct((2, KG, A_HEADS * QW), F32),
        in_specs=[pl.BlockSpec(memory_space=pltpu.SMEM)],
        out_specs=pl.BlockSpec(memory_space=pltpu.VMEM),
        name="bias",
    )(rel_bias.astype(F32))


def _proj_kernel(x_ref, wrow_ref, wt_ref, wukt_ref, g_ref, lbl_ref,
                 ik_ref, c_ref, ga_ref, gb_ref, qs_ref, logf_ref, v_ref,
                 iqt_ref, iwt_ref, qlt_ref, ct_ref):
    tq = x_ref.shape[0]
    xb = x_ref[...].astype(BF16)

    def rowdot(lo, n):
        return jnp.dot(xb, wrow_ref[:, lo:lo + n], preferred_element_type=F32)

    h0 = rowdot(0, 256)
    a = h0[:, :A_KV_RANK]
    ms = jnp.mean(a * a, axis=-1, keepdims=True)
    c = a * lax.rsqrt(ms + RMS_EPS) * g_ref[...]
    c_ref[...] = c.astype(BF16)
    ct = c.T
    ones = jnp.ones((CT_ROWS - A_KV_RANK, KG), BF16)
    for i in range(tq // KG):
        ct_ref[0, i] = jnp.concatenate([ct[:, i * KG:(i + 1) * KG].astype(BF16), ones], axis=0)
    ik_ref[...] = h0[:, A_KV_RANK:A_KV_RANK + IDX_DIM].astype(BF16)

    ga_ref[...] = jax.nn.silu(rowdot(256, A_WIDTH)).astype(BF16)
    qs_ref[...] = jax.nn.silu(rowdot(768, B_FORGET)).astype(BF16)
    lbl = lbl_ref[...]
    e = jnp.exp(lbl - jnp.max(lbl, axis=0, keepdims=True))
    lb = e[0:1, :] / jnp.sum(e, axis=0, keepdims=True)
    f = lb + (1.0 - lb) * jax.nn.sigmoid(rowdot(1280, B_FORGET))
    logf_ref[...] = jnp.log(f)
    v_ref[...] = rowdot(1792, B_WIDTH).astype(BF16)
    gb_ref[...] = jax.nn.silu(rowdot(2304, B_WIDTH)).astype(BF16)

    ht = lax.dot_general(wt_ref[...], xb, (((1,), (1,)), ((), ())), preferred_element_type=F32)
    iqt_ref[0] = ht[A_WIDTH:A_WIDTH + IDX_HEADS * IDX_DIM].astype(BF16)
    o = A_WIDTH + IDX_HEADS * IDX_DIM
    iwt_ref[0] = ht[o:o + IDX_HEADS] * (IDX_HEADS ** -0.5 * IDX_DIM ** -0.5)
    for h in range(A_HEADS):
        aq = ht[h * A_HEAD_DIM:(h + 1) * A_HEAD_DIM].astype(BF16)
        ql = jnp.dot(wukt_ref[h], aq, preferred_element_type=F32) * (A_HEAD_DIM ** -0.5 * LOG2_E)
        qlt_ref[0, h] = ql.astype(BF16)


def _project(x2, w_in, w_uk, kv_g, lb_logits, bsz, seq):
    t = bsz * seq
    tq = min(PROJ_TILE, seq)
    per_b = seq // tq
    pad64 = jnp.zeros((D_MODEL, 64), F32)
    wrow = jnp.concatenate([_col(w_in, 'a_ckv'), _col(w_in, 'a_ik'), pad64, _col(w_in, 'a_gate'),
                            _col(w_in, 'b_q'), _col(w_in, 'b_f'), _col(w_in, 'b_i'),
                            _col(w_in, 'b_gate')], axis=1).astype(BF16)
    pad8 = jnp.zeros((D_MODEL, 8), F32)
    wt = jnp.concatenate([_col(w_in, 'a_q'), _col(w_in, 'a_iq'), _col(w_in, 'a_iw'), pad8],
                         axis=1).T.astype(BF16)
    wukt = jnp.swapaxes(w_uk, 1, 2).astype(BF16)
    nrow, nt = wrow.shape[1], wt.shape[0]
    row = lambda n: pl.BlockSpec((tq, n), lambda i: (i, 0))
    full = lambda shape: pl.BlockSpec(shape, lambda i: (0,) * len(shape))
    out_shape = (
        jax.ShapeDtypeStruct((t, IDX_DIM), BF16),
        jax.ShapeDtypeStruct((t, A_KV_RANK), BF16),
        jax.ShapeDtypeStruct((t, A_WIDTH), BF16),
        jax.ShapeDtypeStruct((t, B_WIDTH), BF16),
        jax.ShapeDtypeStruct((t, B_FORGET), BF16),
        jax.ShapeDtypeStruct((t, B_FORGET), F32),
        jax.ShapeDtypeStruct((t, B_WIDTH), BF16),
        jax.ShapeDtypeStruct((bsz, IDX_HEADS * IDX_DIM, seq), BF16),
        jax.ShapeDtypeStruct((bsz, IDX_HEADS, seq), F32),
        jax.ShapeDtypeStruct((bsz, A_HEADS, A_KV_RANK, seq), BF16),
        jax.ShapeDtypeStruct((bsz, seq // KG, CT_ROWS, KG), BF16),
    )
    out_specs = (
        row(IDX_DIM), row(A_KV_RANK), row(A_WIDTH), row(B_WIDTH), row(B_FORGET), row(B_FORGET), row(B_WIDTH),
        pl.BlockSpec((1, IDX_HEADS * IDX_DIM, tq), lambda i: (i // per_b, 0, i % per_b)),
        pl.BlockSpec((1, IDX_HEADS, tq), lambda i: (i // per_b, 0, i % per_b)),
        pl.BlockSpec((1, A_HEADS, A_KV_RANK, tq), lambda i: (i // per_b, 0, 0, i % per_b)),
        pl.BlockSpec((1, tq // KG, CT_ROWS, KG), lambda i: (i // per_b, i % per_b, 0, 0)),
    )
    return pl.pallas_call(
        _proj_kernel,
        grid=(t // tq,),
        in_specs=[row(D_MODEL), full((D_MODEL, nrow)), full((nt, D_MODEL)),
                  full((A_HEADS, A_KV_RANK, A_HEAD_DIM)), full((1, A_KV_RANK)), full((DEPTH + 1, B_FORGET))],
        out_specs=out_specs,
        out_shape=out_shape,
        compiler_params=pltpu.CompilerParams(dimension_semantics=("parallel",), vmem_limit_bytes=VMEM_LIMIT),
        name="proj",
    )(x2, wrow, wt, wukt, kv_g.reshape(1, A_KV_RANK).astype(F32), lb_logits.astype(F32))


def _chunk_of(pos):
    return lax.shift_right_logical(pos, int(math.log2(CHUNK)))


def _tree_sum(parts):
    while len(parts) > 1:
        parts = [a + b for a, b in zip(parts[::2], parts[1::2])] + ([parts[-1]] if len(parts) % 2 else [])
    return parts[0]


def _order_key(bits):
    return jnp.where(bits < 0, jnp.int32(INT_MIN) - bits, bits)


def _dsa_kernel(iqt_ref, iwt_ref, ik_ref, c_ref, ct_ref, qlt_ref, bias_ref, wuv_ref, o_ref,
                sc_ref, hi_ref, lo_ref, acc_ref, m_ref, thr_ref, nge_ref, ngt_ref, *, k_sel, nb):
    j = pl.program_id(1)
    ngrp = j + 1
    hq = A_HEADS * QW
    batches = range(nb)
    row_iota = lax.broadcasted_iota(I32, (KG, QW), 0)
    lane_iota = lax.broadcasted_iota(I32, (KG, QW), 1)
    q_chunk = _chunk_of(j * QW + lane_iota)

    def rows_of(g):
        return pl.ds(pl.multiple_of(g * KG, KG), KG)

    def score_group(g, diagonal):
        rows = rows_of(g)
        for bi in batches:
            rhs = jnp.concatenate([iqt_ref[bi, h * IDX_DIM:(h + 1) * IDX_DIM, :] for h in range(IDX_HEADS)],
                                  axis=1)
            d = jnp.dot(ik_ref[bi, rows, :], rhs, preferred_element_type=F32)
            sc = iwt_ref[bi, 0:1, :] * jnp.maximum(d[:, :QW], 0.0)
            for h in range(1, IDX_HEADS):
                sc = sc + iwt_ref[bi, h:h + 1, :] * jnp.maximum(d[:, h * QW:(h + 1) * QW], 0.0)
            if diagonal:
                sc = jnp.where(_chunk_of(g * KG + row_iota) <= q_chunk, sc, -jnp.inf)
            sc_ref[bi, rows, :] = sc
            key = _order_key(lax.bitcast_convert_type(sc, I32))
            hi_ref[bi, rows, :] = (key >> 16).astype(I16)
            lo_ref[bi, rows, :] = (key ^ 0x8000).astype(I16)

    def score_inner(g, carry):
        score_group(g, False)
        return carry

    lax.fori_loop(0, j, score_inner, 0)
    score_group(j, True)

    def count16(ref, cands):
        c16 = [c.astype(I16) for c in cands]

        def one(g, accs):
            rows = rows_of(g)
            out = []
            for bi in batches:
                blk = ref[bi, rows, :]
                ind = jnp.where(blk >= c16[bi], jnp.int16(1), jnp.int16(0))
                out.append(accs[bi] + _tree_sum([ind[r:r + I16_ROWS] for r in range(0, KG, I16_ROWS)]))
            return tuple(out)

        def two(p, accs):
            return one(2 * p + 1, one(2 * p, accs))

        accs = tuple(jnp.zeros((I16_ROWS, QW), I16) for _ in batches)
        accs = lax.fori_loop(0, ngrp // 2, two, accs)
        accs = lax.fori_loop(2 * (ngrp // 2), ngrp, one, accs)
        return [jnp.sum(a.astype(I32), axis=0, keepdims=True) for a in accs]

    def bisect16(ref, want, state, first_bit, last_bit):
        def step(i, carry):
            ts, n_ge, n_gt = carry
            cands = [t + lax.shift_left(jnp.int32(1), 15 - i) for t in ts]
            cnts = count16(ref, cands)
            ok = [cnts[bi] >= want[bi] for bi in batches]
            return (tuple(jnp.where(ok[bi], cands[bi], ts[bi]) for bi in batches),
                    tuple(jnp.where(ok[bi], cnts[bi], n_ge[bi]) for bi in batches),
                    tuple(jnp.where(ok[bi], n_gt[bi], cnts[bi]) for bi in batches))
        return lax.fori_loop(first_bit, last_bit, step, state)

    def start16(n_all):
        return (tuple(jnp.full((1, QW), -32768, I32) for _ in batches), tuple(n_all),
                tuple(jnp.zeros((1, QW), I32) for _ in batches))

    want_hi = [jnp.full((1, QW), k_sel, I32) for _ in batches]
    n_rows = [jnp.zeros((1, QW), I32) + ngrp * KG for _ in batches]
    p_hi, n_ge_hi, n_gt_hi = bisect16(hi_ref, want_hi, start16(n_rows), 0, 16)

    def keep_ties(g, carry):
        rows = rows_of(g)
        for bi in batches:
            lo_ref[bi, rows, :] = jnp.where(hi_ref[bi, rows, :] == p_hi[bi].astype(I16), lo_ref[bi, rows, :],
                                            jnp.int16(-32768))
        return carry

    lax.fori_loop(0, ngrp, keep_ties, 0)
    want_lo = [k_sel - n_gt_hi[bi] for bi in batches]
    n_eq_hi = [n_ge_hi[bi] - n_gt_hi[bi] for bi in batches]

    def publish(state):
        p_lo, n_ge_lo, n_gt_lo = state
        for bi in batches:
            thr_ref[bi] = lax.shift_left(p_hi[bi], 16) + (p_lo[bi] + 32768)
            nge_ref[bi] = n_gt_hi[bi] + n_ge_lo[bi]
            ngt_ref[bi] = n_gt_hi[bi] + n_gt_lo[bi]

    lo_state = bisect16(lo_ref, want_lo, start16(n_eq_hi), 0, 16 - LO_LAZY_BITS)
    publish(lo_state)
    undecided = _tree_sum([jnp.where(lo_state[1][bi] != want_lo[bi], 1, 0) for bi in batches])

    @pl.when(jnp.max(undecided) > 0)
    def _():
        publish(bisect16(lo_ref, want_lo, lo_state, 16 - LO_LAZY_BITS, 16))

    def as_float(key):
        return lax.bitcast_convert_type(_order_key(key), F32)

    def count32(cands):
        def body(g, accs):
            rows = rows_of(g)
            out = []
            for bi in batches:
                ind = jnp.where(sc_ref[bi, rows, :] >= cands[bi], 1, 0)
                out.append(accs[bi] + jnp.sum(ind.reshape(KG // 8, 8, QW), axis=0))
            return tuple(out)

        accs = lax.fori_loop(0, ngrp, body, tuple(jnp.zeros((8, QW), I32) for _ in batches))
        return [jnp.sum(a, axis=0, keepdims=True) for a in accs]

    n_ge_f = count32([as_float(thr_ref[bi]) for bi in batches])
    differs = _tree_sum([jnp.where(n_ge_f[bi] != nge_ref[bi], 1, 0) for bi in batches])

    @pl.when(jnp.max(differs) > 0)
    def _():
        def step(i, carry):
            ts, n_ge, n_gt = carry
            cands = [t + lax.shift_left(jnp.int32(1), 31 - i) for t in ts]
            cnts = count32([as_float(c) for c in cands])
            ok = [cnts[bi] >= k_sel for bi in batches]
            return (tuple(jnp.where(ok[bi], cands[bi], ts[bi]) for bi in batches),
                    tuple(jnp.where(ok[bi], cnts[bi], n_ge[bi]) for bi in batches),
                    tuple(jnp.where(ok[bi], n_gt[bi], cnts[bi]) for bi in batches))
        init = (tuple(jnp.full((1, QW), INT_MIN, I32) for _ in batches), tuple(n_rows),
                tuple(jnp.zeros((1, QW), I32) for _ in batches))
        ts, n_ge, n_gt = lax.fori_loop(0, 32, step, init)
        for bi in batches:
            thr_ref[bi] = ts[bi]
            nge_ref[bi] = n_ge[bi]
            ngt_ref[bi] = n_gt[bi]

    thr = [as_float(thr_ref[bi]) for bi in batches]

    excess = _tree_sum([jnp.where((nge_ref[bi] > k_sel) & (thr[bi] > -jnp.inf), 1, 0) for bi in batches])

    @pl.when(jnp.max(excess) > 0)
    def _():
        tri = jnp.where(row_iota >= lane_iota, 1.0, 0.0).astype(BF16)
        need = [(k_sel - ngt_ref[bi]).astype(F32) for bi in batches]

        def demote(g, carry):
            rows = rows_of(g)
            out = []
            for bi in batches:
                blk = sc_ref[bi, rows, :]
                eq = jnp.where(blk == thr[bi], 1.0, 0.0)
                incl = jnp.dot(tri, eq.astype(BF16), preferred_element_type=F32)
                rank = incl - eq + carry[bi]
                sc_ref[bi, rows, :] = jnp.where((blk == thr[bi]) & (rank >= need[bi]), -jnp.inf, blk)
                out.append(carry[bi] + incl[KG - 1:KG, :])
            return tuple(out)

        lax.fori_loop(0, ngrp, demote, tuple(jnp.zeros((1, QW), F32) for _ in batches))

    thr_sel = [jnp.maximum(t, jnp.finfo(F32).min) for t in thr]
    for bi in batches:
        m_ref[bi] = jnp.full((1, hq), NEG_BIG, F32)
        acc_ref[bi] = jnp.zeros((CT_ROWS, hq), F32)

    def attend(g, with_bias):
        half = KG // 2
        chains = [(bi, h) for bi in batches for h in range(A_HEADS)]
        sel_h, c_h = {}, {}
        for bi in batches:
            for u in range(2):
                rows = pl.ds(pl.multiple_of(g * KG + u * half, half), half)
                sel_h[bi, u] = jnp.where(sc_ref[bi, rows, :] >= thr_sel[bi], 0.0, NEG_BIG)
                c_h[bi, u] = c_ref[bi, rows, :]
        s_all = {}
        for bi, h in chains:
            for u in range(2):
                s_all[bi, h, u] = jnp.dot(c_h[bi, u], qlt_ref[bi, h], preferred_element_type=F32)
        p_all = {}
        alpha_all = {}
        for bi, h in chains:
            cols = slice(h * QW, (h + 1) * QW)
            m_loc, p_loc = [], []
            for u in range(2):
                s = s_all[bi, h, u]
                if with_bias:
                    s = s + bias_ref[j - g, u * half:(u + 1) * half, cols]
                s = s + sel_h[bi, u]
                m_u = jnp.max(s, axis=0, keepdims=True)
                m_loc.append(m_u)
                p_loc.append(jnp.exp2(s - m_u).astype(BF16))
            m_old = m_ref[bi, :, cols]
            m_new = jnp.maximum(m_old, jnp.maximum(m_loc[0], m_loc[1]))
            alpha_all[bi, h] = jnp.exp2(m_old - m_new)
            p_all[bi, h] = jnp.concatenate([p_loc[u] * jnp.exp2(m_loc[u] - m_new).astype(BF16) for u in range(2)],
                                           axis=0)
            m_ref[bi, :, cols] = m_new
        for bi, h in chains:
            cols = slice(h * QW, (h + 1) * QW)
            pv = jnp.dot(ct_ref[bi, g], p_all[bi, h], preferred_element_type=F32)
            acc_ref[bi, :, cols] = acc_ref[bi, :, cols] * alpha_all[bi, h] + pv

    def attend_far(g, carry):
        attend(g, False)
        return carry

    def attend_near(g, carry):
        attend(g, True)
        return carry

    near0 = jnp.maximum(j - 1, 0)
    lax.fori_loop(0, near0, attend_far, 0)
    lax.fori_loop(near0, ngrp, attend_near, 0)

    for bi in batches:
        ot = acc_ref[bi, :A_KV_RANK, :] * (1.0 / acc_ref[bi, A_KV_RANK:A_KV_RANK + 1, :])
        for h in range(A_HEADS):
            o_h = ot[:, h * QW:(h + 1) * QW].T
            oa = jnp.dot(o_h.astype(BF16), wuv_ref[h], preferred_element_type=F32)
            o_ref[bi, :, h * A_HEAD_DIM:(h + 1) * A_HEAD_DIM] = oa.astype(BF16)


def _dsa(iqt, iwt, ik, c, ctb, qlt, bias, w_uv, bsz, seq):
    assert seq % QW == 0 and QW % Q_BLOCK == 0
    nb = DSA_NB if bsz % DSA_NB == 0 else 1
    k_sel = min(TOPK_MAX, seq // 4)
    hq = A_HEADS * QW
    return pl.pallas_call(
        functools.partial(_dsa_kernel, k_sel=k_sel, nb=nb),
        grid=(bsz // nb, seq // QW),
        in_specs=[
            pl.BlockSpec((nb, IDX_HEADS * IDX_DIM, QW), lambda b, j: (b, 0, j)),
            pl.BlockSpec((nb, IDX_HEADS, QW), lambda b, j: (b, 0, j)),
            pl.BlockSpec((nb, seq, IDX_DIM), lambda b, j: (b, 0, 0)),
            pl.BlockSpec((nb, seq, A_KV_RANK), lambda b, j: (b, 0, 0)),
            pl.BlockSpec((nb, seq // KG, CT_ROWS, KG), lambda b, j: (b, 0, 0, 0)),
            pl.BlockSpec((nb, A_HEADS, A_KV_RANK, QW), lambda b, j: (b, 0, 0, j)),
            pl.BlockSpec((2, KG, hq), lambda b, j: (0, 0, 0)),
            pl.BlockSpec((A_HEADS, A_KV_RANK, A_HEAD_DIM), lambda b, j: (0, 0, 0)),
        ],
        out_specs=pl.BlockSpec((nb, QW, A_WIDTH), lambda b, j: (b, j, 0)),
        out_shape=jax.ShapeDtypeStruct((bsz, seq, A_WIDTH), BF16),
        scratch_shapes=[
            pltpu.VMEM((nb, seq, QW), F32),
            pltpu.VMEM((nb, seq, QW), I16),
            pltpu.VMEM((nb, seq, QW), I16),
            pltpu.VMEM((nb, CT_ROWS, hq), F32),
            pltpu.VMEM((nb, 1, hq), F32),
            pltpu.VMEM((nb, 1, QW), I32),
            pltpu.VMEM((nb, 1, QW), I32),
            pltpu.VMEM((nb, 1, QW), I32),
        ],
        compiler_params=pltpu.CompilerParams(dimension_semantics=("parallel", "arbitrary"),
                                             vmem_limit_bytes=VMEM_LIMIT),
        name="dsa",
    )(iqt, iwt, ik.reshape(bsz, seq, IDX_DIM), c.reshape(bsz, seq, A_KV_RANK), ctb, qlt, bias,
      w_uv.astype(BF16))


def _hgrn_kernel(qs_ref, logf_ref, v_ref, o_ref, st_ref, *, nb):
    @pl.when(pl.program_id(1) == 0)
    def _():
        st_ref[...] = jnp.zeros(st_ref.shape, F32)

    steps = qs_ref.shape[1]
    nchunk = steps // CHUNK
    dk, dv = B_KEY_DIM, B_VAL_DIM
    r64 = lax.broadcasted_iota(I32, (CHUNK, CHUNK), 0)
    c64 = lax.broadcasted_iota(I32, (CHUNK, CHUNK), 1)
    tri = jnp.where(r64 >= c64, 1.0, 0.0).astype(BF16)
    rt = lax.broadcasted_iota(I32, (CHUNK, LANES), 0)
    cs = lax.broadcasted_iota(I32, (CHUNK, LANES), 1)
    causal = cs <= rt
    nsub = CHUNK // SUB
    zpad_f = jnp.zeros((LANES - CHUNK, LANES), F32)
    nt = (((1,), (1,)), ((), ()))
    units = [(bi, ci, h) for bi in range(nb) for ci in range(nchunk) for h in range(B_HEADS)]

    lf_all, b_all = {}, {}
    for bi in range(nb):
        for ci in range(nchunk):
            lf = logf_ref[bi, ci * CHUNK:(ci + 1) * CHUNK, :]
            hi = lf.astype(BF16)
            lo = (lf - hi.astype(F32)).astype(BF16)
            lf_all[bi, ci] = lf
            b_all[bi, ci] = (jnp.dot(tri, hi, preferred_element_type=F32)
                             + jnp.dot(tri, lo, preferred_element_type=F32))

    q_in, q_st, kst, k_pad, v_t, e_last = {}, {}, {}, {}, {}, {}
    r_all, lhs, upd = {}, {}, {}

    def operands(u):
        bi, ci, h = u
        rows = slice(ci * CHUNK, (ci + 1) * CHUNK)
        cols = slice(h * dk, (h + 1) * dk)
        b = b_all[bi, ci][:, cols]
        kk = 1.0 - jnp.exp(lf_all[bi, ci][:, cols])
        q = qs_ref[bi, rows, cols].astype(F32)
        b_last = b[CHUNK - 1:CHUNK, :]
        starts = [jnp.zeros((1, dk), F32)] + [b[i * SUB - 1:i * SUB, :] for i in range(1, nsub)]
        ref_rows = jnp.concatenate([jnp.broadcast_to(s_, (SUB, dk)) for s_ in starts], axis=0)
        qi = q * jnp.exp(b - ref_rows)
        q_in[bi, ci, h] = qi.astype(BF16)
        q_st[bi, ci, h] = (qi * jnp.exp(ref_rows)).astype(BF16)
        secs = []
        for i in range(nsub):
            n = SUB * (i + 1)
            e = jnp.exp(jnp.minimum(starts[i] - b[:n, :], EXP_CAP))
            ks = (kk[:n, :] * e).astype(BF16)
            secs.append(ks if n == CHUNK else jnp.concatenate([ks, jnp.zeros((CHUNK - n, dk), BF16)], axis=0))
        kst[bi, ci, h] = jnp.concatenate(secs, axis=0)
        k_hat = (kk * jnp.exp(b_last - b)).astype(BF16)
        k_pad[bi, ci, h] = jnp.concatenate([k_hat, jnp.zeros((LANES - CHUNK, dk), BF16)], axis=0)
        v = v_ref[bi, rows, h * dv:(h + 1) * dv]
        v_t[bi, ci, h] = jnp.concatenate([v.astype(F32), zpad_f], axis=0).T.astype(BF16)
        e_last[bi, ci, h] = jnp.exp(b_last)

    def matmuls(u):
        r_all[u] = lax.dot_general(q_in[u], kst[u], nt, preferred_element_type=F32)
        upd[u] = jnp.dot(v_t[u], k_pad[u], preferred_element_type=F32)

    def scores(u):
        r = r_all[u]
        slabs = []
        for i in range(nsub):
            lane0 = (i * CHUNK // LANES) * LANES
            slab = r[i * SUB:(i + 1) * SUB, lane0:lane0 + LANES]
            if (i * CHUNK) % LANES:
                slab = pltpu.roll(slab, LANES - (i * CHUNK) % LANES, 1)
            slabs.append(slab)
        a = jnp.concatenate(slabs, axis=0)
        a = jnp.where(causal, a, 0.0).astype(BF16)
        lhs[u] = jnp.concatenate([a, q_st[u]], axis=1)

    lead = 2
    for u in units[:lead]:
        operands(u)
    for i, u in enumerate(units):
        matmuls(u)
        if i + lead < len(units):
            operands(units[i + lead])
        scores(u)

    chains = [(bi, h) for bi in range(nb) for h in range(B_HEADS)]
    st = {ch: st_ref[ch[0], ch[1]] for ch in chains}
    for ci in range(nchunk):
        for bi, h in chains:
            u = (bi, ci, h)
            w = jnp.concatenate([v_t[u], st[bi, h].astype(BF16)], axis=1)
            o = lax.dot_general(lhs[u], w, nt, preferred_element_type=F32)
            o_ref[bi, ci * CHUNK:(ci + 1) * CHUNK, h * dv:(h + 1) * dv] = o.astype(BF16)
        for bi, h in chains:
            st[bi, h] = st[bi, h] * e_last[bi, ci, h] + upd[bi, ci, h]
    for bi, h in chains:
        st_ref[bi, h] = st[bi, h]


def _hgrn(qs, logf, v, bsz, seq):
    steps = min(HGRN_STEPS, seq)
    nb = HGRN_NB if bsz % HGRN_NB == 0 else 1
    spec = lambda n: pl.BlockSpec((nb, steps, n), lambda b, s: (b, s, 0))
    return pl.pallas_call(
        functools.partial(_hgrn_kernel, nb=nb),
        grid=(bsz // nb, seq // steps),
        in_specs=[spec(B_FORGET), spec(B_FORGET), spec(B_WIDTH)],
        out_specs=spec(B_WIDTH),
        out_shape=jax.ShapeDtypeStruct((bsz, seq, B_WIDTH), BF16),
        scratch_shapes=[pltpu.VMEM((nb, B_HEADS, B_VAL_DIM, B_KEY_DIM), F32)],
        compiler_params=pltpu.CompilerParams(dimension_semantics=("parallel", "arbitrary"),
                                             vmem_limit_bytes=VMEM_LIMIT),
        name="hgrn",
    )(qs.reshape(bsz, seq, B_FORGET), logf.reshape(bsz, seq, B_FORGET), v.reshape(bsz, seq, B_WIDTH))


def _out_kernel(oa_ref, ga_ref, ob_ref, gb_ref, x_ref, wo_ref, hg_ref, lng_ref, lnb_ref, out_ref):
    oa = oa_ref[...].astype(F32) * ga_ref[...].astype(F32)
    ob = ob_ref[...].astype(F32)
    parts = []
    for h in range(B_HEADS):
        oh = ob[:, h * B_VAL_DIM:(h + 1) * B_VAL_DIM]
        ms = jnp.mean(oh * oh, axis=-1, keepdims=True)
        parts.append(oh * lax.rsqrt(ms + RMS_EPS) * hg_ref[:, h * B_VAL_DIM:(h + 1) * B_VAL_DIM])
    obn = jnp.concatenate(parts, axis=1) * gb_ref[...].astype(F32)
    z = jnp.concatenate([oa, obn], axis=1).astype(BF16)
    y = jnp.dot(z, wo_ref[...], preferred_element_type=F32)
    r = DEEPNORM_ALPHA * x_ref[...] + y
    mu = jnp.mean(r, axis=-1, keepdims=True)
    rc = r - mu
    var = jnp.mean(rc * rc, axis=-1, keepdims=True)
    out_ref[...] = rc * lax.rsqrt(var + LN_EPS) * lng_ref[...] + lnb_ref[...]


def _output(oa, ga, ob, gb, x2, w_o, hg, ln_g, ln_b):
    t = x2.shape[0]
    tq = min(OUT_TILE, t)
    row = lambda n: pl.BlockSpec((tq, n), lambda i: (i, 0))
    full = lambda shape: pl.BlockSpec(shape, lambda i: (0,) * len(shape))
    d_mix = A_WIDTH + B_WIDTH
    return pl.pallas_call(
        _out_kernel,
        grid=(t // tq,),
        in_specs=[row(A_WIDTH), row(A_WIDTH), row(B_WIDTH), row(B_WIDTH), row(D_MODEL),
                  full((d_mix, D_MODEL)), full((1, B_WIDTH)), full((1, D_MODEL)), full((1, D_MODEL))],
        out_specs=row(D_MODEL),
        out_shape=jax.ShapeDtypeStruct((t, D_MODEL), F32),
        compiler_params=pltpu.CompilerParams(dimension_semantics=("parallel",), vmem_limit_bytes=VMEM_LIMIT),
        name="out",
    )(oa, ga, ob, gb, x2, w_o.astype(BF16), hg.reshape(1, B_WIDTH).astype(F32),
      ln_g.reshape(1, D_MODEL).astype(F32), ln_b.reshape(1, D_MODEL).astype(F32))


def kernel(x, w_in, w_uk, w_uv, kv_norm_g, rel_bias, lb_logits, hgrn_norm_g, w_o, ln_g, ln_b):
    bsz, seq, d = x.shape
    assert d == D_MODEL and seq % Q_BLOCK == 0 and DEPTH == 1
    t = bsz * seq
    x2 = x.reshape(t, d)
    bias = _bias_tiles(rel_bias)
    ik, c, ga, gb, qs, logf, v, iqt, iwt, qlt, ctb = _project(
        x2, w_in[0], w_uk[0], kv_norm_g[0], lb_logits, bsz, seq)
    oa = _dsa(iqt, iwt, ik, c, ctb, qlt, bias, w_uv[0], bsz, seq)
    ob = _hgrn(qs, logf, v, bsz, seq)
    out = _output(oa.reshape(t, A_WIDTH), ga, ob.reshape(t, B_WIDTH), gb, x2, w_o[0],
                  hgrn_norm_g[0], ln_g[0], ln_b[0])
    return out.reshape(bsz, seq, d)
```

```python
import functools
import math

import jax
import jax.numpy as jnp
from jax import lax
from jax.experimental import pallas as pl
from jax.experimental.pallas import tpu as pltpu

F32 = jnp.float32
BF16 = jnp.bfloat16
I32 = jnp.int32
I16 = jnp.int16

D_MODEL = 1024
DEPTH = 1
CHUNK = 64
Q_BLOCK = 128
A_HEADS = 4
A_HEAD_DIM = 128
A_WIDTH = A_HEADS * A_HEAD_DIM
A_KV_RANK = 128
IDX_HEADS = 8
IDX_DIM = 64
TOPK_MAX = 256
B_HEADS = 4
B_KEY_DIM = 128
B_VAL_DIM = 128
B_WIDTH = B_HEADS * B_VAL_DIM
B_FORGET = B_HEADS * B_KEY_DIM
REL_BUCKETS = 32
REL_MAX_DIST = 256
DEEPNORM_ALPHA = (2.0 * DEPTH) ** 0.25
LN_EPS = 1e-5
RMS_EPS = 1e-6

_SPLITS = (
    ('a_q', A_WIDTH), ('a_ckv', A_KV_RANK), ('a_iq', IDX_HEADS * IDX_DIM), ('a_ik', IDX_DIM),
    ('a_iw', IDX_HEADS), ('a_gate', A_WIDTH), ('b_q', B_FORGET), ('b_f', B_FORGET),
    ('b_i', B_WIDTH), ('b_gate', B_WIDTH),
)
_OFFSETS = {}
_off = 0
for _name, _n in _SPLITS:
    _OFFSETS[_name] = (_off, _n)
    _off += _n

LANES = 128
QW = 256
KG = 256
DSA_NB = 4
I16_ROWS = 16
CT_ROWS = A_KV_RANK + I16_ROWS
SUB = 16
HGRN_STEPS = 512
HGRN_NB = 2
EXP_CAP = 80.0
NEG_BIG = -1e30
LOG2_E = math.log2(math.e)
INT_MIN = -2 ** 31
PROJ_TILE = 512
OUT_TILE = 1024
VMEM_LIMIT = 56 * 1024 * 1024


def _col(w, name):
    o, n = _OFFSETS[name]
    return w[:, o:o + n]


def _t5_bucket(rel):
    nb = REL_BUCKETS // 2
    max_exact = nb // 2
    ret = jnp.where(rel > 0, nb, 0).astype(I32)
    n = jnp.abs(rel)
    nf = jnp.maximum(n, max_exact).astype(F32)
    large = max_exact + jnp.floor(jnp.log(nf / max_exact) / math.log(REL_MAX_DIST / max_exact)
                                  * (nb - max_exact)).astype(I32)
    large = jnp.minimum(large, nb - 1)
    return ret + jnp.where(n < max_exact, n, large)


def _bias_kernel(rb_ref, out_ref):
    sl = lax.broadcasted_iota(I32, (KG, QW), 0)
    ql = lax.broadcasted_iota(I32, (KG, QW), 1)
    far = REL_BUCKETS // 2 - 1
    for d in range(2):
        bucket = _t5_bucket(sl - ql - KG * d)
        for h in range(A_HEADS):
            acc = jnp.zeros((KG, QW), F32)
            for bk in range(REL_BUCKETS):
                acc = jnp.where(bucket == bk, (rb_ref[bk, h] - rb_ref[far, h]) * LOG2_E, acc)
            out_ref[d, :, h * QW:(h + 1) * QW] = acc


def _bias_tiles(rel_bias):
    assert KG == QW and KG >= REL_MAX_DIST
    return pl.pallas_call(
        _bias_kernel,
        out_shape=jax.ShapeDtypeStruct((2, KG, A_HEADS * QW), F32),
        in_specs=[pl.BlockSpec(memory_space=pltpu.SMEM)],
        out_specs=pl.BlockSpec(memory_space=pltpu.VMEM),
        name="bias",
    )(rel_bias.astype(F32))


def _proj_kernel(x_ref, wrow_ref, wt_ref, wukt_ref, g_ref, lbl_ref,
                 ik_ref, c_ref, ga_ref, gb_ref, qs_ref, logf_ref, v_ref,
                 iqt_ref, iwt_ref, qlt_ref, ct_ref):
    tq = x_ref.shape[0]
    xb = x_ref[...].astype(BF16)

    def rowdot(lo, n):
        return jnp.dot(xb, wrow_ref[:, lo:lo + n], preferred_element_type=F32)

    h0 = rowdot(0, 256)
    a = h0[:, :A_KV_RANK]
    ms = jnp.mean(a * a, axis=-1, keepdims=True)
    c = a * lax.rsqrt(ms + RMS_EPS) * g_ref[...]
    c_ref[...] = c.astype(BF16)
    ct = c.T
    ones = jnp.ones((CT_ROWS - A_KV_RANK, KG), BF16)
    for i in range(tq // KG):
        ct_ref[0, i] = jnp.concatenate([ct[:, i * KG:(i + 1) * KG].astype(BF16), ones], axis=0)
    ik_ref[...] = h0[:, A_KV_RANK:A_KV_RANK + IDX_DIM].astype(BF16)

    ga_ref[...] = jax.nn.silu(rowdot(256, A_WIDTH)).astype(BF16)
    qs_ref[...] = jax.nn.silu(rowdot(768, B_FORGET)).astype(BF16)
    lbl = lbl_ref[...]
    e = jnp.exp(lbl - jnp.max(lbl, axis=0, keepdims=True))
    lb = e[0:1, :] / jnp.sum(e, axis=0, keepdims=True)
    f = lb + (1.0 - lb) * jax.nn.sigmoid(rowdot(1280, B_FORGET))
    logf_ref[...] = jnp.log(f)
    v_ref[...] = rowdot(1792, B_WIDTH).astype(BF16)
    gb_ref[...] = jax.nn.silu(rowdot(2304, B_WIDTH)).astype(BF16)

    ht = lax.dot_general(wt_ref[...], xb, (((1,), (1,)), ((), ())), preferred_element_type=F32)
    iqt_ref[0] = ht[A_WIDTH:A_WIDTH + IDX_HEADS * IDX_DIM].astype(BF16)
    o = A_WIDTH + IDX_HEADS * IDX_DIM
    iwt_ref[0] = ht[o:o + IDX_HEADS] * (IDX_HEADS ** -0.5 * IDX_DIM ** -0.5)
    for h in range(A_HEADS):
        aq = ht[h * A_HEAD_DIM:(h + 1) * A_HEAD_DIM].astype(BF16)
        ql = jnp.dot(wukt_ref[h], aq, preferred_element_type=F32) * (A_HEAD_DIM ** -0.5 * LOG2_E)
        qlt_ref[0, h] = ql.astype(BF16)


def _project(x2, w_in, w_uk, kv_g, lb_logits, bsz, seq):
    t = bsz * seq
    tq = min(PROJ_TILE, seq)
    per_b = seq // tq
    pad64 = jnp.zeros((D_MODEL, 64), F32)
    wrow = jnp.concatenate([_col(w_in, 'a_ckv'), _col(w_in, 'a_ik'), pad64, _col(w_in, 'a_gate'),
                            _col(w_in, 'b_q'), _col(w_in, 'b_f'), _col(w_in, 'b_i'),
                            _col(w_in, 'b_gate')], axis=1).astype(BF16)
    pad8 = jnp.zeros((D_MODEL, 8), F32)
    wt = jnp.concatenate([_col(w_in, 'a_q'), _col(w_in, 'a_iq'), _col(w_in, 'a_iw'), pad8],
                         axis=1).T.astype(BF16)
    wukt = jnp.swapaxes(w_uk, 1, 2).astype(BF16)
    nrow, nt = wrow.shape[1], wt.shape[0]
    row = lambda n: pl.BlockSpec((tq, n), lambda i: (i, 0))
    full = lambda shape: pl.BlockSpec(shape, lambda i: (0,) * len(shape))
    out_shape = (
        jax.ShapeDtypeStruct((t, IDX_DIM), BF16),
        jax.ShapeDtypeStruct((t, A_KV_RANK), BF16),
        jax.ShapeDtypeStruct((t, A_WIDTH), BF16),
        jax.ShapeDtypeStruct((t, B_WIDTH), BF16),
        jax.ShapeDtypeStruct((t, B_FORGET), BF16),
        jax.ShapeDtypeStruct((t, B_FORGET), F32),
        jax.ShapeDtypeStruct((t, B_WIDTH), BF16),
        jax.ShapeDtypeStruct((bsz, IDX_HEADS * IDX_DIM, seq), BF16),
        jax.ShapeDtypeStruct((bsz, IDX_HEADS, seq), F32),
        jax.ShapeDtypeStruct((bsz, A_HEADS, A_KV_RANK, seq), BF16),
        jax.ShapeDtypeStruct((bsz, seq // KG, CT_ROWS, KG), BF16),
    )
    out_specs = (
        row(IDX_DIM), row(A_KV_RANK), row(A_WIDTH), row(B_WIDTH), row(B_FORGET), row(B_FORGET), row(B_WIDTH),
        pl.BlockSpec((1, IDX_HEADS * IDX_DIM, tq), lambda i: (i // per_b, 0, i % per_b)),
        pl.BlockSpec((1, IDX_HEADS, tq), lambda i: (i // per_b, 0, i % per_b)),
        pl.BlockSpec((1, A_HEADS, A_KV_RANK, tq), lambda i: (i // per_b, 0, 0, i % per_b)),
        pl.BlockSpec((1, tq // KG, CT_ROWS, KG), lambda i: (i // per_b, i % per_b, 0, 0)),
    )
    return pl.pallas_call(
        _proj_kernel,
        grid=(t // tq,),
        in_specs=[row(D_MODEL), full((D_MODEL, nrow)), full((nt, D_MODEL)),
                  full((A_HEADS, A_KV_RANK, A_HEAD_DIM)), full((1, A_KV_RANK)), full((DEPTH + 1, B_FORGET))],
        out_specs=out_specs,
        out_shape=out_shape,
        compiler_params=pltpu.CompilerParams(dimension_semantics=("parallel",), vmem_limit_bytes=VMEM_LIMIT),
        name="proj",
    )(x2, wrow, wt, wukt, kv_g.reshape(1, A_KV_RANK).astype(F32), lb_logits.astype(F32))


def _chunk_of(pos):
    return lax.shift_right_logical(pos, int(math.log2(CHUNK)))


def _tree_sum(parts):
    while len(parts) > 1:
        parts = [a + b for a, b in zip(parts[::2], parts[1::2])] + ([parts[-1]] if len(parts) % 2 else [])
    return parts[0]


def _order_key(bits):
    return jnp.where(bits < 0, jnp.int32(INT_MIN) - bits, bits)


def _dsa_kernel(iqt_ref, iwt_ref, ik_ref, c_ref, ct_ref, qlt_ref, bias_ref, wuv_ref, o_ref,
                sc_ref, hi_ref, lo_ref, acc_ref, m_ref, thr_ref, nge_ref, ngt_ref, *, k_sel, nb):
    j = pl.program_id(1)
    ngrp = j + 1
    hq = A_HEADS * QW
    batches = range(nb)
    row_iota = lax.broadcasted_iota(I32, (KG, QW), 0)
    lane_iota = lax.broadcasted_iota(I32, (KG, QW), 1)
    q_chunk = _chunk_of(j * QW + lane_iota)

    def rows_of(g):
        return pl.ds(pl.multiple_of(g * KG, KG), KG)

    def score_group(g, diagonal):
        rows = rows_of(g)
        for bi in batches:
            rhs = jnp.concatenate([iqt_ref[bi, h * IDX_DIM:(h + 1) * IDX_DIM, :] for h in range(IDX_HEADS)],
                                  axis=1)
            d = jnp.dot(ik_ref[bi, rows, :], rhs, preferred_element_type=F32)
            sc = iwt_ref[bi, 0:1, :] * jnp.maximum(d[:, :QW], 0.0)
            for h in range(1, IDX_HEADS):
                sc = sc + iwt_ref[bi, h:h + 1, :] * jnp.maximum(d[:, h * QW:(h + 1) * QW], 0.0)
            if diagonal:
                sc = jnp.where(_chunk_of(g * KG + row_iota) <= q_chunk, sc, -jnp.inf)
            sc_ref[bi, rows, :] = sc
            key = _order_key(lax.bitcast_convert_type(sc, I32))
            hi_ref[bi, rows, :] = (key >> 16).astype(I16)
            lo_ref[bi, rows, :] = (key ^ 0x8000).astype(I16)

    def score_inner(g, carry):
        score_group(g, False)
        return carry

    lax.fori_loop(0, j, score_inner, 0)
    score_group(j, True)

    def count16(ref, cands):
        c16 = [c.astype(I16) for c in cands]

        def one(g, accs):
            rows = rows_of(g)
            out = []
            for bi in batches:
                blk = ref[bi, rows, :]
                ind = jnp.where(blk >= c16[bi], jnp.int16(1), jnp.int16(0))
                out.append(accs[bi] + _tree_sum([ind[r:r + I16_ROWS] for r in range(0, KG, I16_ROWS)]))
            return tuple(out)

        def two(p, accs):
            return one(2 * p + 1, one(2 * p, accs))

        accs = tuple(jnp.zeros((I16_ROWS, QW), I16) for _ in batches)
        accs = lax.fori_loop(0, ngrp // 2, two, accs)
        accs = lax.fori_loop(2 * (ngrp // 2), ngrp, one, accs)
        return [jnp.sum(a.astype(I32), axis=0, keepdims=True) for a in accs]

    def bisect16(ref, want, state, first_bit, last_bit):
        def step(i, carry):
            ts, n_ge, n_gt = carry
            cands = [t + lax.shift_left(jnp.int32(1), 15 - i) for t in ts]
            cnts = count16(ref, cands)
            ok = [cnts[bi] >= want[bi] for bi in batches]
            return (tuple(jnp.where(ok[bi], cands[bi], ts[bi]) for bi in batches),
                    tuple(jnp.where(ok[bi], cnts[bi], n_ge[bi]) for bi in batches),
                    tuple(jnp.where(ok[bi], n_gt[bi], cnts[bi]) for bi in batches))
        return lax.fori_loop(first_bit, last_bit, step, state)

    def start16(n_all):
        return (tuple(jnp.full((1, QW), -32768, I32) for _ in batches), tuple(n_all),
                tuple(jnp.zeros((1, QW), I32) for _ in batches))

    want_hi = [jnp.full((1, QW), k_sel, I32) for _ in batches]
    n_rows = [jnp.zeros((1, QW), I32) + ngrp * KG for _ in batches]
    p_hi, n_ge_hi, n_gt_hi = bisect16(hi_ref, want_hi, start16(n_rows), 0, 16)

    def keep_ties(g, carry):
        rows = rows_of(g)
        for bi in batches:
            lo_ref[bi, rows, :] = jnp.where(hi_ref[bi, rows, :] == p_hi[bi].astype(I16), lo_ref[bi, rows, :],
                                            jnp.int16(-32768))
        return carry

    lax.fori_loop(0, ngrp, keep_ties, 0)
    want_lo = [k_sel - n_gt_hi[bi] for bi in batches]
    n_eq_hi = [n_ge_hi[bi] - n_gt_hi[bi] for bi in batches]
    p_lo, n_ge_lo, n_gt_lo = bisect16(lo_ref, want_lo, start16(n_eq_hi), 0, 16)
    for bi in batches:
        thr_ref[bi] = lax.shift_left(p_hi[bi], 16) + (p_lo[bi] + 32768)
        nge_ref[bi] = n_gt_hi[bi] + n_ge_lo[bi]
        ngt_ref[bi] = n_gt_hi[bi] + n_gt_lo[bi]

    def as_float(key):
        return lax.bitcast_convert_type(_order_key(key), F32)

    def count32(cands):
        def body(g, accs):
            rows = rows_of(g)
            out = []
            for bi in batches:
                ind = jnp.where(sc_ref[bi, rows, :] >= cands[bi], 1, 0)
                out.append(accs[bi] + jnp.sum(ind.reshape(KG // 8, 8, QW), axis=0))
            return tuple(out)

        accs = lax.fori_loop(0, ngrp, body, tuple(jnp.zeros((8, QW), I32) for _ in batches))
        return [jnp.sum(a, axis=0, keepdims=True) for a in accs]

    n_ge_f = count32([as_float(thr_ref[bi]) for bi in batches])
    differs = _tree_sum([jnp.where(n_ge_f[bi] != nge_ref[bi], 1, 0) for bi in batches])

    @pl.when(jnp.max(differs) > 0)
    def _():
        def step(i, carry):
            ts, n_ge, n_gt = carry
            cands = [t + lax.shift_left(jnp.int32(1), 31 - i) for t in ts]
            cnts = count32([as_float(c) for c in cands])
            ok = [cnts[bi] >= k_sel for bi in batches]
            return (tuple(jnp.where(ok[bi], cands[bi], ts[bi]) for bi in batches),
                    tuple(jnp.where(ok[bi], cnts[bi], n_ge[bi]) for bi in batches),
                    tuple(jnp.where(ok[bi], n_gt[bi], cnts[bi]) for bi in batches))
        init = (tuple(jnp.full((1, QW), INT_MIN, I32) for _ in batches), tuple(n_rows),
                tuple(jnp.zeros((1, QW), I32) for _ in batches))
        ts, n_ge, n_gt = lax.fori_loop(0, 32, step, init)
        for bi in batches:
            thr_ref[bi] = ts[bi]
            nge_ref[bi] = n_ge[bi]
            ngt_ref[bi] = n_gt[bi]

    thr = [as_float(thr_ref[bi]) for bi in batches]

    excess = _tree_sum([jnp.where((nge_ref[bi] > k_sel) & (thr[bi] > -jnp.inf), 1, 0) for bi in batches])

    @pl.when(jnp.max(excess) > 0)
    def _():
        tri = jnp.where(row_iota >= lane_iota, 1.0, 0.0).astype(BF16)
        need = [(k_sel - ngt_ref[bi]).astype(F32) for bi in batches]

        def demote(g, carry):
            rows = rows_of(g)
            out = []
            for bi in batches:
                blk = sc_ref[bi, rows, :]
                eq = jnp.where(blk == thr[bi], 1.0, 0.0)
                incl = jnp.dot(tri, eq.astype(BF16), preferred_element_type=F32)
                rank = incl - eq + carry[bi]
                sc_ref[bi, rows, :] = jnp.where((blk == thr[bi]) & (rank >= need[bi]), -jnp.inf, blk)
                out.append(carry[bi] + incl[KG - 1:KG, :])
            return tuple(out)

        lax.fori_loop(0, ngrp, demote, tuple(jnp.zeros((1, QW), F32) for _ in batches))

    thr_sel = [jnp.maximum(t, jnp.finfo(F32).min) for t in thr]
    for bi in batches:
        m_ref[bi] = jnp.full((1, hq), NEG_BIG, F32)
        acc_ref[bi] = jnp.zeros((CT_ROWS, hq), F32)

    def attend(g, with_bias):
        half = KG // 2
        chains = [(bi, h) for bi in batches for h in range(A_HEADS)]
        sel_h, c_h = {}, {}
        for bi in batches:
            for u in range(2):
                rows = pl.ds(pl.multiple_of(g * KG + u * half, half), half)
                sel_h[bi, u] = jnp.where(sc_ref[bi, rows, :] >= thr_sel[bi], 0.0, NEG_BIG)
                c_h[bi, u] = c_ref[bi, rows, :]
        s_all = {}
        for bi, h in chains:
            for u in range(2):
                s_all[bi, h, u] = jnp.dot(c_h[bi, u], qlt_ref[bi, h], preferred_element_type=F32)
        p_all = {}
        alpha_all = {}
        for bi, h in chains:
            cols = slice(h * QW, (h + 1) * QW)
            m_loc, p_loc = [], []
            for u in range(2):
                s = s_all[bi, h, u]
                if with_bias:
                    s = s + bias_ref[j - g, u * half:(u + 1) * half, cols]
                s = s + sel_h[bi, u]
                m_u = jnp.max(s, axis=0, keepdims=True)
                m_loc.append(m_u)
                p_loc.append(jnp.exp2(s - m_u).astype(BF16))
            m_old = m_ref[bi, :, cols]
            m_new = jnp.maximum(m_old, jnp.maximum(m_loc[0], m_loc[1]))
            alpha_all[bi, h] = jnp.exp2(m_old - m_new)
            p_all[bi, h] = jnp.concatenate([p_loc[u] * jnp.exp2(m_loc[u] - m_new).astype(BF16) for u in range(2)],
                                           axis=0)
            m_ref[bi, :, cols] = m_new
        for bi, h in chains:
            cols = slice(h * QW, (h + 1) * QW)
            pv = jnp.dot(ct_ref[bi, g], p_all[bi, h], preferred_element_type=F32)
            acc_ref[bi, :, cols] = acc_ref[bi, :, cols] * alpha_all[bi, h] + pv

    def attend_far(g, carry):
        attend(g, False)
        return carry

    def attend_near(g, carry):
        attend(g, True)
        return carry

    near0 = jnp.maximum(j - 1, 0)
    lax.fori_loop(0, near0, attend_far, 0)
    lax.fori_loop(near0, ngrp, attend_near, 0)

    for bi in batches:
        ot = acc_ref[bi, :A_KV_RANK, :] * (1.0 / acc_ref[bi, A_KV_RANK:A_KV_RANK + 1, :])
        for h in range(A_HEADS):
            o_h = ot[:, h * QW:(h + 1) * QW].T
            oa = jnp.dot(o_h.astype(BF16), wuv_ref[h], preferred_element_type=F32)
            o_ref[bi, :, h * A_HEAD_DIM:(h + 1) * A_HEAD_DIM] = oa.astype(BF16)


def _dsa(iqt, iwt, ik, c, ctb, qlt, bias, w_uv, bsz, seq):
    assert seq % QW == 0 and QW % Q_BLOCK == 0
    nb = DSA_NB if bsz % DSA_NB == 0 else 1
    k_sel = min(TOPK_MAX, seq // 4)
    hq = A_HEADS * QW
    return pl.pallas_call(
        functools.partial(_dsa_kernel, k_sel=k_sel, nb=nb),
        grid=(bsz // nb, seq // QW),
        in_specs=[
            pl.BlockSpec((nb, IDX_HEADS * IDX_DIM, QW), lambda b, j: (b, 0, j)),
            pl.BlockSpec((nb, IDX_HEADS, QW), lambda b, j: (b, 0, j)),
            pl.BlockSpec((nb, seq, IDX_DIM), lambda b, j: (b, 0, 0)),
            pl.BlockSpec((nb, seq, A_KV_RANK), lambda b, j: (b, 0, 0)),
            pl.BlockSpec((nb, seq // KG, CT_ROWS, KG), lambda b, j: (b, 0, 0, 0)),
            pl.BlockSpec((nb, A_HEADS, A_KV_RANK, QW), lambda b, j: (b, 0, 0, j)),
            pl.BlockSpec((2, KG, hq), lambda b, j: (0, 0, 0)),
            pl.BlockSpec((A_HEADS, A_KV_RANK, A_HEAD_DIM), lambda b, j: (0, 0, 0)),
        ],
        out_specs=pl.BlockSpec((nb, QW, A_WIDTH), lambda b, j: (b, j, 0)),
        out_shape=jax.ShapeDtypeStruct((bsz, seq, A_WIDTH), BF16),
        scratch_shapes=[
            pltpu.VMEM((nb, seq, QW), F32),
            pltpu.VMEM((nb, seq, QW), I16),
            pltpu.VMEM((nb, seq, QW), I16),
            pltpu.VMEM((nb, CT_ROWS, hq), F32),
            pltpu.VMEM((nb, 1, hq), F32),
            pltpu.VMEM((nb, 1, QW), I32),
            pltpu.VMEM((nb, 1, QW), I32),
            pltpu.VMEM((nb, 1, QW), I32),
        ],
        compiler_params=pltpu.CompilerParams(dimension_semantics=("parallel", "arbitrary"),
                                             vmem_limit_bytes=VMEM_LIMIT),
        name="dsa",
    )(iqt, iwt, ik.reshape(bsz, seq, IDX_DIM), c.reshape(bsz, seq, A_KV_RANK), ctb, qlt, bias,
      w_uv.astype(BF16))


def _hgrn_kernel(qs_ref, logf_ref, v_ref, o_ref, st_ref, bq_ref, oi_ref, *, nb):
    @pl.when(pl.program_id(1) == 0)
    def _():
        st_ref[...] = jnp.zeros(st_ref.shape, F32)

    steps = qs_ref.shape[1]
    nchunk = steps // CHUNK
    dk, dv = B_KEY_DIM, B_VAL_DIM
    r64 = lax.broadcasted_iota(I32, (CHUNK, CHUNK), 0)
    c64 = lax.broadcasted_iota(I32, (CHUNK, CHUNK), 1)
    tri = jnp.where(r64 >= c64, 1.0, 0.0).astype(BF16)
    rt = lax.broadcasted_iota(I32, (CHUNK, LANES), 0)
    cs = lax.broadcasted_iota(I32, (CHUNK, LANES), 1)
    causal = cs <= rt
    nsub = CHUNK // SUB
    zpad_f = jnp.zeros((LANES - CHUNK, LANES), F32)
    nt = (((1,), (1,)), ((), ()))
    units = [(bi, ci, h) for bi in range(nb) for ci in range(nchunk) for h in range(B_HEADS)]

    lf_all, b_all = {}, {}
    for bi in range(nb):
        for ci in range(nchunk):
            lf = logf_ref[bi, ci * CHUNK:(ci + 1) * CHUNK, :]
            hi = lf.astype(BF16)
            lo = (lf - hi.astype(F32)).astype(BF16)
            lf_all[bi, ci] = lf
            b_all[bi, ci] = (jnp.dot(tri, hi, preferred_element_type=F32)
                             + jnp.dot(tri, lo, preferred_element_type=F32))

    span = jnp.zeros((1, B_FORGET), F32)
    for b in b_all.values():
        ends = [jnp.zeros((1, B_FORGET), F32)] + [b[i * SUB - 1:i * SUB, :] for i in range(1, nsub + 1)]
        for i in range(nsub):
            span = jnp.maximum(span, ends[i] - ends[i + 1])
    safe = jnp.max(span) <= EXP_CAP
    chains = [(bi, h) for bi in range(nb) for h in range(B_HEADS)]

    def state_operands(u):
        bi, ci, h = u
        rows = slice(ci * CHUNK, (ci + 1) * CHUNK)
        cols = slice(h * dk, (h + 1) * dk)
        b = b_all[bi, ci][:, cols]
        kk = 1.0 - jnp.exp(lf_all[bi, ci][:, cols])
        b_last = b[CHUNK - 1:CHUNK, :]
        k_hat = (kk * jnp.exp(b_last - b)).astype(BF16)
        k_pad = jnp.concatenate([k_hat, jnp.zeros((LANES - CHUNK, dk), BF16)], axis=0)
        v = v_ref[bi, rows, h * dv:(h + 1) * dv]
        v_t = jnp.concatenate([v.astype(F32), zpad_f], axis=0).T.astype(BF16)
        return b, kk, k_pad, v_t, jnp.exp(b_last)

    @pl.when(safe)
    def _():
        q_in, q_st, kst, k_pad, v_t, e_last = {}, {}, {}, {}, {}, {}
        r_all, lhs, upd = {}, {}, {}

        def operands(u):
            bi, ci, h = u
            b, kk, k_pad[u], v_t[u], e_last[u] = state_operands(u)
            q = qs_ref[bi, ci * CHUNK:(ci + 1) * CHUNK, h * dk:(h + 1) * dk].astype(F32)
            starts = [jnp.zeros((1, dk), F32)] + [b[i * SUB - 1:i * SUB, :] for i in range(1, nsub)]
            ref_rows = jnp.concatenate([jnp.broadcast_to(s_, (SUB, dk)) for s_ in starts], axis=0)
            qi = q * jnp.exp(b - ref_rows)
            q_in[u] = qi.astype(BF16)
            q_st[u] = (qi * jnp.exp(ref_rows)).astype(BF16)
            secs = []
            for i in range(nsub):
                n = SUB * (i + 1)
                ks = (kk[:n, :] * jnp.exp(starts[i] - b[:n, :])).astype(BF16)
                secs.append(ks if n == CHUNK else jnp.concatenate([ks, jnp.zeros((CHUNK - n, dk), BF16)], axis=0))
            kst[u] = jnp.concatenate(secs, axis=0)

        def matmuls(u):
            r_all[u] = lax.dot_general(q_in[u], kst[u], nt, preferred_element_type=F32)
            upd[u] = jnp.dot(v_t[u], k_pad[u], preferred_element_type=F32)

        def scores(u):
            r = r_all[u]
            slabs = []
            for i in range(nsub):
                lane0 = (i * CHUNK // LANES) * LANES
                slab = r[i * SUB:(i + 1) * SUB, lane0:lane0 + LANES]
                if (i * CHUNK) % LANES:
                    slab = pltpu.roll(slab, LANES - (i * CHUNK) % LANES, 1)
                slabs.append(slab)
            a = jnp.concatenate(slabs, axis=0)
            a = jnp.where(causal, a, 0.0).astype(BF16)
            lhs[u] = jnp.concatenate([a, q_st[u]], axis=1)

        for u in units:
            operands(u)
        for u in units:
            matmuls(u)
            scores(u)

        st = {ch: st_ref[ch[0], ch[1]] for ch in chains}
        for ci in range(nchunk):
            for bi, h in chains:
                u = (bi, ci, h)
                w = jnp.concatenate([v_t[u], st[bi, h].astype(BF16)], axis=1)
                o = lax.dot_general(lhs[u], w, nt, preferred_element_type=F32)
                o_ref[bi, ci * CHUNK:(ci + 1) * CHUNK, h * dv:(h + 1) * dv] = o.astype(BF16)
            for bi, h in chains:
                st[bi, h] = st[bi, h] * e_last[bi, ci, h] + upd[bi, ci, h]
        for bi, h in chains:
            st_ref[bi, h] = st[bi, h]

    @pl.when(jnp.logical_not(safe))
    def _():
        for bi in range(nb):
            for ci in range(nchunk):
                rows = slice(ci * CHUNK, (ci + 1) * CHUNK)
                bq_ref[0, bi, rows, :] = b_all[bi, ci]
                bq_ref[1, bi, rows, :] = qs_ref[bi, rows, :].astype(F32)
        row_id = lax.broadcasted_iota(I32, (CHUNK, 1), 0)
        for bi in range(nb):
            for ci in range(nchunk):
                rows = slice(ci * CHUNK, (ci + 1) * CHUNK)
                b = b_all[bi, ci]
                kk = 1.0 - jnp.exp(lf_all[bi, ci])
                v = v_ref[bi, rows, :].astype(F32)

                def row(t, carry, bi=bi, ci=ci, b=b, kk=kk, v=v):
                    b_t = bq_ref[0, bi, pl.ds(ci * CHUNK + t, 1), :]
                    q_t = bq_ref[1, bi, pl.ds(ci * CHUNK + t, 1), :]
                    w = q_t * kk * jnp.exp(jnp.minimum(b_t - b, 0.0))
                    outs = []
                    for h in range(B_HEADS):
                        s_h = jnp.sum(w[:, h * dk:(h + 1) * dk], axis=1, keepdims=True)
                        s_h = jnp.where(row_id <= t, s_h, 0.0)
                        outs.append(jnp.sum(s_h * v[:, h * dv:(h + 1) * dv], axis=0, keepdims=True))
                    oi_ref[bi, pl.ds(ci * CHUNK + t, 1), :] = jnp.concatenate(outs, axis=1)
                    return carry

                lax.fori_loop(0, CHUNK, row, 0)
        st = {ch: st_ref[ch[0], ch[1]] for ch in chains}
        for ci in range(nchunk):
            rows = slice(ci * CHUNK, (ci + 1) * CHUNK)
            for bi, h in chains:
                b, kk, k_pad, v_t, e_last = state_operands((bi, ci, h))
                q_st = (qs_ref[bi, rows, h * dk:(h + 1) * dk].astype(F32) * jnp.exp(b)).astype(BF16)
                o = oi_ref[bi, rows, h * dv:(h + 1) * dv] + lax.dot_general(
                    q_st, st[bi, h].astype(BF16), nt, preferred_element_type=F32)
                o_ref[bi, rows, h * dv:(h + 1) * dv] = o.astype(BF16)
                st[bi, h] = st[bi, h] * e_last + jnp.dot(v_t, k_pad, preferred_element_type=F32)
        for bi, h in chains:
            st_ref[bi, h] = st[bi, h]


def _hgrn(qs, logf, v, bsz, seq):
    steps = min(HGRN_STEPS, seq)
    nb = HGRN_NB if bsz % HGRN_NB == 0 else 1
    spec = lambda n: pl.BlockSpec((nb, steps, n), lambda b, s: (b, s, 0))
    return pl.pallas_call(
        functools.partial(_hgrn_kernel, nb=nb),
        grid=(bsz // nb, seq // steps),
        in_specs=[spec(B_FORGET), spec(B_FORGET), spec(B_WIDTH)],
        out_specs=spec(B_WIDTH),
        out_shape=jax.ShapeDtypeStruct((bsz, seq, B_WIDTH), BF16),
        scratch_shapes=[pltpu.VMEM((nb, B_HEADS, B_VAL_DIM, B_KEY_DIM), F32),
                        pltpu.VMEM((2, nb, steps, B_FORGET), F32),
                        pltpu.VMEM((nb, steps, B_WIDTH), F32)],
        compiler_params=pltpu.CompilerParams(dimension_semantics=("parallel", "arbitrary"),
                                             vmem_limit_bytes=VMEM_LIMIT),
        name="hgrn",
    )(qs.reshape(bsz, seq, B_FORGET), logf.reshape(bsz, seq, B_FORGET), v.reshape(bsz, seq, B_WIDTH))


def _out_kernel(oa_ref, ga_ref, ob_ref, gb_ref, x_ref, wo_ref, hg_ref, lng_ref, lnb_ref, out_ref):
    oa = oa_ref[...].astype(F32) * ga_ref[...].astype(F32)
    ob = ob_ref[...].astype(F32)
    parts = []
    for h in range(B_HEADS):
        oh = ob[:, h * B_VAL_DIM:(h + 1) * B_VAL_DIM]
        ms = jnp.mean(oh * oh, axis=-1, keepdims=True)
        parts.append(oh * lax.rsqrt(ms + RMS_EPS) * hg_ref[:, h * B_VAL_DIM:(h + 1) * B_VAL_DIM])
    obn = jnp.concatenate(parts, axis=1) * gb_ref[...].astype(F32)
    z = jnp.concatenate([oa, obn], axis=1).astype(BF16)
    y = jnp.dot(z, wo_ref[...], preferred_element_type=F32)
    r = DEEPNORM_ALPHA * x_ref[...] + y
    mu = jnp.mean(r, axis=-1, keepdims=True)
    rc = r - mu
    var = jnp.mean(rc * rc, axis=-1, keepdims=True)
    out_ref[...] = rc * lax.rsqrt(var + LN_EPS) * lng_ref[...] + lnb_ref[...]


def _output(oa, ga, ob, gb, x2, w_o, hg, ln_g, ln_b):
    t = x2.shape[0]
    tq = min(OUT_TILE, t)
    row = lambda n: pl.BlockSpec((tq, n), lambda i: (i, 0))
    full = lambda shape: pl.BlockSpec(shape, lambda i: (0,) * len(shape))
    d_mix = A_WIDTH + B_WIDTH
    return pl.pallas_call(
        _out_kernel,
        grid=(t // tq,),
        in_specs=[row(A_WIDTH), row(A_WIDTH), row(B_WIDTH), row(B_WIDTH), row(D_MODEL),
                  full((d_mix, D_MODEL)), full((1, B_WIDTH)), full((1, D_MODEL)), full((1, D_MODEL))],
        out_specs=row(D_MODEL),
        out_shape=jax.ShapeDtypeStruct((t, D_MODEL), F32),
        compiler_params=pltpu.CompilerParams(dimension_semantics=("parallel",), vmem_limit_bytes=VMEM_LIMIT),
        name="out",
    )(oa, ga, ob, gb, x2, w_o.astype(BF16), hg.reshape(1, B_WIDTH).astype(F32),
      ln_g.reshape(1, D_MODEL).astype(F32), ln_b.reshape(1, D_MODEL).astype(F32))


def kernel(x, w_in, w_uk, w_uv, kv_norm_g, rel_bias, lb_logits, hgrn_norm_g, w_o, ln_g, ln_b):
    bsz, seq, d = x.shape
    assert d == D_MODEL and seq % Q_BLOCK == 0 and DEPTH == 1
    t = bsz * seq
    x2 = x.reshape(t, d)
    bias = _bias_tiles(rel_bias)
    ik, c, ga, gb, qs, logf, v, iqt, iwt, qlt, ctb = _project(
        x2, w_in[0], w_uk[0], kv_norm_g[0], lb_logits, bsz, seq)
    oa = _dsa(iqt, iwt, ik, c, ctb, qlt, bias, w_uv[0], bsz, seq)
    ob = _hgrn(qs, logf, v, bsz, seq)
    out = _output(oa.reshape(t, A_WIDTH), ga, ob.reshape(t, B_WIDTH), gb, x2, w_o[0],
                  hgrn_norm_g[0], ln_g[0], ln_b[0])
    return out.reshape(bsz, seq, d)
```

```python
import functools
import math

import jax
import jax.numpy as jnp
from jax import lax
from jax.experimental import pallas as pl
from jax.experimental.pallas import tpu as pltpu

F32 = jnp.float32
BF16 = jnp.bfloat16
I32 = jnp.int32
I16 = jnp.int16

D_MODEL = 1024
DEPTH = 1
CHUNK = 64
Q_BLOCK = 128
A_HEADS = 4
A_HEAD_DIM = 128
A_WIDTH = A_HEADS * A_HEAD_DIM
A_KV_RANK = 128
IDX_HEADS = 8
IDX_DIM = 64
TOPK_MAX = 256
B_HEADS = 4
B_KEY_DIM = 128
B_VAL_DIM = 128
B_WIDTH = B_HEADS * B_VAL_DIM
B_FORGET = B_HEADS * B_KEY_DIM
REL_BUCKETS = 32
REL_MAX_DIST = 256
DEEPNORM_ALPHA = (2.0 * DEPTH) ** 0.25
LN_EPS = 1e-5
RMS_EPS = 1e-6

_SPLITS = (
    ('a_q', A_WIDTH), ('a_ckv', A_KV_RANK), ('a_iq', IDX_HEADS * IDX_DIM), ('a_ik', IDX_DIM),
    ('a_iw', IDX_HEADS), ('a_gate', A_WIDTH), ('b_q', B_FORGET), ('b_f', B_FORGET),
    ('b_i', B_WIDTH), ('b_gate', B_WIDTH),
)
_OFFSETS = {}
_off = 0
for _name, _n in _SPLITS:
    _OFFSETS[_name] = (_off, _n)
    _off += _n

LANES = 128
QW = 256
KG = 256
DSA_NB = 4
I16_ROWS = 16
CT_ROWS = A_KV_RANK + I16_ROWS
SUB = 16
HGRN_STEPS = 512
HGRN_NB = 2
EXP_CAP = 80.0
NEG_BIG = -1e30
LOG2_E = math.log2(math.e)
INT_MIN = -2 ** 31
PROJ_TILE = 512
OUT_TILE = 1024
VMEM_LIMIT = 56 * 1024 * 1024


def _col(w, name):
    o, n = _OFFSETS[name]
    return w[:, o:o + n]


def _t5_bucket(rel):
    nb = REL_BUCKETS // 2
    max_exact = nb // 2
    ret = jnp.where(rel > 0, nb, 0).astype(I32)
    n = jnp.abs(rel)
    nf = jnp.maximum(n, max_exact).astype(F32)
    large = max_exact + jnp.floor(jnp.log(nf / max_exact) / math.log(REL_MAX_DIST / max_exact)
                                  * (nb - max_exact)).astype(I32)
    large = jnp.minimum(large, nb - 1)
    return ret + jnp.where(n < max_exact, n, large)


def _bias_kernel(rb_ref, out_ref):
    sl = lax.broadcasted_iota(I32, (KG, QW), 0)
    ql = lax.broadcasted_iota(I32, (KG, QW), 1)
    far = REL_BUCKETS // 2 - 1
    for d in range(2):
        bucket = _t5_bucket(sl - ql - KG * d)
        for h in range(A_HEADS):
            acc = jnp.zeros((KG, QW), F32)
            for bk in range(REL_BUCKETS):
                acc = jnp.where(bucket == bk, (rb_ref[bk, h] - rb_ref[far, h]) * LOG2_E, acc)
            out_ref[d, :, h * QW:(h + 1) * QW] = acc


def _bias_tiles(rel_bias):
    assert KG == QW and KG >= REL_MAX_DIST
    return pl.pallas_call(
        _bias_kernel,
        out_shape=jax.ShapeDtypeStruct((2, KG, A_HEADS * QW), F32),
        in_specs=[pl.BlockSpec(memory_space=pltpu.SMEM)],
        out_specs=pl.BlockSpec(memory_space=pltpu.VMEM),
        name="bias",
    )(rel_bias.astype(F32))


def _proj_kernel(x_ref, wrow_ref, wt_ref, wukt_ref, g_ref, lbl_ref,
                 ik_ref, c_ref, ga_ref, gb_ref, qs_ref, logf_ref, v_ref,
                 iqt_ref, iwt_ref, qlt_ref, ct_ref):
    tq = x_ref.shape[0]
    xb = x_ref[...].astype(BF16)

    def rowdot(lo, n):
        return jnp.dot(xb, wrow_ref[:, lo:lo + n], preferred_element_type=F32)

    h0 = rowdot(0, 256)
    a = h0[:, :A_KV_RANK]
    ms = jnp.mean(a * a, axis=-1, keepdims=True)
    c = a * lax.rsqrt(ms + RMS_EPS) * g_ref[...]
    c_ref[...] = c.astype(BF16)
    ct = c.T
    ones = jnp.ones((CT_ROWS - A_KV_RANK, KG), BF16)
    for i in range(tq // KG):
        ct_ref[0, i] = jnp.concatenate([ct[:, i * KG:(i + 1) * KG].astype(BF16), ones], axis=0)
    ik_ref[...] = h0[:, A_KV_RANK:A_KV_RANK + IDX_DIM].astype(BF16)

    ga_ref[...] = jax.nn.silu(rowdot(256, A_WIDTH)).astype(BF16)
    qs_ref[...] = jax.nn.silu(rowdot(768, B_FORGET)).astype(BF16)
    lbl = lbl_ref[...]
    e = jnp.exp(lbl - jnp.max(lbl, axis=0, keepdims=True))
    lb = e[0:1, :] / jnp.sum(e, axis=0, keepdims=True)
    f = lb + (1.0 - lb) * jax.nn.sigmoid(rowdot(1280, B_FORGET))
    logf_ref[...] = jnp.log(f)
    v_ref[...] = rowdot(1792, B_WIDTH).astype(BF16)
    gb_ref[...] = jax.nn.silu(rowdot(2304, B_WIDTH)).astype(BF16)

    ht = lax.dot_general(wt_ref[...], xb, (((1,), (1,)), ((), ())), preferred_element_type=F32)
    iqt_ref[0] = ht[A_WIDTH:A_WIDTH + IDX_HEADS * IDX_DIM].astype(BF16)
    o = A_WIDTH + IDX_HEADS * IDX_DIM
    iwt_ref[0] = ht[o:o + IDX_HEADS] * (IDX_HEADS ** -0.5 * IDX_DIM ** -0.5)
    for h in range(A_HEADS):
        aq = ht[h * A_HEAD_DIM:(h + 1) * A_HEAD_DIM].astype(BF16)
        ql = jnp.dot(wukt_ref[h], aq, preferred_element_type=F32) * (A_HEAD_DIM ** -0.5 * LOG2_E)
        qlt_ref[0, h] = ql.astype(BF16)


def _project(x2, w_in, w_uk, kv_g, lb_logits, bsz, seq):
    t = bsz * seq
    tq = min(PROJ_TILE, seq)
    per_b = seq // tq
    pad64 = jnp.zeros((D_MODEL, 64), F32)
    wrow = jnp.concatenate([_col(w_in, 'a_ckv'), _col(w_in, 'a_ik'), pad64, _col(w_in, 'a_gate'),
                            _col(w_in, 'b_q'), _col(w_in, 'b_f'), _col(w_in, 'b_i'),
                            _col(w_in, 'b_gate')], axis=1).astype(BF16)
    pad8 = jnp.zeros((D_MODEL, 8), F32)
    wt = jnp.concatenate([_col(w_in, 'a_q'), _col(w_in, 'a_iq'), _col(w_in, 'a_iw'), pad8],
                         axis=1).T.astype(BF16)
    wukt = jnp.swapaxes(w_uk, 1, 2).astype(BF16)
    nrow, nt = wrow.shape[1], wt.shape[0]
    row = lambda n: pl.BlockSpec((tq, n), lambda i: (i, 0))
    full = lambda shape: pl.BlockSpec(shape, lambda i: (0,) * len(shape))
    out_shape = (
        jax.ShapeDtypeStruct((t, IDX_DIM), BF16),
        jax.ShapeDtypeStruct((t, A_KV_RANK), BF16),
        jax.ShapeDtypeStruct((t, A_WIDTH), BF16),
        jax.ShapeDtypeStruct((t, B_WIDTH), BF16),
        jax.ShapeDtypeStruct((t, B_FORGET), BF16),
        jax.ShapeDtypeStruct((t, B_FORGET), F32),
        jax.ShapeDtypeStruct((t, B_WIDTH), BF16),
        jax.ShapeDtypeStruct((bsz, IDX_HEADS * IDX_DIM, seq), BF16),
        jax.ShapeDtypeStruct((bsz, IDX_HEADS, seq), F32),
        jax.ShapeDtypeStruct((bsz, A_HEADS, A_KV_RANK, seq), BF16),
        jax.ShapeDtypeStruct((bsz, seq // KG, CT_ROWS, KG), BF16),
    )
    out_specs = (
        row(IDX_DIM), row(A_KV_RANK), row(A_WIDTH), row(B_WIDTH), row(B_FORGET), row(B_FORGET), row(B_WIDTH),
        pl.BlockSpec((1, IDX_HEADS * IDX_DIM, tq), lambda i: (i // per_b, 0, i % per_b)),
        pl.BlockSpec((1, IDX_HEADS, tq), lambda i: (i // per_b, 0, i % per_b)),
        pl.BlockSpec((1, A_HEADS, A_KV_RANK, tq), lambda i: (i // per_b, 0, 0, i % per_b)),
        pl.BlockSpec((1, tq // KG, CT_ROWS, KG), lambda i: (i // per_b, i % per_b, 0, 0)),
    )
    return pl.pallas_call(
        _proj_kernel,
        grid=(t // tq,),
        in_specs=[row(D_MODEL), full((D_MODEL, nrow)), full((nt, D_MODEL)),
                  full((A_HEADS, A_KV_RANK, A_HEAD_DIM)), full((1, A_KV_RANK)), full((DEPTH + 1, B_FORGET))],
        out_specs=out_specs,
        out_shape=out_shape,
        compiler_params=pltpu.CompilerParams(dimension_semantics=("parallel",), vmem_limit_bytes=VMEM_LIMIT),
        name="proj",
    )(x2, wrow, wt, wukt, kv_g.reshape(1, A_KV_RANK).astype(F32), lb_logits.astype(F32))


def _chunk_of(pos):
    return lax.shift_right_logical(pos, int(math.log2(CHUNK)))


def _tree_sum(parts):
    while len(parts) > 1:
        parts = [a + b for a, b in zip(parts[::2], parts[1::2])] + ([parts[-1]] if len(parts) % 2 else [])
    return parts[0]


def _order_key(bits):
    return jnp.where(bits < 0, jnp.int32(INT_MIN) - bits, bits)


def _dsa_kernel(iqt_ref, iwt_ref, ik_ref, c_ref, ct_ref, qlt_ref, bias_ref, wuv_ref, ga_ref, o_ref,
                sc_ref, hi_ref, lo_ref, acc_ref, m_ref, thr_ref, nge_ref, ngt_ref, *, k_sel, nb):
    j = pl.program_id(1)
    ngrp = j + 1
    hq = A_HEADS * QW
    batches = range(nb)
    row_iota = lax.broadcasted_iota(I32, (KG, QW), 0)
    lane_iota = lax.broadcasted_iota(I32, (KG, QW), 1)
    q_chunk = _chunk_of(j * QW + lane_iota)

    def rows_of(g):
        return pl.ds(pl.multiple_of(g * KG, KG), KG)

    def score_group(g, diagonal):
        rows = rows_of(g)
        for bi in batches:
            rhs = jnp.concatenate([iqt_ref[bi, h * IDX_DIM:(h + 1) * IDX_DIM, :] for h in range(IDX_HEADS)],
                                  axis=1)
            d = jnp.dot(ik_ref[bi, rows, :], rhs, preferred_element_type=F32)
            sc = iwt_ref[bi, 0:1, :] * jnp.maximum(d[:, :QW], 0.0)
            for h in range(1, IDX_HEADS):
                sc = sc + iwt_ref[bi, h:h + 1, :] * jnp.maximum(d[:, h * QW:(h + 1) * QW], 0.0)
            if diagonal:
                sc = jnp.where(_chunk_of(g * KG + row_iota) <= q_chunk, sc, -jnp.inf)
            sc_ref[bi, rows, :] = sc
            key = _order_key(lax.bitcast_convert_type(sc, I32))
            hi_ref[bi, rows, :] = (key >> 16).astype(I16)
            lo_ref[bi, rows, :] = (key ^ 0x8000).astype(I16)

    def score_inner(g, carry):
        score_group(g, False)
        return carry

    lax.fori_loop(0, j, score_inner, 0)
    score_group(j, True)

    def count16(ref, cands):
        c16 = [c.astype(I16) for c in cands]

        def one(g, accs):
            rows = rows_of(g)
            out = []
            for bi in batches:
                blk = ref[bi, rows, :]
                ind = jnp.where(blk >= c16[bi], jnp.int16(1), jnp.int16(0))
                out.append(accs[bi] + _tree_sum([ind[r:r + I16_ROWS] for r in range(0, KG, I16_ROWS)]))
            return tuple(out)

        def two(p, accs):
            return one(2 * p + 1, one(2 * p, accs))

        accs = tuple(jnp.zeros((I16_ROWS, QW), I16) for _ in batches)
        accs = lax.fori_loop(0, ngrp // 2, two, accs)
        accs = lax.fori_loop(2 * (ngrp // 2), ngrp, one, accs)
        return [jnp.sum(a.astype(I32), axis=0, keepdims=True) for a in accs]

    def bisect16(ref, want, state, first_bit, last_bit):
        def step(i, carry):
            ts, n_ge, n_gt = carry
            cands = [t + lax.shift_left(jnp.int32(1), 15 - i) for t in ts]
            cnts = count16(ref, cands)
            ok = [cnts[bi] >= want[bi] for bi in batches]
            return (tuple(jnp.where(ok[bi], cands[bi], ts[bi]) for bi in batches),
                    tuple(jnp.where(ok[bi], cnts[bi], n_ge[bi]) for bi in batches),
                    tuple(jnp.where(ok[bi], n_gt[bi], cnts[bi]) for bi in batches))
        return lax.fori_loop(first_bit, last_bit, step, state)

    def start16(n_all):
        return (tuple(jnp.full((1, QW), -32768, I32) for _ in batches), tuple(n_all),
                tuple(jnp.zeros((1, QW), I32) for _ in batches))

    want_hi = [jnp.full((1, QW), k_sel, I32) for _ in batches]
    n_rows = [jnp.zeros((1, QW), I32) + ngrp * KG for _ in batches]
    p_hi, n_ge_hi, n_gt_hi = bisect16(hi_ref, want_hi, start16(n_rows), 0, 16)

    def keep_ties(g, carry):
        rows = rows_of(g)
        for bi in batches:
            lo_ref[bi, rows, :] = jnp.where(hi_ref[bi, rows, :] == p_hi[bi].astype(I16), lo_ref[bi, rows, :],
                                            jnp.int16(-32768))
        return carry

    lax.fori_loop(0, ngrp, keep_ties, 0)
    want_lo = [k_sel - n_gt_hi[bi] for bi in batches]
    n_eq_hi = [n_ge_hi[bi] - n_gt_hi[bi] for bi in batches]
    p_lo, n_ge_lo, n_gt_lo = bisect16(lo_ref, want_lo, start16(n_eq_hi), 0, 16)
    for bi in batches:
        thr_ref[bi] = lax.shift_left(p_hi[bi], 16) + (p_lo[bi] + 32768)
        nge_ref[bi] = n_gt_hi[bi] + n_ge_lo[bi]
        ngt_ref[bi] = n_gt_hi[bi] + n_gt_lo[bi]

    def as_float(key):
        return lax.bitcast_convert_type(_order_key(key), F32)

    def count32(cands):
        def body(g, accs):
            rows = rows_of(g)
            out = []
            for bi in batches:
                ind = jnp.where(sc_ref[bi, rows, :] >= cands[bi], 1, 0)
                out.append(accs[bi] + jnp.sum(ind.reshape(KG // 8, 8, QW), axis=0))
            return tuple(out)

        accs = lax.fori_loop(0, ngrp, body, tuple(jnp.zeros((8, QW), I32) for _ in batches))
        return [jnp.sum(a, axis=0, keepdims=True) for a in accs]

    n_ge_f = count32([as_float(thr_ref[bi]) for bi in batches])
    differs = _tree_sum([jnp.where(n_ge_f[bi] != nge_ref[bi], 1, 0) for bi in batches])

    @pl.when(jnp.max(differs) > 0)
    def _():
        def step(i, carry):
            ts, n_ge, n_gt = carry
            cands = [t + lax.shift_left(jnp.int32(1), 31 - i) for t in ts]
            cnts = count32([as_float(c) for c in cands])
            ok = [cnts[bi] >= k_sel for bi in batches]
            return (tuple(jnp.where(ok[bi], cands[bi], ts[bi]) for bi in batches),
                    tuple(jnp.where(ok[bi], cnts[bi], n_ge[bi]) for bi in batches),
                    tuple(jnp.where(ok[bi], n_gt[bi], cnts[bi]) for bi in batches))
        init = (tuple(jnp.full((1, QW), INT_MIN, I32) for _ in batches), tuple(n_rows),
                tuple(jnp.zeros((1, QW), I32) for _ in batches))
        ts, n_ge, n_gt = lax.fori_loop(0, 32, step, init)
        for bi in batches:
            thr_ref[bi] = ts[bi]
            nge_ref[bi] = n_ge[bi]
            ngt_ref[bi] = n_gt[bi]

    thr = [as_float(thr_ref[bi]) for bi in batches]

    excess = _tree_sum([jnp.where((nge_ref[bi] > k_sel) & (thr[bi] > -jnp.inf), 1, 0) for bi in batches])

    @pl.when(jnp.max(excess) > 0)
    def _():
        tri = jnp.where(row_iota >= lane_iota, 1.0, 0.0).astype(BF16)
        need = [(k_sel - ngt_ref[bi]).astype(F32) for bi in batches]

        def demote(g, carry):
            rows = rows_of(g)
            out = []
            for bi in batches:
                blk = sc_ref[bi, rows, :]
                eq = jnp.where(blk == thr[bi], 1.0, 0.0)
                incl = jnp.dot(tri, eq.astype(BF16), preferred_element_type=F32)
                rank = incl - eq + carry[bi]
                sc_ref[bi, rows, :] = jnp.where((blk == thr[bi]) & (rank >= need[bi]), -jnp.inf, blk)
                out.append(carry[bi] + incl[KG - 1:KG, :])
            return tuple(out)

        lax.fori_loop(0, ngrp, demote, tuple(jnp.zeros((1, QW), F32) for _ in batches))

    thr_sel = [jnp.maximum(t, jnp.finfo(F32).min) for t in thr]
    for bi in batches:
        m_ref[bi] = jnp.full((1, hq), NEG_BIG, F32)
        acc_ref[bi] = jnp.zeros((CT_ROWS, hq), F32)

    def attend(g, with_bias):
        half = KG // 2
        chains = [(bi, h) for bi in batches for h in range(A_HEADS)]
        sel_h, c_h = {}, {}
        for bi in batches:
            for u in range(2):
                rows = pl.ds(pl.multiple_of(g * KG + u * half, half), half)
                sel_h[bi, u] = jnp.where(sc_ref[bi, rows, :] >= thr_sel[bi], 0.0, NEG_BIG)
                c_h[bi, u] = c_ref[bi, rows, :]
        s_all = {}
        for bi, h in chains:
            for u in range(2):
                s_all[bi, h, u] = jnp.dot(c_h[bi, u], qlt_ref[bi, h], preferred_element_type=F32)
        p_all = {}
        alpha_all = {}
        for bi, h in chains:
            cols = slice(h * QW, (h + 1) * QW)
            m_loc, p_loc = [], []
            for u in range(2):
                s = s_all[bi, h, u]
                if with_bias:
                    s = s + bias_ref[j - g, u * half:(u + 1) * half, cols]
                s = s + sel_h[bi, u]
                m_u = jnp.max(s, axis=0, keepdims=True)
                m_loc.append(m_u)
                p_loc.append(jnp.exp2(s - m_u).astype(BF16))
            m_old = m_ref[bi, :, cols]
            m_new = jnp.maximum(m_old, jnp.maximum(m_loc[0], m_loc[1]))
            alpha_all[bi, h] = jnp.exp2(m_old - m_new)
            p_all[bi, h] = jnp.concatenate([p_loc[u] * jnp.exp2(m_loc[u] - m_new).astype(BF16) for u in range(2)],
                                           axis=0)
            m_ref[bi, :, cols] = m_new
        for bi, h in chains:
            cols = slice(h * QW, (h + 1) * QW)
            pv = jnp.dot(ct_ref[bi, g], p_all[bi, h], preferred_element_type=F32)
            acc_ref[bi, :, cols] = acc_ref[bi, :, cols] * alpha_all[bi, h] + pv

    def attend_far(g, carry):
        attend(g, False)
        return carry

    def attend_near(g, carry):
        attend(g, True)
        return carry

    near0 = jnp.maximum(j - 1, 0)
    lax.fori_loop(0, near0, attend_far, 0)
    lax.fori_loop(near0, ngrp, attend_near, 0)

    for bi in batches:
        ot = acc_ref[bi, :A_KV_RANK, :] * (1.0 / acc_ref[bi, A_KV_RANK:A_KV_RANK + 1, :])
        for h in range(A_HEADS):
            o_h = ot[:, h * QW:(h + 1) * QW].T
            cols = slice(h * A_HEAD_DIM, (h + 1) * A_HEAD_DIM)
            oa = jnp.dot(o_h.astype(BF16), wuv_ref[h], preferred_element_type=F32)
            o_ref[bi, :, cols] = (oa * ga_ref[bi, :, cols].astype(F32)).astype(BF16)


def _dsa(iqt, iwt, ik, c, ctb, qlt, bias, w_uv, ga, bsz, seq):
    assert seq % QW == 0 and QW % Q_BLOCK == 0
    nb = DSA_NB if bsz % DSA_NB == 0 else 1
    k_sel = min(TOPK_MAX, seq // 4)
    hq = A_HEADS * QW
    return pl.pallas_call(
        functools.partial(_dsa_kernel, k_sel=k_sel, nb=nb),
        grid=(bsz // nb, seq // QW),
        in_specs=[
            pl.BlockSpec((nb, IDX_HEADS * IDX_DIM, QW), lambda b, j: (b, 0, j)),
            pl.BlockSpec((nb, IDX_HEADS, QW), lambda b, j: (b, 0, j)),
            pl.BlockSpec((nb, seq, IDX_DIM), lambda b, j: (b, 0, 0)),
            pl.BlockSpec((nb, seq, A_KV_RANK), lambda b, j: (b, 0, 0)),
            pl.BlockSpec((nb, seq // KG, CT_ROWS, KG), lambda b, j: (b, 0, 0, 0)),
            pl.BlockSpec((nb, A_HEADS, A_KV_RANK, QW), lambda b, j: (b, 0, 0, j)),
            pl.BlockSpec((2, KG, hq), lambda b, j: (0, 0, 0)),
            pl.BlockSpec((A_HEADS, A_KV_RANK, A_HEAD_DIM), lambda b, j: (0, 0, 0)),
            pl.BlockSpec((nb, QW, A_WIDTH), lambda b, j: (b, j, 0)),
        ],
        out_specs=pl.BlockSpec((nb, QW, A_WIDTH), lambda b, j: (b, j, 0)),
        out_shape=jax.ShapeDtypeStruct((bsz, seq, A_WIDTH), BF16),
        scratch_shapes=[
            pltpu.VMEM((nb, seq, QW), F32),
            pltpu.VMEM((nb, seq, QW), I16),
            pltpu.VMEM((nb, seq, QW), I16),
            pltpu.VMEM((nb, CT_ROWS, hq), F32),
            pltpu.VMEM((nb, 1, hq), F32),
            pltpu.VMEM((nb, 1, QW), I32),
            pltpu.VMEM((nb, 1, QW), I32),
            pltpu.VMEM((nb, 1, QW), I32),
        ],
        compiler_params=pltpu.CompilerParams(dimension_semantics=("parallel", "arbitrary"),
                                             vmem_limit_bytes=VMEM_LIMIT),
        name="dsa",
    )(iqt, iwt, ik.reshape(bsz, seq, IDX_DIM), c.reshape(bsz, seq, A_KV_RANK), ctb, qlt, bias,
      w_uv.astype(BF16), ga.reshape(bsz, seq, A_WIDTH))


def _hgrn_kernel(qs_ref, logf_ref, v_ref, gb_ref, hg_ref, o_ref, st_ref, bq_ref, oi_ref, *, nb):
    @pl.when(pl.program_id(1) == 0)
    def _():
        st_ref[...] = jnp.zeros(st_ref.shape, F32)

    steps = qs_ref.shape[1]
    nchunk = steps // CHUNK
    dk, dv = B_KEY_DIM, B_VAL_DIM
    r64 = lax.broadcasted_iota(I32, (CHUNK, CHUNK), 0)
    c64 = lax.broadcasted_iota(I32, (CHUNK, CHUNK), 1)
    tri = jnp.where(r64 >= c64, 1.0, 0.0).astype(BF16)
    rt = lax.broadcasted_iota(I32, (CHUNK, LANES), 0)
    cs = lax.broadcasted_iota(I32, (CHUNK, LANES), 1)
    causal = cs <= rt
    nsub = CHUNK // SUB
    zpad_f = jnp.zeros((LANES - CHUNK, LANES), F32)
    nt = (((1,), (1,)), ((), ()))
    units = [(bi, ci, h) for bi in range(nb) for ci in range(nchunk) for h in range(B_HEADS)]

    span = jnp.zeros((1, B_FORGET), F32)
    for bi in range(nb):
        for r0 in range(0, steps, SUB):
            span = jnp.maximum(span, -jnp.sum(logf_ref[bi, r0:r0 + SUB, :], axis=0, keepdims=True))
    safe = jnp.max(span) <= EXP_CAP
    chains = [(bi, h) for bi in range(nb) for h in range(B_HEADS)]
    lf_all, b_all = {}, {}

    def cum_decay():
        for bi in range(nb):
            for ci in range(nchunk):
                lf = logf_ref[bi, ci * CHUNK:(ci + 1) * CHUNK, :]
                hi = lf.astype(BF16)
                lo = (lf - hi.astype(F32)).astype(BF16)
                lf_all[bi, ci] = lf
                b_all[bi, ci] = (jnp.dot(tri, hi, preferred_element_type=F32)
                                 + jnp.dot(tri, lo, preferred_element_type=F32))

    def finish(o, bi, rows, h):
        cols = slice(h * dv, (h + 1) * dv)
        ms = jnp.mean(o * o, axis=-1, keepdims=True)
        z = o * lax.rsqrt(ms + RMS_EPS) * hg_ref[:, cols] * gb_ref[bi, rows, cols].astype(F32)
        o_ref[bi, rows, cols] = z.astype(BF16)

    def state_operands(u):
        bi, ci, h = u
        rows = slice(ci * CHUNK, (ci + 1) * CHUNK)
        cols = slice(h * dk, (h + 1) * dk)
        b = b_all[bi, ci][:, cols]
        kk = 1.0 - jnp.exp(lf_all[bi, ci][:, cols])
        b_last = b[CHUNK - 1:CHUNK, :]
        k_hat = (kk * jnp.exp(b_last - b)).astype(BF16)
        k_pad = jnp.concatenate([k_hat, jnp.zeros((LANES - CHUNK, dk), BF16)], axis=0)
        v = v_ref[bi, rows, h * dv:(h + 1) * dv]
        v_t = jnp.concatenate([v.astype(F32), zpad_f], axis=0).T.astype(BF16)
        return b, kk, k_pad, v_t, jnp.exp(b_last)

    @pl.when(safe)
    def _():
        cum_decay()
        q_in, q_st, kst, k_pad, v_t, e_last = {}, {}, {}, {}, {}, {}
        r_all, lhs, upd = {}, {}, {}

        def operands(u):
            bi, ci, h = u
            b, kk, k_pad[u], v_t[u], e_last[u] = state_operands(u)
            q = qs_ref[bi, ci * CHUNK:(ci + 1) * CHUNK, h * dk:(h + 1) * dk].astype(F32)
            starts = [jnp.zeros((1, dk), F32)] + [b[i * SUB - 1:i * SUB, :] for i in range(1, nsub)]
            ref_rows = jnp.concatenate([jnp.broadcast_to(s_, (SUB, dk)) for s_ in starts], axis=0)
            qi = q * jnp.exp(b - ref_rows)
            q_in[u] = qi.astype(BF16)
            q_st[u] = (qi * jnp.exp(ref_rows)).astype(BF16)
            secs = []
            for i in range(nsub):
                n = SUB * (i + 1)
                ks = (kk[:n, :] * jnp.exp(starts[i] - b[:n, :])).astype(BF16)
                secs.append(ks if n == CHUNK else jnp.concatenate([ks, jnp.zeros((CHUNK - n, dk), BF16)], axis=0))
            kst[u] = jnp.concatenate(secs, axis=0)

        def matmuls(u):
            r_all[u] = lax.dot_general(q_in[u], kst[u], nt, preferred_element_type=F32)
            upd[u] = jnp.dot(v_t[u], k_pad[u], preferred_element_type=F32)

        def scores(u):
            r = r_all[u]
            slabs = []
            for i in range(nsub):
                lane0 = (i * CHUNK // LANES) * LANES
                slab = r[i * SUB:(i + 1) * SUB, lane0:lane0 + LANES]
                if (i * CHUNK) % LANES:
                    slab = pltpu.roll(slab, LANES - (i * CHUNK) % LANES, 1)
                slabs.append(slab)
            a = jnp.concatenate(slabs, axis=0)
            a = jnp.where(causal, a, 0.0).astype(BF16)
            lhs[u] = jnp.concatenate([a, q_st[u]], axis=1)

        for u in units:
            operands(u)
        for u in units:
            matmuls(u)
            scores(u)

        st = {ch: st_ref[ch[0], ch[1]] for ch in chains}
        for ci in range(nchunk):
            for bi, h in chains:
                u = (bi, ci, h)
                w = jnp.concatenate([v_t[u], st[bi, h].astype(BF16)], axis=1)
                o = lax.dot_general(lhs[u], w, nt, preferred_element_type=F32)
                finish(o, bi, slice(ci * CHUNK, (ci + 1) * CHUNK), h)
            for bi, h in chains:
                st[bi, h] = st[bi, h] * e_last[bi, ci, h] + upd[bi, ci, h]
        for bi, h in chains:
            st_ref[bi, h] = st[bi, h]

    @pl.when(jnp.logical_not(safe))
    def _():
        cum_decay()
        for bi in range(nb):
            for ci in range(nchunk):
                rows = slice(ci * CHUNK, (ci + 1) * CHUNK)
                bq_ref[0, bi, rows, :] = b_all[bi, ci]
                bq_ref[1, bi, rows, :] = qs_ref[bi, rows, :].astype(F32)
        row_id = lax.broadcasted_iota(I32, (CHUNK, 1), 0)
        for bi in range(nb):
            for ci in range(nchunk):
                rows = slice(ci * CHUNK, (ci + 1) * CHUNK)
                b = b_all[bi, ci]
                kk = 1.0 - jnp.exp(lf_all[bi, ci])
                v = v_ref[bi, rows, :].astype(F32)

                def row(t, carry, bi=bi, ci=ci, b=b, kk=kk, v=v):
                    b_t = bq_ref[0, bi, pl.ds(ci * CHUNK + t, 1), :]
                    q_t = bq_ref[1, bi, pl.ds(ci * CHUNK + t, 1), :]
                    w = q_t * kk * jnp.exp(jnp.minimum(b_t - b, 0.0))
                    outs = []
                    for h in range(B_HEADS):
                        s_h = jnp.sum(w[:, h * dk:(h + 1) * dk], axis=1, keepdims=True)
                        s_h = jnp.where(row_id <= t, s_h, 0.0)
                        outs.append(jnp.sum(s_h * v[:, h * dv:(h + 1) * dv], axis=0, keepdims=True))
                    oi_ref[bi, pl.ds(ci * CHUNK + t, 1), :] = jnp.concatenate(outs, axis=1)
                    return carry

                lax.fori_loop(0, CHUNK, row, 0)
        st = {ch: st_ref[ch[0], ch[1]] for ch in chains}
        for ci in range(nchunk):
            rows = slice(ci * CHUNK, (ci + 1) * CHUNK)
            for bi, h in chains:
                b, kk, k_pad, v_t, e_last = state_operands((bi, ci, h))
                q_st = (qs_ref[bi, rows, h * dk:(h + 1) * dk].astype(F32) * jnp.exp(b)).astype(BF16)
                o = oi_ref[bi, rows, h * dv:(h + 1) * dv] + lax.dot_general(
                    q_st, st[bi, h].astype(BF16), nt, preferred_element_type=F32)
                finish(o, bi, rows, h)
                st[bi, h] = st[bi, h] * e_last + jnp.dot(v_t, k_pad, preferred_element_type=F32)
        for bi, h in chains:
            st_ref[bi, h] = st[bi, h]


def _hgrn(qs, logf, v, gb, hg, bsz, seq):
    steps = min(HGRN_STEPS, seq)
    nb = HGRN_NB if bsz % HGRN_NB == 0 else 1
    spec = lambda n: pl.BlockSpec((nb, steps, n), lambda b, s: (b, s, 0))
    return pl.pallas_call(
        functools.partial(_hgrn_kernel, nb=nb),
        grid=(bsz // nb, seq // steps),
        in_specs=[spec(B_FORGET), spec(B_FORGET), spec(B_WIDTH), spec(B_WIDTH),
                  pl.BlockSpec((1, B_WIDTH), lambda b, s: (0, 0))],
        out_specs=spec(B_WIDTH),
        out_shape=jax.ShapeDtypeStruct((bsz, seq, B_WIDTH), BF16),
        scratch_shapes=[pltpu.VMEM((nb, B_HEADS, B_VAL_DIM, B_KEY_DIM), F32),
                        pltpu.VMEM((2, nb, steps, B_FORGET), F32),
                        pltpu.VMEM((nb, steps, B_WIDTH), F32)],
        compiler_params=pltpu.CompilerParams(dimension_semantics=("parallel", "arbitrary"),
                                             vmem_limit_bytes=VMEM_LIMIT),
        name="hgrn",
    )(qs.reshape(bsz, seq, B_FORGET), logf.reshape(bsz, seq, B_FORGET), v.reshape(bsz, seq, B_WIDTH),
      gb.reshape(bsz, seq, B_WIDTH), hg.reshape(1, B_WIDTH).astype(F32))


def _out_kernel(za_ref, zb_ref, x_ref, wo_ref, lng_ref, lnb_ref, out_ref):
    z = jnp.concatenate([za_ref[...], zb_ref[...]], axis=1)
    y = jnp.dot(z, wo_ref[...], preferred_element_type=F32)
    r = DEEPNORM_ALPHA * x_ref[...] + y
    mu = jnp.mean(r, axis=-1, keepdims=True)
    rc = r - mu
    var = jnp.mean(rc * rc, axis=-1, keepdims=True)
    out_ref[...] = rc * lax.rsqrt(var + LN_EPS) * lng_ref[...] + lnb_ref[...]


def _output(za, zb, x2, w_o, ln_g, ln_b):
    t = x2.shape[0]
    tq = min(OUT_TILE, t)
    row = lambda n: pl.BlockSpec((tq, n), lambda i: (i, 0))
    full = lambda shape: pl.BlockSpec(shape, lambda i: (0,) * len(shape))
    d_mix = A_WIDTH + B_WIDTH
    return pl.pallas_call(
        _out_kernel,
        grid=(t // tq,),
        in_specs=[row(A_WIDTH), row(B_WIDTH), row(D_MODEL),
                  full((d_mix, D_MODEL)), full((1, D_MODEL)), full((1, D_MODEL))],
        out_specs=row(D_MODEL),
        out_shape=jax.ShapeDtypeStruct((t, D_MODEL), F32),
        compiler_params=pltpu.CompilerParams(dimension_semantics=("parallel",), vmem_limit_bytes=VMEM_LIMIT),
        name="out",
    )(za, zb, x2, w_o.astype(BF16), ln_g.reshape(1, D_MODEL).astype(F32), ln_b.reshape(1, D_MODEL).astype(F32))


def kernel(x, w_in, w_uk, w_uv, kv_norm_g, rel_bias, lb_logits, hgrn_norm_g, w_o, ln_g, ln_b):
    bsz, seq, d = x.shape
    assert d == D_MODEL and seq % Q_BLOCK == 0 and DEPTH == 1
    t = bsz * seq
    x2 = x.reshape(t, d)
    bias = _bias_tiles(rel_bias)
    ik, c, ga, gb, qs, logf, v, iqt, iwt, qlt, ctb = _project(
        x2, w_in[0], w_uk[0], kv_norm_g[0], lb_logits, bsz, seq)
    za = _dsa(iqt, iwt, ik, c, ctb, qlt, bias, w_uv[0], ga, bsz, seq)
    zb = _hgrn(qs, logf, v, gb, hgrn_norm_g[0], bsz, seq)
    out = _output(za.reshape(t, A_WIDTH), zb.reshape(t, B_WIDTH), x2, w_o[0], ln_g[0], ln_b[0])
    return out.reshape(bsz, seq, d)
```

```python
import functools
import math

import jax
import jax.numpy as jnp
from jax import lax
from jax.experimental import pallas as pl
from jax.experimental.pallas import tpu as pltpu

F32 = jnp.float32
BF16 = jnp.bfloat16
I32 = jnp.int32
I16 = jnp.int16

D_MODEL = 1024
DEPTH = 1
CHUNK = 64
Q_BLOCK = 128
A_HEADS = 4
A_HEAD_DIM = 128
A_WIDTH = A_HEADS * A_HEAD_DIM
A_KV_RANK = 128
IDX_HEADS = 8
IDX_DIM = 64
TOPK_MAX = 256
B_HEADS = 4
B_KEY_DIM = 128
B_VAL_DIM = 128
B_WIDTH = B_HEADS * B_VAL_DIM
B_FORGET = B_HEADS * B_KEY_DIM
REL_BUCKETS = 32
REL_MAX_DIST = 256
DEEPNORM_ALPHA = (2.0 * DEPTH) ** 0.25
LN_EPS = 1e-5
RMS_EPS = 1e-6

_SPLITS = (
    ('a_q', A_WIDTH), ('a_ckv', A_KV_RANK), ('a_iq', IDX_HEADS * IDX_DIM), ('a_ik', IDX_DIM),
    ('a_iw', IDX_HEADS), ('a_gate', A_WIDTH), ('b_q', B_FORGET), ('b_f', B_FORGET),
    ('b_i', B_WIDTH), ('b_gate', B_WIDTH),
)
_OFFSETS = {}
_off = 0
for _name, _n in _SPLITS:
    _OFFSETS[_name] = (_off, _n)
    _off += _n

LANES = 128
QW = 256
KG = 256
DSA_NB = 4
I16_ROWS = 16
CT_ROWS = A_KV_RANK + I16_ROWS
SUB = 16
HGRN_STEPS = 512
HGRN_NB = 2
EXP_CAP = 100.0
NEG_BIG = -1e30
LOG2_E = math.log2(math.e)
INT_MIN = -2 ** 31
PROJ_TILE = 1024
OUT_TILE = 1024
VMEM_LIMIT = 56 * 1024 * 1024


def _col(w, name):
    o, n = _OFFSETS[name]
    return w[:, o:o + n]


def _t5_bucket(rel):
    nb = REL_BUCKETS // 2
    max_exact = nb // 2
    ret = jnp.where(rel > 0, nb, 0).astype(I32)
    n = jnp.abs(rel)
    nf = jnp.maximum(n, max_exact).astype(F32)
    large = max_exact + jnp.floor(jnp.log(nf / max_exact) / math.log(REL_MAX_DIST / max_exact)
                                  * (nb - max_exact)).astype(I32)
    large = jnp.minimum(large, nb - 1)
    return ret + jnp.where(n < max_exact, n, large)


def _bias_kernel(rb_ref, out_ref):
    sl = lax.broadcasted_iota(I32, (KG, QW), 0)
    ql = lax.broadcasted_iota(I32, (KG, QW), 1)
    far = REL_BUCKETS // 2 - 1
    for d in range(2):
        bucket = _t5_bucket(sl - ql - KG * d)
        for h in range(A_HEADS):
            acc = jnp.zeros((KG, QW), F32)
            for bk in range(REL_BUCKETS):
                acc = jnp.where(bucket == bk, (rb_ref[bk, h] - rb_ref[far, h]) * LOG2_E, acc)
            out_ref[d, :, h * QW:(h + 1) * QW] = acc


def _bias_tiles(rel_bias):
    assert KG == QW and KG >= REL_MAX_DIST
    return pl.pallas_call(
        _bias_kernel,
        out_shape=jax.ShapeDtypeStruct((2, KG, A_HEADS * QW), F32),
        in_specs=[pl.BlockSpec(memory_space=pltpu.SMEM)],
        out_specs=pl.BlockSpec(memory_space=pltpu.VMEM),
        name="bias",
    )(rel_bias.astype(F32))


def _proj_kernel(x_ref, wrow_ref, wt_ref, wukt_ref, g_ref, lbl_ref,
                 ik_ref, c_ref, ga_ref, gb_ref, qs_ref, logf_ref, v_ref,
                 iqt_ref, iwt_ref, qlt_ref, ct_ref):
    tq = x_ref.shape[0]
    xb = x_ref[...].astype(BF16)

    def rowdot(lo, n):
        return jnp.dot(xb, wrow_ref[:, lo:lo + n], preferred_element_type=F32)

    h0 = rowdot(0, 256)
    a = h0[:, :A_KV_RANK]
    ms = jnp.mean(a * a, axis=-1, keepdims=True)
    c = a * lax.rsqrt(ms + RMS_EPS) * g_ref[...]
    c_ref[...] = c.astype(BF16)
    ct = c.T
    ones = jnp.ones((CT_ROWS - A_KV_RANK, KG), BF16)
    for i in range(tq // KG):
        ct_ref[0, i] = jnp.concatenate([ct[:, i * KG:(i + 1) * KG].astype(BF16), ones], axis=0)
    ik_ref[...] = h0[:, A_KV_RANK:A_KV_RANK + IDX_DIM].astype(BF16)

    ga_ref[...] = jax.nn.silu(rowdot(256, A_WIDTH)).astype(BF16)
    qs_ref[...] = jax.nn.silu(rowdot(768, B_FORGET)).astype(BF16)
    lbl = lbl_ref[...]
    e = jnp.exp(lbl - jnp.max(lbl, axis=0, keepdims=True))
    lb = e[0:1, :] / jnp.sum(e, axis=0, keepdims=True)
    f = lb + (1.0 - lb) * jax.nn.sigmoid(rowdot(1280, B_FORGET))
    logf_ref[...] = jnp.log2(f)
    v_ref[...] = rowdot(1792, B_WIDTH).astype(BF16)
    gb_ref[...] = jax.nn.silu(rowdot(2304, B_WIDTH)).astype(BF16)

    ht = lax.dot_general(wt_ref[...], xb, (((1,), (1,)), ((), ())), preferred_element_type=F32)
    iqt_ref[0] = ht[A_WIDTH:A_WIDTH + IDX_HEADS * IDX_DIM].astype(BF16)
    o = A_WIDTH + IDX_HEADS * IDX_DIM
    iwt_ref[0] = ht[o:o + IDX_HEADS] * (IDX_HEADS ** -0.5 * IDX_DIM ** -0.5)
    for h in range(A_HEADS):
        aq = ht[h * A_HEAD_DIM:(h + 1) * A_HEAD_DIM].astype(BF16)
        ql = jnp.dot(wukt_ref[h], aq, preferred_element_type=F32) * (A_HEAD_DIM ** -0.5 * LOG2_E)
        qlt_ref[0, h] = ql.astype(BF16)


def _project(x2, w_in, w_uk, kv_g, lb_logits, bsz, seq):
    t = bsz * seq
    tq = min(PROJ_TILE, seq)
    per_b = seq // tq
    pad64 = jnp.zeros((D_MODEL, 64), F32)
    wrow = jnp.concatenate([_col(w_in, 'a_ckv'), _col(w_in, 'a_ik'), pad64, _col(w_in, 'a_gate'),
                            _col(w_in, 'b_q'), _col(w_in, 'b_f'), _col(w_in, 'b_i'),
                            _col(w_in, 'b_gate')], axis=1).astype(BF16)
    pad8 = jnp.zeros((D_MODEL, 8), F32)
    wt = jnp.concatenate([_col(w_in, 'a_q'), _col(w_in, 'a_iq'), _col(w_in, 'a_iw'), pad8],
                         axis=1).T.astype(BF16)
    wukt = jnp.swapaxes(w_uk, 1, 2).astype(BF16)
    nrow, nt = wrow.shape[1], wt.shape[0]
    row = lambda n: pl.BlockSpec((tq, n), lambda i: (i, 0))
    full = lambda shape: pl.BlockSpec(shape, lambda i: (0,) * len(shape))
    out_shape = (
        jax.ShapeDtypeStruct((t, IDX_DIM), BF16),
        jax.ShapeDtypeStruct((t, A_KV_RANK), BF16),
        jax.ShapeDtypeStruct((t, A_WIDTH), BF16),
        jax.ShapeDtypeStruct((t, B_WIDTH), BF16),
        jax.ShapeDtypeStruct((t, B_FORGET), BF16),
        jax.ShapeDtypeStruct((t, B_FORGET), F32),
        jax.ShapeDtypeStruct((t, B_WIDTH), BF16),
        jax.ShapeDtypeStruct((bsz, IDX_HEADS * IDX_DIM, seq), BF16),
        jax.ShapeDtypeStruct((bsz, IDX_HEADS, seq), F32),
        jax.ShapeDtypeStruct((bsz, A_HEADS, A_KV_RANK, seq), BF16),
        jax.ShapeDtypeStruct((bsz, seq // KG, CT_ROWS, KG), BF16),
    )
    out_specs = (
        row(IDX_DIM), row(A_KV_RANK), row(A_WIDTH), row(B_WIDTH), row(B_FORGET), row(B_FORGET), row(B_WIDTH),
        pl.BlockSpec((1, IDX_HEADS * IDX_DIM, tq), lambda i: (i // per_b, 0, i % per_b)),
        pl.BlockSpec((1, IDX_HEADS, tq), lambda i: (i // per_b, 0, i % per_b)),
        pl.BlockSpec((1, A_HEADS, A_KV_RANK, tq), lambda i: (i // per_b, 0, 0, i % per_b)),
        pl.BlockSpec((1, tq // KG, CT_ROWS, KG), lambda i: (i // per_b, i % per_b, 0, 0)),
    )
    return pl.pallas_call(
        _proj_kernel,
        grid=(t // tq,),
        in_specs=[row(D_MODEL), full((D_MODEL, nrow)), full((nt, D_MODEL)),
                  full((A_HEADS, A_KV_RANK, A_HEAD_DIM)), full((1, A_KV_RANK)), full((DEPTH + 1, B_FORGET))],
        out_specs=out_specs,
        out_shape=out_shape,
        compiler_params=pltpu.CompilerParams(dimension_semantics=("parallel",), vmem_limit_bytes=VMEM_LIMIT),
        name="proj",
    )(x2, wrow, wt, wukt, kv_g.reshape(1, A_KV_RANK).astype(F32), lb_logits.astype(F32))


def _chunk_of(pos):
    return lax.shift_right_logical(pos, int(math.log2(CHUNK)))


def _tree_sum(parts):
    while len(parts) > 1:
        parts = [a + b for a, b in zip(parts[::2], parts[1::2])] + ([parts[-1]] if len(parts) % 2 else [])
    return parts[0]


def _order_key(bits):
    return jnp.where(bits < 0, jnp.int32(INT_MIN) - bits, bits)


def _dsa_kernel(iqt_ref, iwt_ref, ik_ref, c_ref, ct_ref, qlt_ref, bias_ref, wuv_ref, ga_ref, o_ref,
                sc_ref, hi_ref, lo_ref, acc_ref, m_ref, thr_ref, nge_ref, ngt_ref, *, k_sel, nb):
    j = pl.program_id(1)
    ngrp = j + 1
    hq = A_HEADS * QW
    batches = range(nb)
    row_iota = lax.broadcasted_iota(I32, (KG, QW), 0)
    lane_iota = lax.broadcasted_iota(I32, (KG, QW), 1)
    q_chunk = _chunk_of(j * QW + lane_iota)

    def rows_of(g):
        return pl.ds(pl.multiple_of(g * KG, KG), KG)

    def score_group(g, diagonal):
        rows = rows_of(g)
        for bi in batches:
            rhs = jnp.concatenate([iqt_ref[bi, h * IDX_DIM:(h + 1) * IDX_DIM, :] for h in range(IDX_HEADS)],
                                  axis=1)
            d = jnp.dot(ik_ref[bi, rows, :], rhs, preferred_element_type=F32)
            sc = iwt_ref[bi, 0:1, :] * jnp.maximum(d[:, :QW], 0.0)
            for h in range(1, IDX_HEADS):
                sc = sc + iwt_ref[bi, h:h + 1, :] * jnp.maximum(d[:, h * QW:(h + 1) * QW], 0.0)
            if diagonal:
                sc = jnp.where(_chunk_of(g * KG + row_iota) <= q_chunk, sc, -jnp.inf)
            sc_ref[bi, rows, :] = sc
            key = _order_key(lax.bitcast_convert_type(sc, I32))
            hi_ref[bi, rows, :] = (key >> 16).astype(I16)
            lo_ref[bi, rows, :] = (key ^ 0x8000).astype(I16)

    def score_inner(g, carry):
        score_group(g, False)
        return carry

    lax.fori_loop(0, j, score_inner, 0)
    score_group(j, True)

    def count16(ref, cands):
        c16 = [c.astype(I16) for c in cands]

        def one(g, accs):
            rows = rows_of(g)
            out = []
            for bi in batches:
                blk = ref[bi, rows, :]
                ind = jnp.where(blk >= c16[bi], jnp.int16(1), jnp.int16(0))
                out.append(accs[bi] + _tree_sum([ind[r:r + I16_ROWS] for r in range(0, KG, I16_ROWS)]))
            return tuple(out)

        def two(p, accs):
            return one(2 * p + 1, one(2 * p, accs))

        accs = tuple(jnp.zeros((I16_ROWS, QW), I16) for _ in batches)
        accs = lax.fori_loop(0, ngrp // 2, two, accs)
        accs = lax.fori_loop(2 * (ngrp // 2), ngrp, one, accs)
        return [jnp.sum(a.astype(I32), axis=0, keepdims=True) for a in accs]

    def bisect16(ref, want, state, first_bit, last_bit):
        def step(i, carry):
            ts, n_ge, n_gt = carry
            cands = [t + lax.shift_left(jnp.int32(1), 15 - i) for t in ts]
            cnts = count16(ref, cands)
            ok = [cnts[bi] >= want[bi] for bi in batches]
            return (tuple(jnp.where(ok[bi], cands[bi], ts[bi]) for bi in batches),
                    tuple(jnp.where(ok[bi], cnts[bi], n_ge[bi]) for bi in batches),
                    tuple(jnp.where(ok[bi], n_gt[bi], cnts[bi]) for bi in batches))
        return lax.fori_loop(first_bit, last_bit, step, state)

    def start16(n_all):
        return (tuple(jnp.full((1, QW), -32768, I32) for _ in batches), tuple(n_all),
                tuple(jnp.zeros((1, QW), I32) for _ in batches))

    want_hi = [jnp.full((1, QW), k_sel, I32) for _ in batches]
    n_rows = [jnp.zeros((1, QW), I32) + ngrp * KG for _ in batches]
    p_hi, n_ge_hi, n_gt_hi = bisect16(hi_ref, want_hi, start16(n_rows), 0, 16)

    def keep_ties(g, carry):
        rows = rows_of(g)
        for bi in batches:
            lo_ref[bi, rows, :] = jnp.where(hi_ref[bi, rows, :] == p_hi[bi].astype(I16), lo_ref[bi, rows, :],
                                            jnp.int16(-32768))
        return carry

    lax.fori_loop(0, ngrp, keep_ties, 0)
    want_lo = [k_sel - n_gt_hi[bi] for bi in batches]
    n_eq_hi = [n_ge_hi[bi] - n_gt_hi[bi] for bi in batches]
    p_lo, n_ge_lo, n_gt_lo = bisect16(lo_ref, want_lo, start16(n_eq_hi), 0, 16)
    for bi in batches:
        thr_ref[bi] = lax.shift_left(p_hi[bi], 16) + (p_lo[bi] + 32768)
        nge_ref[bi] = n_gt_hi[bi] + n_ge_lo[bi]
        ngt_ref[bi] = n_gt_hi[bi] + n_gt_lo[bi]

    def as_float(key):
        return lax.bitcast_convert_type(_order_key(key), F32)

    def count32(cands):
        def body(g, accs):
            rows = rows_of(g)
            out = []
            for bi in batches:
                ind = jnp.where(sc_ref[bi, rows, :] >= cands[bi], 1, 0)
                out.append(accs[bi] + jnp.sum(ind.reshape(KG // 8, 8, QW), axis=0))
            return tuple(out)

        accs = lax.fori_loop(0, ngrp, body, tuple(jnp.zeros((8, QW), I32) for _ in batches))
        return [jnp.sum(a, axis=0, keepdims=True) for a in accs]

    n_ge_f = count32([as_float(thr_ref[bi]) for bi in batches])
    differs = _tree_sum([jnp.where(n_ge_f[bi] != nge_ref[bi], 1, 0) for bi in batches])

    @pl.when(jnp.max(differs) > 0)
    def _():
        def step(i, carry):
            ts, n_ge, n_gt = carry
            cands = [t + lax.shift_left(jnp.int32(1), 31 - i) for t in ts]
            cnts = count32([as_float(c) for c in cands])
            ok = [cnts[bi] >= k_sel for bi in batches]
            return (tuple(jnp.where(ok[bi], cands[bi], ts[bi]) for bi in batches),
                    tuple(jnp.where(ok[bi], cnts[bi], n_ge[bi]) for bi in batches),
                    tuple(jnp.where(ok[bi], n_gt[bi], cnts[bi]) for bi in batches))
        init = (tuple(jnp.full((1, QW), INT_MIN, I32) for _ in batches), tuple(n_rows),
                tuple(jnp.zeros((1, QW), I32) for _ in batches))
        ts, n_ge, n_gt = lax.fori_loop(0, 32, step, init)
        for bi in batches:
            thr_ref[bi] = ts[bi]
            nge_ref[bi] = n_ge[bi]
            ngt_ref[bi] = n_gt[bi]

    thr = [as_float(thr_ref[bi]) for bi in batches]

    excess = _tree_sum([jnp.where((nge_ref[bi] > k_sel) & (thr[bi] > -jnp.inf), 1, 0) for bi in batches])

    @pl.when(jnp.max(excess) > 0)
    def _():
        tri = jnp.where(row_iota >= lane_iota, 1.0, 0.0).astype(BF16)
        need = [(k_sel - ngt_ref[bi]).astype(F32) for bi in batches]

        def demote(g, carry):
            rows = rows_of(g)
            out = []
            for bi in batches:
                blk = sc_ref[bi, rows, :]
                eq = jnp.where(blk == thr[bi], 1.0, 0.0)
                incl = jnp.dot(tri, eq.astype(BF16), preferred_element_type=F32)
                rank = incl - eq + carry[bi]
                sc_ref[bi, rows, :] = jnp.where((blk == thr[bi]) & (rank >= need[bi]), -jnp.inf, blk)
                out.append(carry[bi] + incl[KG - 1:KG, :])
            return tuple(out)

        lax.fori_loop(0, ngrp, demote, tuple(jnp.zeros((1, QW), F32) for _ in batches))

    thr_sel = [jnp.maximum(t, jnp.finfo(F32).min) for t in thr]
    for bi in batches:
        m_ref[bi] = jnp.full((1, hq), NEG_BIG, F32)
        acc_ref[bi] = jnp.zeros((CT_ROWS, hq), F32)

    def attend(g, with_bias):
        half = KG // 2
        chains = [(bi, h) for bi in batches for h in range(A_HEADS)]
        sel_h, c_h = {}, {}
        for bi in batches:
            for u in range(2):
                rows = pl.ds(pl.multiple_of(g * KG + u * half, half), half)
                sel_h[bi, u] = jnp.where(sc_ref[bi, rows, :] >= thr_sel[bi], 0.0, NEG_BIG)
                c_h[bi, u] = c_ref[bi, rows, :]
        s_all = {}
        for bi, h in chains:
            for u in range(2):
                s_all[bi, h, u] = jnp.dot(c_h[bi, u], qlt_ref[bi, h], preferred_element_type=F32)
        p_all = {}
        alpha_all = {}
        for bi, h in chains:
            cols = slice(h * QW, (h + 1) * QW)
            m_loc, p_loc = [], []
            for u in range(2):
                s = s_all[bi, h, u]
                if with_bias:
                    s = s + bias_ref[j - g, u * half:(u + 1) * half, cols]
                s = s + sel_h[bi, u]
                m_u = jnp.max(s, axis=0, keepdims=True)
                m_loc.append(m_u)
                p_loc.append(jnp.exp2(s - m_u).astype(BF16))
            m_old = m_ref[bi, :, cols]
            m_new = jnp.maximum(m_old, jnp.maximum(m_loc[0], m_loc[1]))
            alpha_all[bi, h] = jnp.exp2(m_old - m_new)
            p_all[bi, h] = jnp.concatenate([p_loc[u] * jnp.exp2(m_loc[u] - m_new).astype(BF16) for u in range(2)],
                                           axis=0)
            m_ref[bi, :, cols] = m_new
        for bi, h in chains:
            cols = slice(h * QW, (h + 1) * QW)
            pv = jnp.dot(ct_ref[bi, g], p_all[bi, h], preferred_element_type=F32)
            acc_ref[bi, :, cols] = acc_ref[bi, :, cols] * alpha_all[bi, h] + pv

    def attend_far(g, carry):
        attend(g, False)
        return carry

    lax.fori_loop(0, j - 1, attend_far, 0)

    @pl.when(j >= 1)
    def _():
        attend(j - 1, True)

    attend(j, True)

    for bi in batches:
        ot = acc_ref[bi, :A_KV_RANK, :] * (1.0 / acc_ref[bi, A_KV_RANK:A_KV_RANK + 1, :])
        for h in range(A_HEADS):
            o_h = ot[:, h * QW:(h + 1) * QW].T
            cols = slice(h * A_HEAD_DIM, (h + 1) * A_HEAD_DIM)
            oa = jnp.dot(o_h.astype(BF16), wuv_ref[h], preferred_element_type=F32)
            o_ref[bi, :, cols] = (oa * ga_ref[bi, :, cols].astype(F32)).astype(BF16)


def _dsa(iqt, iwt, ik, c, ctb, qlt, bias, w_uv, ga, bsz, seq):
    assert seq % QW == 0 and QW % Q_BLOCK == 0
    nb = DSA_NB if bsz % DSA_NB == 0 else 1
    k_sel = min(TOPK_MAX, seq // 4)
    hq = A_HEADS * QW
    return pl.pallas_call(
        functools.partial(_dsa_kernel, k_sel=k_sel, nb=nb),
        grid=(bsz // nb, seq // QW),
        in_specs=[
            pl.BlockSpec((nb, IDX_HEADS * IDX_DIM, QW), lambda b, j: (b, 0, j)),
            pl.BlockSpec((nb, IDX_HEADS, QW), lambda b, j: (b, 0, j)),
            pl.BlockSpec((nb, seq, IDX_DIM), lambda b, j: (b, 0, 0)),
            pl.BlockSpec((nb, seq, A_KV_RANK), lambda b, j: (b, 0, 0)),
            pl.BlockSpec((nb, seq // KG, CT_ROWS, KG), lambda b, j: (b, 0, 0, 0)),
            pl.BlockSpec((nb, A_HEADS, A_KV_RANK, QW), lambda b, j: (b, 0, 0, j)),
            pl.BlockSpec((2, KG, hq), lambda b, j: (0, 0, 0)),
            pl.BlockSpec((A_HEADS, A_KV_RANK, A_HEAD_DIM), lambda b, j: (0, 0, 0)),
            pl.BlockSpec((nb, QW, A_WIDTH), lambda b, j: (b, j, 0)),
        ],
        out_specs=pl.BlockSpec((nb, QW, A_WIDTH), lambda b, j: (b, j, 0)),
        out_shape=jax.ShapeDtypeStruct((bsz, seq, A_WIDTH), BF16),
        scratch_shapes=[
            pltpu.VMEM((nb, seq, QW), F32),
            pltpu.VMEM((nb, seq, QW), I16),
            pltpu.VMEM((nb, seq, QW), I16),
            pltpu.VMEM((nb, CT_ROWS, hq), F32),
            pltpu.VMEM((nb, 1, hq), F32),
            pltpu.VMEM((nb, 1, QW), I32),
            pltpu.VMEM((nb, 1, QW), I32),
            pltpu.VMEM((nb, 1, QW), I32),
        ],
        compiler_params=pltpu.CompilerParams(dimension_semantics=("parallel", "arbitrary"),
                                             vmem_limit_bytes=VMEM_LIMIT),
        name="dsa",
    )(iqt, iwt, ik.reshape(bsz, seq, IDX_DIM), c.reshape(bsz, seq, A_KV_RANK), ctb, qlt, bias,
      w_uv.astype(BF16), ga.reshape(bsz, seq, A_WIDTH))


def _hgrn_kernel(qs_ref, logf_ref, v_ref, gb_ref, hg_ref, o_ref, st_ref, bq_ref, oi_ref, *, nb):
    @pl.when(pl.program_id(1) == 0)
    def _():
        st_ref[...] = jnp.zeros(st_ref.shape, F32)

    steps = qs_ref.shape[1]
    nchunk = steps // CHUNK
    dk, dv = B_KEY_DIM, B_VAL_DIM
    r64 = lax.broadcasted_iota(I32, (CHUNK, CHUNK), 0)
    c64 = lax.broadcasted_iota(I32, (CHUNK, CHUNK), 1)
    tri = jnp.where(r64 >= c64, 1.0, 0.0).astype(BF16)
    rt = lax.broadcasted_iota(I32, (CHUNK, LANES), 0)
    cs = lax.broadcasted_iota(I32, (CHUNK, LANES), 1)
    causal = cs <= rt
    nsub = CHUNK // SUB
    zpad_f = jnp.zeros((LANES - CHUNK, LANES), F32)
    nt = (((1,), (1,)), ((), ()))
    units = [(bi, ci, h) for bi in range(nb) for ci in range(nchunk) for h in range(B_HEADS)]

    span = jnp.zeros((1, B_FORGET), F32)
    for bi in range(nb):
        for r0 in range(0, steps, SUB):
            span = jnp.maximum(span, -jnp.sum(logf_ref[bi, r0:r0 + SUB, :], axis=0, keepdims=True))
    safe = jnp.max(span) <= EXP_CAP
    chains = [(bi, h) for bi in range(nb) for h in range(B_HEADS)]
    lf_all, b_all = {}, {}

    def cum_decay():
        for bi in range(nb):
            for ci in range(nchunk):
                lf = logf_ref[bi, ci * CHUNK:(ci + 1) * CHUNK, :]
                hi = lf.astype(BF16)
                lo = (lf - hi.astype(F32)).astype(BF16)
                lf_all[bi, ci] = lf
                b_all[bi, ci] = (jnp.dot(tri, hi, preferred_element_type=F32)
                                 + jnp.dot(tri, lo, preferred_element_type=F32))

    def finish(o, bi, rows, h):
        cols = slice(h * dv, (h + 1) * dv)
        ms = jnp.mean(o * o, axis=-1, keepdims=True)
        z = o * lax.rsqrt(ms + RMS_EPS) * hg_ref[:, cols] * gb_ref[bi, rows, cols].astype(F32)
        o_ref[bi, rows, cols] = z.astype(BF16)

    def state_operands(u):
        bi, ci, h = u
        rows = slice(ci * CHUNK, (ci + 1) * CHUNK)
        cols = slice(h * dk, (h + 1) * dk)
        b = b_all[bi, ci][:, cols]
        kk = 1.0 - jnp.exp2(lf_all[bi, ci][:, cols])
        b_last = b[CHUNK - 1:CHUNK, :]
        k_hat = (kk * jnp.exp2(b_last - b)).astype(BF16)
        k_pad = jnp.concatenate([k_hat, jnp.zeros((LANES - CHUNK, dk), BF16)], axis=0)
        v = v_ref[bi, rows, h * dv:(h + 1) * dv]
        v_t = jnp.concatenate([v.astype(F32), zpad_f], axis=0).T.astype(BF16)
        return b, kk, k_pad, v_t, jnp.exp2(b_last)

    @pl.when(safe)
    def _():
        cum_decay()
        q_in, q_st, kst, k_pad, v_t, e_last = {}, {}, {}, {}, {}, {}
        r_all, lhs, upd = {}, {}, {}

        def operands(u):
            bi, ci, h = u
            b, kk, k_pad[u], v_t[u], e_last[u] = state_operands(u)
            q = qs_ref[bi, ci * CHUNK:(ci + 1) * CHUNK, h * dk:(h + 1) * dk].astype(F32)
            starts = [jnp.zeros((1, dk), F32)] + [b[i * SUB - 1:i * SUB, :] for i in range(1, nsub)]
            ref_rows = jnp.concatenate([jnp.broadcast_to(s_, (SUB, dk)) for s_ in starts], axis=0)
            qi = q * jnp.exp2(b - ref_rows)
            q_in[u] = qi.astype(BF16)
            q_st[u] = (qi * jnp.exp2(ref_rows)).astype(BF16)
            secs = []
            for i in range(nsub):
                n = SUB * (i + 1)
                ks = (kk[:n, :] * jnp.exp2(starts[i] - b[:n, :])).astype(BF16)
                secs.append(ks if n == CHUNK else jnp.concatenate([ks, jnp.zeros((CHUNK - n, dk), BF16)], axis=0))
            kst[u] = jnp.concatenate(secs, axis=0)

        def matmuls(u):
            r_all[u] = lax.dot_general(q_in[u], kst[u], nt, preferred_element_type=F32)
            upd[u] = jnp.dot(v_t[u], k_pad[u], preferred_element_type=F32)

        def scores(u):
            r = r_all[u]
            slabs = []
            for i in range(nsub):
                lane0 = (i * CHUNK // LANES) * LANES
                slab = r[i * SUB:(i + 1) * SUB, lane0:lane0 + LANES]
                if (i * CHUNK) % LANES:
                    slab = pltpu.roll(slab, LANES - (i * CHUNK) % LANES, 1)
                slabs.append(slab)
            a = jnp.concatenate(slabs, axis=0)
            a = jnp.where(causal, a, 0.0).astype(BF16)
            lhs[u] = jnp.concatenate([a, q_st[u]], axis=1)

        for u in units:
            operands(u)
        for u in units:
            matmuls(u)
            scores(u)

        st = {ch: st_ref[ch[0], ch[1]] for ch in chains}
        for ci in range(nchunk):
            for bi, h in chains:
                u = (bi, ci, h)
                w = jnp.concatenate([v_t[u], st[bi, h].astype(BF16)], axis=1)
                o = lax.dot_general(lhs[u], w, nt, preferred_element_type=F32)
                finish(o, bi, slice(ci * CHUNK, (ci + 1) * CHUNK), h)
            for bi, h in chains:
                st[bi, h] = st[bi, h] * e_last[bi, ci, h] + upd[bi, ci, h]
        for bi, h in chains:
            st_ref[bi, h] = st[bi, h]

    @pl.when(jnp.logical_not(safe))
    def _():
        cum_decay()
        for bi in range(nb):
            for ci in range(nchunk):
                rows = slice(ci * CHUNK, (ci + 1) * CHUNK)
                bq_ref[0, bi, rows, :] = b_all[bi, ci]
                bq_ref[1, bi, rows, :] = qs_ref[bi, rows, :].astype(F32)
        row_id = lax.broadcasted_iota(I32, (CHUNK, 1), 0)
        for bi in range(nb):
            for ci in range(nchunk):
                rows = slice(ci * CHUNK, (ci + 1) * CHUNK)
                b = b_all[bi, ci]
                kk = 1.0 - jnp.exp2(lf_all[bi, ci])
                v = v_ref[bi, rows, :].astype(F32)

                def row(t, carry, bi=bi, ci=ci, b=b, kk=kk, v=v):
                    b_t = bq_ref[0, bi, pl.ds(ci * CHUNK + t, 1), :]
                    q_t = bq_ref[1, bi, pl.ds(ci * CHUNK + t, 1), :]
                    w = q_t * kk * jnp.exp2(jnp.minimum(b_t - b, 0.0))
                    outs = []
                    for h in range(B_HEADS):
                        s_h = jnp.sum(w[:, h * dk:(h + 1) * dk], axis=1, keepdims=True)
                        s_h = jnp.where(row_id <= t, s_h, 0.0)
                        outs.append(jnp.sum(s_h * v[:, h * dv:(h + 1) * dv], axis=0, keepdims=True))
                    oi_ref[bi, pl.ds(ci * CHUNK + t, 1), :] = jnp.concatenate(outs, axis=1)
                    return carry

                lax.fori_loop(0, CHUNK, row, 0)
        st = {ch: st_ref[ch[0], ch[1]] for ch in chains}
        for ci in range(nchunk):
            rows = slice(ci * CHUNK, (ci + 1) * CHUNK)
            for bi, h in chains:
                b, kk, k_pad, v_t, e_last = state_operands((bi, ci, h))
                q_st = (qs_ref[bi, rows, h * dk:(h + 1) * dk].astype(F32) * jnp.exp2(b)).astype(BF16)
                o = oi_ref[bi, rows, h * dv:(h + 1) * dv] + lax.dot_general(
                    q_st, st[bi, h].astype(BF16), nt, preferred_element_type=F32)
                finish(o, bi, rows, h)
                st[bi, h] = st[bi, h] * e_last + jnp.dot(v_t, k_pad, preferred_element_type=F32)
        for bi, h in chains:
            st_ref[bi, h] = st[bi, h]


def _hgrn(qs, logf, v, gb, hg, bsz, seq):
    steps = min(HGRN_STEPS, seq)
    nb = HGRN_NB if bsz % HGRN_NB == 0 else 1
    spec = lambda n: pl.BlockSpec((nb, steps, n), lambda b, s: (b, s, 0))
    return pl.pallas_call(
        functools.partial(_hgrn_kernel, nb=nb),
        grid=(bsz // nb, seq // steps),
        in_specs=[spec(B_FORGET), spec(B_FORGET), spec(B_WIDTH), spec(B_WIDTH),
                  pl.BlockSpec((1, B_WIDTH), lambda b, s: (0, 0))],
        out_specs=spec(B_WIDTH),
        out_shape=jax.ShapeDtypeStruct((bsz, seq, B_WIDTH), BF16),
        scratch_shapes=[pltpu.VMEM((nb, B_HEADS, B_VAL_DIM, B_KEY_DIM), F32),
                        pltpu.VMEM((2, nb, steps, B_FORGET), F32),
                        pltpu.VMEM((nb, steps, B_WIDTH), F32)],
        compiler_params=pltpu.CompilerParams(dimension_semantics=("parallel", "arbitrary"),
                                             vmem_limit_bytes=VMEM_LIMIT),
        name="hgrn",
    )(qs.reshape(bsz, seq, B_FORGET), logf.reshape(bsz, seq, B_FORGET), v.reshape(bsz, seq, B_WIDTH),
      gb.reshape(bsz, seq, B_WIDTH), hg.reshape(1, B_WIDTH).astype(F32))


def _out_kernel(za_ref, zb_ref, x_ref, wo_ref, lng_ref, lnb_ref, out_ref):
    z = jnp.concatenate([za_ref[...], zb_ref[...]], axis=1)
    y = jnp.dot(z, wo_ref[...], preferred_element_type=F32)
    r = DEEPNORM_ALPHA * x_ref[...] + y
    mu = jnp.mean(r, axis=-1, keepdims=True)
    rc = r - mu
    var = jnp.mean(rc * rc, axis=-1, keepdims=True)
    out_ref[...] = rc * lax.rsqrt(var + LN_EPS) * lng_ref[...] + lnb_ref[...]


def _output(za, zb, x2, w_o, ln_g, ln_b):
    t = x2.shape[0]
    tq = min(OUT_TILE, t)
    row = lambda n: pl.BlockSpec((tq, n), lambda i: (i, 0))
    full = lambda shape: pl.BlockSpec(shape, lambda i: (0,) * len(shape))
    d_mix = A_WIDTH + B_WIDTH
    return pl.pallas_call(
        _out_kernel,
        grid=(t // tq,),
        in_specs=[row(A_WIDTH), row(B_WIDTH), row(D_MODEL),
                  full((d_mix, D_MODEL)), full((1, D_MODEL)), full((1, D_MODEL))],
        out_specs=row(D_MODEL),
        out_shape=jax.ShapeDtypeStruct((t, D_MODEL), F32),
        compiler_params=pltpu.CompilerParams(dimension_semantics=("parallel",), vmem_limit_bytes=VMEM_LIMIT),
        name="out",
    )(za, zb, x2, w_o.astype(BF16), ln_g.reshape(1, D_MODEL).astype(F32), ln_b.reshape(1, D_MODEL).astype(F32))


def kernel(x, w_in, w_uk, w_uv, kv_norm_g, rel_bias, lb_logits, hgrn_norm_g, w_o, ln_g, ln_b):
    bsz, seq, d = x.shape
    assert d == D_MODEL and seq % Q_BLOCK == 0 and DEPTH == 1
    t = bsz * seq
    x2 = x.reshape(t, d)
    bias = _bias_tiles(rel_bias)
    ik, c, ga, gb, qs, logf, v, iqt, iwt, qlt, ctb = _project(
        x2, w_in[0], w_uk[0], kv_norm_g[0], lb_logits, bsz, seq)
    za = _dsa(iqt, iwt, ik, c, ctb, qlt, bias, w_uv[0], ga, bsz, seq)
    zb = _hgrn(qs, logf, v, gb, hgrn_norm_g[0], bsz, seq)
    out = _output(za.reshape(t, A_WIDTH), zb.reshape(t, B_WIDTH), x2, w_o[0], ln_g[0], ln_b[0])
    return out.reshape(bsz, seq, d)
```

```python
import functools
import math

import jax
import jax.numpy as jnp
from jax import lax
from jax.experimental import pallas as pl
from jax.experimental.pallas import tpu as pltpu

F32 = jnp.float32
BF16 = jnp.bfloat16
I32 = jnp.int32
I16 = jnp.int16

D_MODEL = 1024
DEPTH = 1
CHUNK = 64
Q_BLOCK = 128
A_HEADS = 4
A_HEAD_DIM = 128
A_WIDTH = A_HEADS * A_HEAD_DIM
A_KV_RANK = 128
IDX_HEADS = 8
IDX_DIM = 64
TOPK_MAX = 256
B_HEADS = 4
B_KEY_DIM = 128
B_VAL_DIM = 128
B_WIDTH = B_HEADS * B_VAL_DIM
B_FORGET = B_HEADS * B_KEY_DIM
REL_BUCKETS = 32
REL_MAX_DIST = 256
DEEPNORM_ALPHA = (2.0 * DEPTH) ** 0.25
LN_EPS = 1e-5
RMS_EPS = 1e-6

_SPLITS = (
    ('a_q', A_WIDTH), ('a_ckv', A_KV_RANK), ('a_iq', IDX_HEADS * IDX_DIM), ('a_ik', IDX_DIM),
    ('a_iw', IDX_HEADS), ('a_gate', A_WIDTH), ('b_q', B_FORGET), ('b_f', B_FORGET),
    ('b_i', B_WIDTH), ('b_gate', B_WIDTH),
)
_OFFSETS = {}
_off = 0
for _name, _n in _SPLITS:
    _OFFSETS[_name] = (_off, _n)
    _off += _n

LANES = 128
QW = 256
KG = 256
DSA_NB = 4
I16_ROWS = 16
CT_ROWS = A_KV_RANK + I16_ROWS
SUB = 16
HGRN_STEPS = 512
HGRN_NB = 2
EXP_CAP = 100.0
NEG_BIG = -1e30
LOG2_E = math.log2(math.e)
INT_MIN = -2 ** 31
PROJ_TILE = 1024
OUT_TILE = 2048
VMEM_LIMIT = 56 * 1024 * 1024


def _col(w, name):
    o, n = _OFFSETS[name]
    return w[:, o:o + n]


def _t5_bucket(rel):
    nb = REL_BUCKETS // 2
    max_exact = nb // 2
    ret = jnp.where(rel > 0, nb, 0).astype(I32)
    n = jnp.abs(rel)
    nf = jnp.maximum(n, max_exact).astype(F32)
    large = max_exact + jnp.floor(jnp.log(nf / max_exact) / math.log(REL_MAX_DIST / max_exact)
                                  * (nb - max_exact)).astype(I32)
    large = jnp.minimum(large, nb - 1)
    return ret + jnp.where(n < max_exact, n, large)


def _bias_kernel(rb_ref, out_ref):
    sl = lax.broadcasted_iota(I32, (KG, QW), 0)
    ql = lax.broadcasted_iota(I32, (KG, QW), 1)
    far = REL_BUCKETS // 2 - 1
    for d in range(2):
        bucket = _t5_bucket(sl - ql - KG * d)
        for h in range(A_HEADS):
            acc = jnp.zeros((KG, QW), F32)
            for bk in range(REL_BUCKETS):
                acc = jnp.where(bucket == bk, (rb_ref[bk, h] - rb_ref[far, h]) * LOG2_E, acc)
            out_ref[d, :, h * QW:(h + 1) * QW] = acc


def _bias_tiles(rel_bias):
    assert KG == QW and KG >= REL_MAX_DIST
    return pl.pallas_call(
        _bias_kernel,
        out_shape=jax.ShapeDtypeStruct((2, KG, A_HEADS * QW), F32),
        in_specs=[pl.BlockSpec(memory_space=pltpu.SMEM)],
        out_specs=pl.BlockSpec(memory_space=pltpu.VMEM),
        name="bias",
    )(rel_bias.astype(F32))


def _proj_kernel(x_ref, wrow_ref, wt_ref, wukt_ref, g_ref, lbl_ref,
                 ik_ref, c_ref, ga_ref, gb_ref, qs_ref, logf_ref, v_ref,
                 iqt_ref, iwt_ref, qlt_ref, ct_ref):
    tq = x_ref.shape[0]
    xb = x_ref[...].astype(BF16)

    def rowdot(lo, n):
        return jnp.dot(xb, wrow_ref[:, lo:lo + n], preferred_element_type=F32)

    h0 = rowdot(0, 256)
    a = h0[:, :A_KV_RANK]
    ms = jnp.mean(a * a, axis=-1, keepdims=True)
    c = a * lax.rsqrt(ms + RMS_EPS) * g_ref[...]
    c_ref[...] = c.astype(BF16)
    ct = c.T
    ones = jnp.ones((CT_ROWS - A_KV_RANK, KG), BF16)
    for i in range(tq // KG):
        ct_ref[0, i] = jnp.concatenate([ct[:, i * KG:(i + 1) * KG].astype(BF16), ones], axis=0)
    ik_ref[...] = h0[:, A_KV_RANK:A_KV_RANK + IDX_DIM].astype(BF16)

    ga_ref[...] = jax.nn.silu(rowdot(256, A_WIDTH)).astype(BF16)
    qs_ref[...] = jax.nn.silu(rowdot(768, B_FORGET)).astype(BF16)
    lbl = lbl_ref[...]
    e = jnp.exp(lbl - jnp.max(lbl, axis=0, keepdims=True))
    lb = e[0:1, :] / jnp.sum(e, axis=0, keepdims=True)
    f = lb + (1.0 - lb) * jax.nn.sigmoid(rowdot(1280, B_FORGET))
    logf_ref[...] = jnp.log2(f)
    v_ref[...] = rowdot(1792, B_WIDTH).astype(BF16)
    gb_ref[...] = jax.nn.silu(rowdot(2304, B_WIDTH)).astype(BF16)

    ht = lax.dot_general(wt_ref[...], xb, (((1,), (1,)), ((), ())), preferred_element_type=F32)
    iqt_ref[0] = ht[A_WIDTH:A_WIDTH + IDX_HEADS * IDX_DIM].astype(BF16)
    o = A_WIDTH + IDX_HEADS * IDX_DIM
    iwt_ref[0] = ht[o:o + IDX_HEADS] * (IDX_HEADS ** -0.5 * IDX_DIM ** -0.5)
    for h in range(A_HEADS):
        aq = ht[h * A_HEAD_DIM:(h + 1) * A_HEAD_DIM].astype(BF16)
        ql = jnp.dot(wukt_ref[h], aq, preferred_element_type=F32) * (A_HEAD_DIM ** -0.5 * LOG2_E)
        qlt_ref[0, h] = ql.astype(BF16)


def _project(x2, w_in, w_uk, kv_g, lb_logits, bsz, seq):
    t = bsz * seq
    tq = min(PROJ_TILE, seq)
    per_b = seq // tq
    pad64 = jnp.zeros((D_MODEL, 64), F32)
    wrow = jnp.concatenate([_col(w_in, 'a_ckv'), _col(w_in, 'a_ik'), pad64, _col(w_in, 'a_gate'),
                            _col(w_in, 'b_q'), _col(w_in, 'b_f'), _col(w_in, 'b_i'),
                            _col(w_in, 'b_gate')], axis=1).astype(BF16)
    pad8 = jnp.zeros((D_MODEL, 8), F32)
    wt = jnp.concatenate([_col(w_in, 'a_q'), _col(w_in, 'a_iq'), _col(w_in, 'a_iw'), pad8],
                         axis=1).T.astype(BF16)
    wukt = jnp.swapaxes(w_uk, 1, 2).astype(BF16)
    nrow, nt = wrow.shape[1], wt.shape[0]
    row = lambda n: pl.BlockSpec((tq, n), lambda i: (i, 0))
    full = lambda shape: pl.BlockSpec(shape, lambda i: (0,) * len(shape))
    out_shape = (
        jax.ShapeDtypeStruct((t, IDX_DIM), BF16),
        jax.ShapeDtypeStruct((t, A_KV_RANK), BF16),
        jax.ShapeDtypeStruct((t, A_WIDTH), BF16),
        jax.ShapeDtypeStruct((t, B_WIDTH), BF16),
        jax.ShapeDtypeStruct((t, B_FORGET), BF16),
        jax.ShapeDtypeStruct((t, B_FORGET), F32),
        jax.ShapeDtypeStruct((t, B_WIDTH), BF16),
        jax.ShapeDtypeStruct((bsz, IDX_HEADS * IDX_DIM, seq), BF16),
        jax.ShapeDtypeStruct((bsz, IDX_HEADS, seq), F32),
        jax.ShapeDtypeStruct((bsz, A_HEADS, A_KV_RANK, seq), BF16),
        jax.ShapeDtypeStruct((bsz, seq // KG, CT_ROWS, KG), BF16),
    )
    out_specs = (
        row(IDX_DIM), row(A_KV_RANK), row(A_WIDTH), row(B_WIDTH), row(B_FORGET), row(B_FORGET), row(B_WIDTH),
        pl.BlockSpec((1, IDX_HEADS * IDX_DIM, tq), lambda i: (i // per_b, 0, i % per_b)),
        pl.BlockSpec((1, IDX_HEADS, tq), lambda i: (i // per_b, 0, i % per_b)),
        pl.BlockSpec((1, A_HEADS, A_KV_RANK, tq), lambda i: (i // per_b, 0, 0, i % per_b)),
        pl.BlockSpec((1, tq // KG, CT_ROWS, KG), lambda i: (i // per_b, i % per_b, 0, 0)),
    )
    return pl.pallas_call(
        _proj_kernel,
        grid=(t // tq,),
        in_specs=[row(D_MODEL), full((D_MODEL, nrow)), full((nt, D_MODEL)),
                  full((A_HEADS, A_KV_RANK, A_HEAD_DIM)), full((1, A_KV_RANK)), full((DEPTH + 1, B_FORGET))],
        out_specs=out_specs,
        out_shape=out_shape,
        compiler_params=pltpu.CompilerParams(dimension_semantics=("parallel",), vmem_limit_bytes=VMEM_LIMIT),
        name="proj",
    )(x2, wrow, wt, wukt, kv_g.reshape(1, A_KV_RANK).astype(F32), lb_logits.astype(F32))


def _chunk_of(pos):
    return lax.shift_right_logical(pos, int(math.log2(CHUNK)))


def _tree_sum(parts):
    while len(parts) > 1:
        parts = [a + b for a, b in zip(parts[::2], parts[1::2])] + ([parts[-1]] if len(parts) % 2 else [])
    return parts[0]


def _order_key(bits):
    return jnp.where(bits < 0, jnp.int32(INT_MIN) - bits, bits)


def _dsa_kernel(iqt_ref, iwt_ref, ik_ref, c_ref, ct_ref, qlt_ref, bias_ref, wuv_ref, ga_ref, o_ref,
                sc_ref, hi_ref, lo_ref, acc_ref, m_ref, thr_ref, nge_ref, ngt_ref, *, k_sel, nb):
    j = pl.program_id(1)
    ngrp = j + 1
    hq = A_HEADS * QW
    batches = range(nb)
    row_iota = lax.broadcasted_iota(I32, (KG, QW), 0)
    lane_iota = lax.broadcasted_iota(I32, (KG, QW), 1)
    q_chunk = _chunk_of(j * QW + lane_iota)

    def rows_of(g):
        return pl.ds(pl.multiple_of(g * KG, KG), KG)

    def score_group(g, diagonal):
        rows = rows_of(g)
        for bi in batches:
            rhs = jnp.concatenate([iqt_ref[bi, h * IDX_DIM:(h + 1) * IDX_DIM, :] for h in range(IDX_HEADS)],
                                  axis=1)
            d = jnp.dot(ik_ref[bi, rows, :], rhs, preferred_element_type=F32)
            sc = iwt_ref[bi, 0:1, :] * jnp.maximum(d[:, :QW], 0.0)
            for h in range(1, IDX_HEADS):
                sc = sc + iwt_ref[bi, h:h + 1, :] * jnp.maximum(d[:, h * QW:(h + 1) * QW], 0.0)
            if diagonal:
                sc = jnp.where(_chunk_of(g * KG + row_iota) <= q_chunk, sc, -jnp.inf)
            sc_ref[bi, rows, :] = sc
            key = _order_key(lax.bitcast_convert_type(sc, I32))
            hi_ref[bi, rows, :] = (key >> 16).astype(I16)
            lo_ref[bi, rows, :] = (key ^ 0x8000).astype(I16)

    def score_inner(g, carry):
        score_group(g, False)
        return carry

    lax.fori_loop(0, j, score_inner, 0)
    score_group(j, True)

    def count16(ref, cands):
        c16 = [c.astype(I16) for c in cands]

        def one(g, accs):
            rows = rows_of(g)
            out = []
            for bi in batches:
                blk = ref[bi, rows, :]
                ind = jnp.where(blk >= c16[bi], jnp.int16(1), jnp.int16(0))
                out.append(accs[bi] + _tree_sum([ind[r:r + I16_ROWS] for r in range(0, KG, I16_ROWS)]))
            return tuple(out)

        def two(p, accs):
            return one(2 * p + 1, one(2 * p, accs))

        accs = tuple(jnp.zeros((I16_ROWS, QW), I16) for _ in batches)
        accs = lax.fori_loop(0, ngrp // 2, two, accs)
        accs = lax.fori_loop(2 * (ngrp // 2), ngrp, one, accs)
        return [jnp.sum(a.astype(I32), axis=0, keepdims=True) for a in accs]

    def bisect16(ref, want, state, first_bit, last_bit):
        def step(i, carry):
            ts, n_ge, n_gt = carry
            cands = [t + lax.shift_left(jnp.int32(1), 15 - i) for t in ts]
            cnts = count16(ref, cands)
            ok = [cnts[bi] >= want[bi] for bi in batches]
            return (tuple(jnp.where(ok[bi], cands[bi], ts[bi]) for bi in batches),
                    tuple(jnp.where(ok[bi], cnts[bi], n_ge[bi]) for bi in batches),
                    tuple(jnp.where(ok[bi], n_gt[bi], cnts[bi]) for bi in batches))
        return lax.fori_loop(first_bit, last_bit, step, state)

    def start16(n_all):
        return (tuple(jnp.full((1, QW), -32768, I32) for _ in batches), tuple(n_all),
                tuple(jnp.zeros((1, QW), I32) for _ in batches))

    want_hi = [jnp.full((1, QW), k_sel, I32) for _ in batches]
    n_rows = [jnp.zeros((1, QW), I32) + ngrp * KG for _ in batches]
    p_hi, n_ge_hi, n_gt_hi = bisect16(hi_ref, want_hi, start16(n_rows), 0, 16)

    def keep_ties(g, carry):
        rows = rows_of(g)
        for bi in batches:
            lo_ref[bi, rows, :] = jnp.where(hi_ref[bi, rows, :] == p_hi[bi].astype(I16), lo_ref[bi, rows, :],
                                            jnp.int16(-32768))
        return carry

    lax.fori_loop(0, ngrp, keep_ties, 0)
    want_lo = [k_sel - n_gt_hi[bi] for bi in batches]
    n_eq_hi = [n_ge_hi[bi] - n_gt_hi[bi] for bi in batches]
    p_lo, n_ge_lo, n_gt_lo = bisect16(lo_ref, want_lo, start16(n_eq_hi), 0, 16)
    for bi in batches:
        thr_ref[bi] = lax.shift_left(p_hi[bi], 16) + (p_lo[bi] + 32768)
        nge_ref[bi] = n_gt_hi[bi] + n_ge_lo[bi]
        ngt_ref[bi] = n_gt_hi[bi] + n_gt_lo[bi]

    def as_float(key):
        return lax.bitcast_convert_type(_order_key(key), F32)

    def count32(cands):
        def body(g, accs):
            rows = rows_of(g)
            out = []
            for bi in batches:
                ind = jnp.where(sc_ref[bi, rows, :] >= cands[bi], 1, 0)
                out.append(accs[bi] + jnp.sum(ind.reshape(KG // 8, 8, QW), axis=0))
            return tuple(out)

        accs = lax.fori_loop(0, ngrp, body, tuple(jnp.zeros((8, QW), I32) for _ in batches))
        return [jnp.sum(a, axis=0, keepdims=True) for a in accs]

    n_ge_f = count32([as_float(thr_ref[bi]) for bi in batches])
    differs = _tree_sum([jnp.where(n_ge_f[bi] != nge_ref[bi], 1, 0) for bi in batches])

    @pl.when(jnp.max(differs) > 0)
    def _():
        def step(i, carry):
            ts, n_ge, n_gt = carry
            cands = [t + lax.shift_left(jnp.int32(1), 31 - i) for t in ts]
            cnts = count32([as_float(c) for c in cands])
            ok = [cnts[bi] >= k_sel for bi in batches]
            return (tuple(jnp.where(ok[bi], cands[bi], ts[bi]) for bi in batches),
                    tuple(jnp.where(ok[bi], cnts[bi], n_ge[bi]) for bi in batches),
                    tuple(jnp.where(ok[bi], n_gt[bi], cnts[bi]) for bi in batches))
        init = (tuple(jnp.full((1, QW), INT_MIN, I32) for _ in batches), tuple(n_rows),
                tuple(jnp.zeros((1, QW), I32) for _ in batches))
        ts, n_ge, n_gt = lax.fori_loop(0, 32, step, init)
        for bi in batches:
            thr_ref[bi] = ts[bi]
            nge_ref[bi] = n_ge[bi]
            ngt_ref[bi] = n_gt[bi]

    thr = [as_float(thr_ref[bi]) for bi in batches]

    excess = _tree_sum([jnp.where((nge_ref[bi] > k_sel) & (thr[bi] > -jnp.inf), 1, 0) for bi in batches])

    @pl.when(jnp.max(excess) > 0)
    def _():
        tri = jnp.where(row_iota >= lane_iota, 1.0, 0.0).astype(BF16)
        need = [(k_sel - ngt_ref[bi]).astype(F32) for bi in batches]

        def demote(g, carry):
            rows = rows_of(g)
            out = []
            for bi in batches:
                blk = sc_ref[bi, rows, :]
                eq = jnp.where(blk == thr[bi], 1.0, 0.0)
                incl = jnp.dot(tri, eq.astype(BF16), preferred_element_type=F32)
                rank = incl - eq + carry[bi]
                sc_ref[bi, rows, :] = jnp.where((blk == thr[bi]) & (rank >= need[bi]), -jnp.inf, blk)
                out.append(carry[bi] + incl[KG - 1:KG, :])
            return tuple(out)

        lax.fori_loop(0, ngrp, demote, tuple(jnp.zeros((1, QW), F32) for _ in batches))

    thr_sel = [jnp.maximum(t, jnp.finfo(F32).min) for t in thr]
    for bi in batches:
        m_ref[bi] = jnp.full((1, hq), NEG_BIG, F32)
        acc_ref[bi] = jnp.zeros((CT_ROWS, hq), F32)

    def attend(g, with_bias):
        half = KG // 2
        chains = [(bi, h) for bi in batches for h in range(A_HEADS)]
        sel_h, c_h = {}, {}
        for bi in batches:
            for u in range(2):
                rows = pl.ds(pl.multiple_of(g * KG + u * half, half), half)
                sel_h[bi, u] = jnp.where(sc_ref[bi, rows, :] >= thr_sel[bi], 0.0, NEG_BIG)
                c_h[bi, u] = c_ref[bi, rows, :]
        s_all = {}
        for bi, h in chains:
            for u in range(2):
                s_all[bi, h, u] = jnp.dot(c_h[bi, u], qlt_ref[bi, h], preferred_element_type=F32)
        p_all = {}
        alpha_all = {}
        for bi, h in chains:
            cols = slice(h * QW, (h + 1) * QW)
            m_loc, p_loc = [], []
            for u in range(2):
                s = s_all[bi, h, u]
                if with_bias:
                    s = s + bias_ref[j - g, u * half:(u + 1) * half, cols]
                s = s + sel_h[bi, u]
                m_u = jnp.max(s, axis=0, keepdims=True)
                m_loc.append(m_u)
                p_loc.append(jnp.exp2(s - m_u).astype(BF16))
            m_old = m_ref[bi, :, cols]
            m_new = jnp.maximum(m_old, jnp.maximum(m_loc[0], m_loc[1]))
            alpha_all[bi, h] = jnp.exp2(m_old - m_new)
            p_all[bi, h] = jnp.concatenate([p_loc[u] * jnp.exp2(m_loc[u] - m_new).astype(BF16) for u in range(2)],
                                           axis=0)
            m_ref[bi, :, cols] = m_new
        for bi, h in chains:
            cols = slice(h * QW, (h + 1) * QW)
            pv = jnp.dot(ct_ref[bi, g], p_all[bi, h], preferred_element_type=F32)
            acc_ref[bi, :, cols] = acc_ref[bi, :, cols] * alpha_all[bi, h] + pv

    def attend_far(g, carry):
        attend(g, False)
        return carry

    lax.fori_loop(0, j - 1, attend_far, 0)

    @pl.when(j >= 1)
    def _():
        attend(j - 1, True)

    attend(j, True)

    for bi in batches:
        ot = acc_ref[bi, :A_KV_RANK, :] * (1.0 / acc_ref[bi, A_KV_RANK:A_KV_RANK + 1, :])
        for h in range(A_HEADS):
            o_h = ot[:, h * QW:(h + 1) * QW].T
            cols = slice(h * A_HEAD_DIM, (h + 1) * A_HEAD_DIM)
            oa = jnp.dot(o_h.astype(BF16), wuv_ref[h], preferred_element_type=F32)
            o_ref[bi, :, cols] = (oa * ga_ref[bi, :, cols].astype(F32)).astype(BF16)


def _dsa(iqt, iwt, ik, c, ctb, qlt, bias, w_uv, ga, bsz, seq):
    assert seq % QW == 0 and QW % Q_BLOCK == 0
    nb = DSA_NB if bsz % DSA_NB == 0 else 1
    k_sel = min(TOPK_MAX, seq // 4)
    hq = A_HEADS * QW
    return pl.pallas_call(
        functools.partial(_dsa_kernel, k_sel=k_sel, nb=nb),
        grid=(bsz // nb, seq // QW),
        in_specs=[
            pl.BlockSpec((nb, IDX_HEADS * IDX_DIM, QW), lambda b, j: (b, 0, j)),
            pl.BlockSpec((nb, IDX_HEADS, QW), lambda b, j: (b, 0, j)),
            pl.BlockSpec((nb, seq, IDX_DIM), lambda b, j: (b, 0, 0)),
            pl.BlockSpec((nb, seq, A_KV_RANK), lambda b, j: (b, 0, 0)),
            pl.BlockSpec((nb, seq // KG, CT_ROWS, KG), lambda b, j: (b, 0, 0, 0)),
            pl.BlockSpec((nb, A_HEADS, A_KV_RANK, QW), lambda b, j: (b, 0, 0, j)),
            pl.BlockSpec((2, KG, hq), lambda b, j: (0, 0, 0)),
            pl.BlockSpec((A_HEADS, A_KV_RANK, A_HEAD_DIM), lambda b, j: (0, 0, 0)),
            pl.BlockSpec((nb, QW, A_WIDTH), lambda b, j: (b, j, 0)),
        ],
        out_specs=pl.BlockSpec((nb, QW, A_WIDTH), lambda b, j: (b, j, 0)),
        out_shape=jax.ShapeDtypeStruct((bsz, seq, A_WIDTH), BF16),
        scratch_shapes=[
            pltpu.VMEM((nb, seq, QW), F32),
            pltpu.VMEM((nb, seq, QW), I16),
            pltpu.VMEM((nb, seq, QW), I16),
            pltpu.VMEM((nb, CT_ROWS, hq), F32),
            pltpu.VMEM((nb, 1, hq), F32),
            pltpu.VMEM((nb, 1, QW), I32),
            pltpu.VMEM((nb, 1, QW), I32),
            pltpu.VMEM((nb, 1, QW), I32),
        ],
        compiler_params=pltpu.CompilerParams(dimension_semantics=("parallel", "arbitrary"),
                                             vmem_limit_bytes=VMEM_LIMIT),
        name="dsa",
    )(iqt, iwt, ik.reshape(bsz, seq, IDX_DIM), c.reshape(bsz, seq, A_KV_RANK), ctb, qlt, bias,
      w_uv.astype(BF16), ga.reshape(bsz, seq, A_WIDTH))


def _hgrn_kernel(qs_ref, logf_ref, v_ref, gb_ref, hg_ref, o_ref, st_ref, bq_ref, oi_ref, *, nb):
    @pl.when(pl.program_id(1) == 0)
    def _():
        st_ref[...] = jnp.zeros(st_ref.shape, F32)

    steps = qs_ref.shape[1]
    nchunk = steps // CHUNK
    dk, dv = B_KEY_DIM, B_VAL_DIM
    r64 = lax.broadcasted_iota(I32, (CHUNK, CHUNK), 0)
    c64 = lax.broadcasted_iota(I32, (CHUNK, CHUNK), 1)
    tri = jnp.where(r64 >= c64, 1.0, 0.0).astype(BF16)
    rt = lax.broadcasted_iota(I32, (CHUNK, LANES), 0)
    cs = lax.broadcasted_iota(I32, (CHUNK, LANES), 1)
    causal = cs <= rt
    nsub = CHUNK // SUB
    zpad_f = jnp.zeros((LANES - CHUNK, LANES), F32)
    nt = (((1,), (1,)), ((), ()))
    units = [(bi, ci, h) for bi in range(nb) for ci in range(nchunk) for h in range(B_HEADS)]

    span = jnp.zeros((1, B_FORGET), F32)
    for bi in range(nb):
        for r0 in range(0, steps, SUB):
            span = jnp.maximum(span, -jnp.sum(logf_ref[bi, r0:r0 + SUB, :], axis=0, keepdims=True))
    safe = jnp.max(span) <= EXP_CAP
    chains = [(bi, h) for bi in range(nb) for h in range(B_HEADS)]
    lf_all, b_all = {}, {}

    def cum_decay():
        for bi in range(nb):
            for ci in range(nchunk):
                lf = logf_ref[bi, ci * CHUNK:(ci + 1) * CHUNK, :]
                hi = lf.astype(BF16)
                lo = (lf - hi.astype(F32)).astype(BF16)
                lf_all[bi, ci] = lf
                b_all[bi, ci] = (jnp.dot(tri, hi, preferred_element_type=F32)
                                 + jnp.dot(tri, lo, preferred_element_type=F32))

    def finish(o, bi, rows, h):
        cols = slice(h * dv, (h + 1) * dv)
        ms = jnp.mean(o * o, axis=-1, keepdims=True)
        z = o * lax.rsqrt(ms + RMS_EPS) * hg_ref[:, cols] * gb_ref[bi, rows, cols].astype(F32)
        o_ref[bi, rows, cols] = z.astype(BF16)

    def state_operands(u):
        bi, ci, h = u
        rows = slice(ci * CHUNK, (ci + 1) * CHUNK)
        cols = slice(h * dk, (h + 1) * dk)
        b = b_all[bi, ci][:, cols]
        kk = 1.0 - jnp.exp2(lf_all[bi, ci][:, cols])
        b_last = b[CHUNK - 1:CHUNK, :]
        k_hat = (kk * jnp.exp2(b_last - b)).astype(BF16)
        k_pad = jnp.concatenate([k_hat, jnp.zeros((LANES - CHUNK, dk), BF16)], axis=0)
        v = v_ref[bi, rows, h * dv:(h + 1) * dv]
        v_t = jnp.concatenate([v.astype(F32), zpad_f], axis=0).T.astype(BF16)
        return b, kk, k_pad, v_t, jnp.exp2(b_last)

    @pl.when(safe)
    def _():
        cum_decay()
        q_in, q_st, kst, k_pad, v_t, e_last = {}, {}, {}, {}, {}, {}
        r_all, lhs, upd = {}, {}, {}

        def operands(u):
            bi, ci, h = u
            b, kk, k_pad[u], v_t[u], e_last[u] = state_operands(u)
            q = qs_ref[bi, ci * CHUNK:(ci + 1) * CHUNK, h * dk:(h + 1) * dk].astype(F32)
            starts = [jnp.zeros((1, dk), F32)] + [b[i * SUB - 1:i * SUB, :] for i in range(1, nsub)]
            ref_rows = jnp.concatenate([jnp.broadcast_to(s_, (SUB, dk)) for s_ in starts], axis=0)
            qi = q * jnp.exp2(b - ref_rows)
            q_in[u] = qi.astype(BF16)
            q_st[u] = (qi * jnp.exp2(ref_rows)).astype(BF16)
            secs = []
            for i in range(nsub):
                n = SUB * (i + 1)
                ks = (kk[:n, :] * jnp.exp2(starts[i] - b[:n, :])).astype(BF16)
                secs.append(ks if n == CHUNK else jnp.concatenate([ks, jnp.zeros((CHUNK - n, dk), BF16)], axis=0))
            kst[u] = jnp.concatenate(secs, axis=0)

        def matmuls(u):
            r_all[u] = lax.dot_general(q_in[u], kst[u], nt, preferred_element_type=F32)
            upd[u] = jnp.dot(v_t[u], k_pad[u], preferred_element_type=F32)

        def scores(u):
            r = r_all[u]
            slabs = []
            for i in range(nsub):
                lane0 = (i * CHUNK // LANES) * LANES
                slab = r[i * SUB:(i + 1) * SUB, lane0:lane0 + LANES]
                if (i * CHUNK) % LANES:
                    slab = pltpu.roll(slab, LANES - (i * CHUNK) % LANES, 1)
                slabs.append(slab)
            a = jnp.concatenate(slabs, axis=0)
            a = jnp.where(causal, a, 0.0).astype(BF16)
            lhs[u] = jnp.concatenate([a, q_st[u]], axis=1)

        for u in units:
            operands(u)
        for u in units:
            matmuls(u)
            scores(u)

        st = {ch: st_ref[ch[0], ch[1]] for ch in chains}
        for ci in range(nchunk):
            for bi, h in chains:
                u = (bi, ci, h)
                w = jnp.concatenate([v_t[u], st[bi, h].astype(BF16)], axis=1)
                o = lax.dot_general(lhs[u], w, nt, preferred_element_type=F32)
                finish(o, bi, slice(ci * CHUNK, (ci + 1) * CHUNK), h)
            for bi, h in chains:
                st[bi, h] = st[bi, h] * e_last[bi, ci, h] + upd[bi, ci, h]
        for bi, h in chains:
            st_ref[bi, h] = st[bi, h]

    @pl.when(jnp.logical_not(safe))
    def _():
        cum_decay()
        for bi in range(nb):
            for ci in range(nchunk):
                rows = slice(ci * CHUNK, (ci + 1) * CHUNK)
                bq_ref[0, bi, rows, :] = b_all[bi, ci]
                bq_ref[1, bi, rows, :] = qs_ref[bi, rows, :].astype(F32)
        row_id = lax.broadcasted_iota(I32, (CHUNK, 1), 0)
        for bi in range(nb):
            for ci in range(nchunk):
                rows = slice(ci * CHUNK, (ci + 1) * CHUNK)
                b = b_all[bi, ci]
                kk = 1.0 - jnp.exp2(lf_all[bi, ci])
                v = v_ref[bi, rows, :].astype(F32)

                def row(t, carry, bi=bi, ci=ci, b=b, kk=kk, v=v):
                    b_t = bq_ref[0, bi, pl.ds(ci * CHUNK + t, 1), :]
                    q_t = bq_ref[1, bi, pl.ds(ci * CHUNK + t, 1), :]
                    w = q_t * kk * jnp.exp2(jnp.minimum(b_t - b, 0.0))
                    outs = []
                    for h in range(B_HEADS):
                        s_h = jnp.sum(w[:, h * dk:(h + 1) * dk], axis=1, keepdims=True)
                        s_h = jnp.where(row_id <= t, s_h, 0.0)
                        outs.append(jnp.sum(s_h * v[:, h * dv:(h + 1) * dv], axis=0, keepdims=True))
                    oi_ref[bi, pl.ds(ci * CHUNK + t, 1), :] = jnp.concatenate(outs, axis=1)
                    return carry

                lax.fori_loop(0, CHUNK, row, 0)
        st = {ch: st_ref[ch[0], ch[1]] for ch in chains}
        for ci in range(nchunk):
            rows = slice(ci * CHUNK, (ci + 1) * CHUNK)
            for bi, h in chains:
                b, kk, k_pad, v_t, e_last = state_operands((bi, ci, h))
                q_st = (qs_ref[bi, rows, h * dk:(h + 1) * dk].astype(F32) * jnp.exp2(b)).astype(BF16)
                o = oi_ref[bi, rows, h * dv:(h + 1) * dv] + lax.dot_general(
                    q_st, st[bi, h].astype(BF16), nt, preferred_element_type=F32)
                finish(o, bi, rows, h)
                st[bi, h] = st[bi, h] * e_last + jnp.dot(v_t, k_pad, preferred_element_type=F32)
        for bi, h in chains:
            st_ref[bi, h] = st[bi, h]


def _hgrn(qs, logf, v, gb, hg, bsz, seq):
    steps = min(HGRN_STEPS, seq)
    nb = HGRN_NB if bsz % HGRN_NB == 0 else 1
    spec = lambda n: pl.BlockSpec((nb, steps, n), lambda b, s: (b, s, 0))
    return pl.pallas_call(
        functools.partial(_hgrn_kernel, nb=nb),
        grid=(bsz // nb, seq // steps),
        in_specs=[spec(B_FORGET), spec(B_FORGET), spec(B_WIDTH), spec(B_WIDTH),
                  pl.BlockSpec((1, B_WIDTH), lambda b, s: (0, 0))],
        out_specs=spec(B_WIDTH),
        out_shape=jax.ShapeDtypeStruct((bsz, seq, B_WIDTH), BF16),
        scratch_shapes=[pltpu.VMEM((nb, B_HEADS, B_VAL_DIM, B_KEY_DIM), F32),
                        pltpu.VMEM((2, nb, steps, B_FORGET), F32),
                        pltpu.VMEM((nb, steps, B_WIDTH), F32)],
        compiler_params=pltpu.CompilerParams(dimension_semantics=("parallel", "arbitrary"),
                                             vmem_limit_bytes=VMEM_LIMIT),
        name="hgrn",
    )(qs.reshape(bsz, seq, B_FORGET), logf.reshape(bsz, seq, B_FORGET), v.reshape(bsz, seq, B_WIDTH),
      gb.reshape(bsz, seq, B_WIDTH), hg.reshape(1, B_WIDTH).astype(F32))


def _out_kernel(za_ref, zb_ref, x_ref, wo_ref, lng_ref, lnb_ref, out_ref):
    z = jnp.concatenate([za_ref[...], zb_ref[...]], axis=1)
    y = jnp.dot(z, wo_ref[...], preferred_element_type=F32)
    r = DEEPNORM_ALPHA * x_ref[...] + y
    mu = jnp.mean(r, axis=-1, keepdims=True)
    rc = r - mu
    var = jnp.mean(rc * rc, axis=-1, keepdims=True)
    out_ref[...] = rc * lax.rsqrt(var + LN_EPS) * lng_ref[...] + lnb_ref[...]


def _output(za, zb, x2, w_o, ln_g, ln_b):
    t = x2.shape[0]
    tq = min(OUT_TILE, t)
    row = lambda n: pl.BlockSpec((tq, n), lambda i: (i, 0))
    full = lambda shape: pl.BlockSpec(shape, lambda i: (0,) * len(shape))
    d_mix = A_WIDTH + B_WIDTH
    return pl.pallas_call(
        _out_kernel,
        grid=(t // tq,),
        in_specs=[row(A_WIDTH), row(B_WIDTH), row(D_MODEL),
                  full((d_mix, D_MODEL)), full((1, D_MODEL)), full((1, D_MODEL))],
        out_specs=row(D_MODEL),
        out_shape=jax.ShapeDtypeStruct((t, D_MODEL), F32),
        compiler_params=pltpu.CompilerParams(dimension_semantics=("parallel",), vmem_limit_bytes=VMEM_LIMIT),
        name="out",
    )(za, zb, x2, w_o.astype(BF16), ln_g.reshape(1, D_MODEL).astype(F32), ln_b.reshape(1, D_MODEL).astype(F32))


def kernel(x, w_in, w_uk, w_uv, kv_norm_g, rel_bias, lb_logits, hgrn_norm_g, w_o, ln_g, ln_b):
    bsz, seq, d = x.shape
    assert d == D_MODEL and seq % Q_BLOCK == 0 and DEPTH == 1
    t = bsz * seq
    x2 = x.reshape(t, d)
    bias = _bias_tiles(rel_bias)
    ik, c, ga, gb, qs, logf, v, iqt, iwt, qlt, ctb = _project(
        x2, w_in[0], w_uk[0], kv_norm_g[0], lb_logits, bsz, seq)
    za = _dsa(iqt, iwt, ik, c, ctb, qlt, bias, w_uv[0], ga, bsz, seq)
    zb = _hgrn(qs, logf, v, gb, hgrn_norm_g[0], bsz, seq)
    out = _output(za.reshape(t, A_WIDTH), zb.reshape(t, B_WIDTH), x2, w_o[0], ln_g[0], ln_b[0])
    return out.reshape(bsz, seq, d)
```

```python
import functools
import math

import jax
import jax.numpy as jnp
from jax import lax
from jax.experimental import pallas as pl
from jax.experimental.pallas import tpu as pltpu

F32 = jnp.float32
BF16 = jnp.bfloat16
I32 = jnp.int32
I16 = jnp.int16

D_MODEL = 1024
DEPTH = 1
CHUNK = 64
Q_BLOCK = 128
A_HEADS = 4
A_HEAD_DIM = 128
A_WIDTH = A_HEADS * A_HEAD_DIM
A_KV_RANK = 128
IDX_HEADS = 8
IDX_DIM = 64
TOPK_MAX = 256
B_HEADS = 4
B_KEY_DIM = 128
B_VAL_DIM = 128
B_WIDTH = B_HEADS * B_VAL_DIM
B_FORGET = B_HEADS * B_KEY_DIM
REL_BUCKETS = 32
REL_MAX_DIST = 256
DEEPNORM_ALPHA = (2.0 * DEPTH) ** 0.25
LN_EPS = 1e-5
RMS_EPS = 1e-6

_SPLITS = (
    ('a_q', A_WIDTH), ('a_ckv', A_KV_RANK), ('a_iq', IDX_HEADS * IDX_DIM), ('a_ik', IDX_DIM),
    ('a_iw', IDX_HEADS), ('a_gate', A_WIDTH), ('b_q', B_FORGET), ('b_f', B_FORGET),
    ('b_i', B_WIDTH), ('b_gate', B_WIDTH),
)
_OFFSETS = {}
_off = 0
for _name, _n in _SPLITS:
    _OFFSETS[_name] = (_off, _n)
    _off += _n

LANES = 128
QW = 256
KG = 256
DSA_NB = 4
I16_ROWS = 16
CT_ROWS = A_KV_RANK + I16_ROWS
SUB = 16
HGRN_STEPS = 512
HGRN_NB = 2
EXP_CAP = 100.0
NEG_BIG = -1e30
LOG2_E = math.log2(math.e)
INT_MIN = -2 ** 31
PROJ_TILE = 1024
OUT_TILE = 2048
OUT_SUB = 256
VMEM_LIMIT = 56 * 1024 * 1024


def _col(w, name):
    o, n = _OFFSETS[name]
    return w[:, o:o + n]


def _t5_bucket(rel):
    nb = REL_BUCKETS // 2
    max_exact = nb // 2
    ret = jnp.where(rel > 0, nb, 0).astype(I32)
    n = jnp.abs(rel)
    nf = jnp.maximum(n, max_exact).astype(F32)
    large = max_exact + jnp.floor(jnp.log(nf / max_exact) / math.log(REL_MAX_DIST / max_exact)
                                  * (nb - max_exact)).astype(I32)
    large = jnp.minimum(large, nb - 1)
    return ret + jnp.where(n < max_exact, n, large)


def _bias_kernel(rb_ref, out_ref):
    sl = lax.broadcasted_iota(I32, (KG, QW), 0)
    ql = lax.broadcasted_iota(I32, (KG, QW), 1)
    far = REL_BUCKETS // 2 - 1
    for d in range(2):
        bucket = _t5_bucket(sl - ql - KG * d)
        for h in range(A_HEADS):
            acc = jnp.zeros((KG, QW), F32)
            for bk in range(REL_BUCKETS):
                acc = jnp.where(bucket == bk, (rb_ref[bk, h] - rb_ref[far, h]) * LOG2_E, acc)
            out_ref[d, :, h * QW:(h + 1) * QW] = acc


def _bias_tiles(rel_bias):
    assert KG == QW and KG >= REL_MAX_DIST
    return pl.pallas_call(
        _bias_kernel,
        out_shape=jax.ShapeDtypeStruct((2, KG, A_HEADS * QW), F32),
        in_specs=[pl.BlockSpec(memory_space=pltpu.SMEM)],
        out_specs=pl.BlockSpec(memory_space=pltpu.VMEM),
        name="bias",
    )(rel_bias.astype(F32))


def _proj_kernel(x_ref, wrow_ref, wt_ref, wukt_ref, g_ref, lbl_ref,
                 ik_ref, c_ref, ga_ref, gb_ref, qs_ref, logf_ref, v_ref,
                 iqt_ref, iwt_ref, qlt_ref, ct_ref):
    tq = x_ref.shape[0]
    xb = x_ref[...].astype(BF16)

    def rowdot(lo, n):
        return jnp.dot(xb, wrow_ref[:, lo:lo + n], preferred_element_type=F32)

    h0 = rowdot(0, 256)
    a = h0[:, :A_KV_RANK]
    ms = jnp.mean(a * a, axis=-1, keepdims=True)
    c = a * lax.rsqrt(ms + RMS_EPS) * g_ref[...]
    c_ref[...] = c.astype(BF16)
    ct = c.T
    ones = jnp.ones((CT_ROWS - A_KV_RANK, KG), BF16)
    for i in range(tq // KG):
        ct_ref[0, i] = jnp.concatenate([ct[:, i * KG:(i + 1) * KG].astype(BF16), ones], axis=0)
    ik_ref[...] = h0[:, A_KV_RANK:A_KV_RANK + IDX_DIM].astype(BF16)

    ga_ref[...] = jax.nn.silu(rowdot(256, A_WIDTH)).astype(BF16)
    qs_ref[...] = jax.nn.silu(rowdot(768, B_FORGET)).astype(BF16)
    lbl = lbl_ref[...]
    e = jnp.exp(lbl - jnp.max(lbl, axis=0, keepdims=True))
    lb = e[0:1, :] / jnp.sum(e, axis=0, keepdims=True)
    f = lb + (1.0 - lb) * jax.nn.sigmoid(rowdot(1280, B_FORGET))
    logf_ref[...] = jnp.log2(f)
    v_ref[...] = rowdot(1792, B_WIDTH).astype(BF16)
    gb_ref[...] = jax.nn.silu(rowdot(2304, B_WIDTH)).astype(BF16)

    ht = lax.dot_general(wt_ref[...], xb, (((1,), (1,)), ((), ())), preferred_element_type=F32)
    iqt_ref[0] = ht[A_WIDTH:A_WIDTH + IDX_HEADS * IDX_DIM].astype(BF16)
    o = A_WIDTH + IDX_HEADS * IDX_DIM
    iwt_ref[0] = ht[o:o + IDX_HEADS] * (IDX_HEADS ** -0.5 * IDX_DIM ** -0.5)
    for h in range(A_HEADS):
        aq = ht[h * A_HEAD_DIM:(h + 1) * A_HEAD_DIM].astype(BF16)
        ql = jnp.dot(wukt_ref[h], aq, preferred_element_type=F32) * (A_HEAD_DIM ** -0.5 * LOG2_E)
        qlt_ref[0, h] = ql.astype(BF16)


def _project(x2, w_in, w_uk, kv_g, lb_logits, bsz, seq):
    t = bsz * seq
    tq = min(PROJ_TILE, seq)
    per_b = seq // tq
    pad64 = jnp.zeros((D_MODEL, 64), F32)
    wrow = jnp.concatenate([_col(w_in, 'a_ckv'), _col(w_in, 'a_ik'), pad64, _col(w_in, 'a_gate'),
                            _col(w_in, 'b_q'), _col(w_in, 'b_f'), _col(w_in, 'b_i'),
                            _col(w_in, 'b_gate')], axis=1).astype(BF16)
    pad8 = jnp.zeros((D_MODEL, 8), F32)
    wt = jnp.concatenate([_col(w_in, 'a_q'), _col(w_in, 'a_iq'), _col(w_in, 'a_iw'), pad8],
                         axis=1).T.astype(BF16)
    wukt = jnp.swapaxes(w_uk, 1, 2).astype(BF16)
    nrow, nt = wrow.shape[1], wt.shape[0]
    row = lambda n: pl.BlockSpec((tq, n), lambda i: (i, 0))
    full = lambda shape: pl.BlockSpec(shape, lambda i: (0,) * len(shape))
    out_shape = (
        jax.ShapeDtypeStruct((t, IDX_DIM), BF16),
        jax.ShapeDtypeStruct((t, A_KV_RANK), BF16),
        jax.ShapeDtypeStruct((t, A_WIDTH), BF16),
        jax.ShapeDtypeStruct((t, B_WIDTH), BF16),
        jax.ShapeDtypeStruct((t, B_FORGET), BF16),
        jax.ShapeDtypeStruct((t, B_FORGET), F32),
        jax.ShapeDtypeStruct((t, B_WIDTH), BF16),
        jax.ShapeDtypeStruct((bsz, IDX_HEADS * IDX_DIM, seq), BF16),
        jax.ShapeDtypeStruct((bsz, IDX_HEADS, seq), F32),
        jax.ShapeDtypeStruct((bsz, A_HEADS, A_KV_RANK, seq), BF16),
        jax.ShapeDtypeStruct((bsz, seq // KG, CT_ROWS, KG), BF16),
    )
    out_specs = (
        row(IDX_DIM), row(A_KV_RANK), row(A_WIDTH), row(B_WIDTH), row(B_FORGET), row(B_FORGET), row(B_WIDTH),
        pl.BlockSpec((1, IDX_HEADS * IDX_DIM, tq), lambda i: (i // per_b, 0, i % per_b)),
        pl.BlockSpec((1, IDX_HEADS, tq), lambda i: (i // per_b, 0, i % per_b)),
        pl.BlockSpec((1, A_HEADS, A_KV_RANK, tq), lambda i: (i // per_b, 0, 0, i % per_b)),
        pl.BlockSpec((1, tq // KG, CT_ROWS, KG), lambda i: (i // per_b, i % per_b, 0, 0)),
    )
    return pl.pallas_call(
        _proj_kernel,
        grid=(t // tq,),
        in_specs=[row(D_MODEL), full((D_MODEL, nrow)), full((nt, D_MODEL)),
                  full((A_HEADS, A_KV_RANK, A_HEAD_DIM)), full((1, A_KV_RANK)), full((DEPTH + 1, B_FORGET))],
        out_specs=out_specs,
        out_shape=out_shape,
        compiler_params=pltpu.CompilerParams(dimension_semantics=("parallel",), vmem_limit_bytes=VMEM_LIMIT),
        name="proj",
    )(x2, wrow, wt, wukt, kv_g.reshape(1, A_KV_RANK).astype(F32), lb_logits.astype(F32))


def _chunk_of(pos):
    return lax.shift_right_logical(pos, int(math.log2(CHUNK)))


def _tree_sum(parts):
    while len(parts) > 1:
        parts = [a + b for a, b in zip(parts[::2], parts[1::2])] + ([parts[-1]] if len(parts) % 2 else [])
    return parts[0]


def _order_key(bits):
    return jnp.where(bits < 0, jnp.int32(INT_MIN) - bits, bits)


def _dsa_kernel(iqt_ref, iwt_ref, ik_ref, c_ref, ct_ref, qlt_ref, bias_ref, wuv_ref, ga_ref, o_ref,
                sc_ref, hi_ref, lo_ref, acc_ref, m_ref, thr_ref, nge_ref, ngt_ref, *, k_sel, nb):
    j = pl.program_id(1)
    ngrp = j + 1
    hq = A_HEADS * QW
    batches = range(nb)
    row_iota = lax.broadcasted_iota(I32, (KG, QW), 0)
    lane_iota = lax.broadcasted_iota(I32, (KG, QW), 1)
    q_chunk = _chunk_of(j * QW + lane_iota)

    def rows_of(g):
        return pl.ds(pl.multiple_of(g * KG, KG), KG)

    def score_group(g, diagonal):
        rows = rows_of(g)
        for bi in batches:
            rhs = jnp.concatenate([iqt_ref[bi, h * IDX_DIM:(h + 1) * IDX_DIM, :] for h in range(IDX_HEADS)],
                                  axis=1)
            d = jnp.dot(ik_ref[bi, rows, :], rhs, preferred_element_type=F32)
            sc = iwt_ref[bi, 0:1, :] * jnp.maximum(d[:, :QW], 0.0)
            for h in range(1, IDX_HEADS):
                sc = sc + iwt_ref[bi, h:h + 1, :] * jnp.maximum(d[:, h * QW:(h + 1) * QW], 0.0)
            if diagonal:
                sc = jnp.where(_chunk_of(g * KG + row_iota) <= q_chunk, sc, -jnp.inf)
            sc_ref[bi, rows, :] = sc
            key = _order_key(lax.bitcast_convert_type(sc, I32))
            hi_ref[bi, rows, :] = (key >> 16).astype(I16)
            lo_ref[bi, rows, :] = (key ^ 0x8000).astype(I16)

    def score_inner(g, carry):
        score_group(g, False)
        return carry

    lax.fori_loop(0, j, score_inner, 0)
    score_group(j, True)

    def count16(ref, cands):
        c16 = [c.astype(I16) for c in cands]

        def one(g, accs):
            rows = rows_of(g)
            out = []
            for bi in batches:
                blk = ref[bi, rows, :]
                ind = jnp.where(blk >= c16[bi], jnp.int16(1), jnp.int16(0))
                out.append(accs[bi] + _tree_sum([ind[r:r + I16_ROWS] for r in range(0, KG, I16_ROWS)]))
            return tuple(out)

        def two(p, accs):
            return one(2 * p + 1, one(2 * p, accs))

        accs = tuple(jnp.zeros((I16_ROWS, QW), I16) for _ in batches)
        accs = lax.fori_loop(0, ngrp // 2, two, accs)
        accs = lax.fori_loop(2 * (ngrp // 2), ngrp, one, accs)
        return [jnp.sum(a.astype(I32), axis=0, keepdims=True) for a in accs]

    def bisect16(ref, want, state, first_bit, last_bit):
        def step(i, carry):
            ts, n_ge, n_gt = carry
            cands = [t + lax.shift_left(jnp.int32(1), 15 - i) for t in ts]
            cnts = count16(ref, cands)
            ok = [cnts[bi] >= want[bi] for bi in batches]
            return (tuple(jnp.where(ok[bi], cands[bi], ts[bi]) for bi in batches),
                    tuple(jnp.where(ok[bi], cnts[bi], n_ge[bi]) for bi in batches),
                    tuple(jnp.where(ok[bi], n_gt[bi], cnts[bi]) for bi in batches))
        return lax.fori_loop(first_bit, last_bit, step, state)

    def start16(n_all):
        return (tuple(jnp.full((1, QW), -32768, I32) for _ in batches), tuple(n_all),
                tuple(jnp.zeros((1, QW), I32) for _ in batches))

    want_hi = [jnp.full((1, QW), k_sel, I32) for _ in batches]
    n_rows = [jnp.zeros((1, QW), I32) + ngrp * KG for _ in batches]
    p_hi, n_ge_hi, n_gt_hi = bisect16(hi_ref, want_hi, start16(n_rows), 0, 16)

    def keep_ties(g, carry):
        rows = rows_of(g)
        for bi in batches:
            lo_ref[bi, rows, :] = jnp.where(hi_ref[bi, rows, :] == p_hi[bi].astype(I16), lo_ref[bi, rows, :],
                                            jnp.int16(-32768))
        return carry

    lax.fori_loop(0, ngrp, keep_ties, 0)
    want_lo = [k_sel - n_gt_hi[bi] for bi in batches]
    n_eq_hi = [n_ge_hi[bi] - n_gt_hi[bi] for bi in batches]
    p_lo, n_ge_lo, n_gt_lo = bisect16(lo_ref, want_lo, start16(n_eq_hi), 0, 16)
    for bi in batches:
        thr_ref[bi] = lax.shift_left(p_hi[bi], 16) + (p_lo[bi] + 32768)
        nge_ref[bi] = n_gt_hi[bi] + n_ge_lo[bi]
        ngt_ref[bi] = n_gt_hi[bi] + n_gt_lo[bi]

    def as_float(key):
        return lax.bitcast_convert_type(_order_key(key), F32)

    def count32(cands):
        def body(g, accs):
            rows = rows_of(g)
            out = []
            for bi in batches:
                ind = jnp.where(sc_ref[bi, rows, :] >= cands[bi], 1, 0)
                out.append(accs[bi] + jnp.sum(ind.reshape(KG // 8, 8, QW), axis=0))
            return tuple(out)

        accs = lax.fori_loop(0, ngrp, body, tuple(jnp.zeros((8, QW), I32) for _ in batches))
        return [jnp.sum(a, axis=0, keepdims=True) for a in accs]

    n_ge_f = count32([as_float(thr_ref[bi]) for bi in batches])
    differs = _tree_sum([jnp.where(n_ge_f[bi] != nge_ref[bi], 1, 0) for bi in batches])

    @pl.when(jnp.max(differs) > 0)
    def _():
        def step(i, carry):
            ts, n_ge, n_gt = carry
            cands = [t + lax.shift_left(jnp.int32(1), 31 - i) for t in ts]
            cnts = count32([as_float(c) for c in cands])
            ok = [cnts[bi] >= k_sel for bi in batches]
            return (tuple(jnp.where(ok[bi], cands[bi], ts[bi]) for bi in batches),
                    tuple(jnp.where(ok[bi], cnts[bi], n_ge[bi]) for bi in batches),
                    tuple(jnp.where(ok[bi], n_gt[bi], cnts[bi]) for bi in batches))
        init = (tuple(jnp.full((1, QW), INT_MIN, I32) for _ in batches), tuple(n_rows),
                tuple(jnp.zeros((1, QW), I32) for _ in batches))
        ts, n_ge, n_gt = lax.fori_loop(0, 32, step, init)
        for bi in batches:
            thr_ref[bi] = ts[bi]
            nge_ref[bi] = n_ge[bi]
            ngt_ref[bi] = n_gt[bi]

    thr = [as_float(thr_ref[bi]) for bi in batches]

    excess = _tree_sum([jnp.where((nge_ref[bi] > k_sel) & (thr[bi] > -jnp.inf), 1, 0) for bi in batches])

    @pl.when(jnp.max(excess) > 0)
    def _():
        tri = jnp.where(row_iota >= lane_iota, 1.0, 0.0).astype(BF16)
        need = [(k_sel - ngt_ref[bi]).astype(F32) for bi in batches]

        def demote(g, carry):
            rows = rows_of(g)
            out = []
            for bi in batches:
                blk = sc_ref[bi, rows, :]
                eq = jnp.where(blk == thr[bi], 1.0, 0.0)
                incl = jnp.dot(tri, eq.astype(BF16), preferred_element_type=F32)
                rank = incl - eq + carry[bi]
                sc_ref[bi, rows, :] = jnp.where((blk == thr[bi]) & (rank >= need[bi]), -jnp.inf, blk)
                out.append(carry[bi] + incl[KG - 1:KG, :])
            return tuple(out)

        lax.fori_loop(0, ngrp, demote, tuple(jnp.zeros((1, QW), F32) for _ in batches))

    thr_sel = [jnp.maximum(t, jnp.finfo(F32).min) for t in thr]
    for bi in batches:
        m_ref[bi] = jnp.full((1, hq), NEG_BIG, F32)
        acc_ref[bi] = jnp.zeros((CT_ROWS, hq), F32)

    def attend(g, with_bias):
        half = KG // 2
        chains = [(bi, h) for bi in batches for h in range(A_HEADS)]
        sel_h, c_h = {}, {}
        for bi in batches:
            for u in range(2):
                rows = pl.ds(pl.multiple_of(g * KG + u * half, half), half)
                sel_h[bi, u] = jnp.where(sc_ref[bi, rows, :] >= thr_sel[bi], 0.0, NEG_BIG)
                c_h[bi, u] = c_ref[bi, rows, :]
        s_all = {}
        for bi, h in chains:
            for u in range(2):
                s_all[bi, h, u] = jnp.dot(c_h[bi, u], qlt_ref[bi, h], preferred_element_type=F32)
        p_all = {}
        alpha_all = {}
        for bi, h in chains:
            cols = slice(h * QW, (h + 1) * QW)
            m_loc, p_loc = [], []
            for u in range(2):
                s = s_all[bi, h, u]
                if with_bias:
                    s = s + bias_ref[j - g, u * half:(u + 1) * half, cols]
                s = s + sel_h[bi, u]
                m_u = jnp.max(s, axis=0, keepdims=True)
                m_loc.append(m_u)
                p_loc.append(jnp.exp2(s - m_u).astype(BF16))
            m_old = m_ref[bi, :, cols]
            m_new = jnp.maximum(m_old, jnp.maximum(m_loc[0], m_loc[1]))
            alpha_all[bi, h] = jnp.exp2(m_old - m_new)
            p_all[bi, h] = jnp.concatenate([p_loc[u] * jnp.exp2(m_loc[u] - m_new).astype(BF16) for u in range(2)],
                                           axis=0)
            m_ref[bi, :, cols] = m_new
        for bi, h in chains:
            cols = slice(h * QW, (h + 1) * QW)
            pv = jnp.dot(ct_ref[bi, g], p_all[bi, h], preferred_element_type=F32)
            acc_ref[bi, :, cols] = acc_ref[bi, :, cols] * alpha_all[bi, h] + pv

    def attend_far(g, carry):
        attend(g, False)
        return carry

    lax.fori_loop(0, j - 1, attend_far, 0)

    @pl.when(j >= 1)
    def _():
        attend(j - 1, True)

    attend(j, True)

    for bi in batches:
        ot = acc_ref[bi, :A_KV_RANK, :] * (1.0 / acc_ref[bi, A_KV_RANK:A_KV_RANK + 1, :])
        for h in range(A_HEADS):
            o_h = ot[:, h * QW:(h + 1) * QW].T
            cols = slice(h * A_HEAD_DIM, (h + 1) * A_HEAD_DIM)
            oa = jnp.dot(o_h.astype(BF16), wuv_ref[h], preferred_element_type=F32)
            o_ref[bi, :, cols] = (oa * ga_ref[bi, :, cols].astype(F32)).astype(BF16)


def _dsa(iqt, iwt, ik, c, ctb, qlt, bias, w_uv, ga, bsz, seq):
    assert seq % QW == 0 and QW % Q_BLOCK == 0
    nb = DSA_NB if bsz % DSA_NB == 0 else 1
    k_sel = min(TOPK_MAX, seq // 4)
    hq = A_HEADS * QW
    return pl.pallas_call(
        functools.partial(_dsa_kernel, k_sel=k_sel, nb=nb),
        grid=(bsz // nb, seq // QW),
        in_specs=[
            pl.BlockSpec((nb, IDX_HEADS * IDX_DIM, QW), lambda b, j: (b, 0, j)),
            pl.BlockSpec((nb, IDX_HEADS, QW), lambda b, j: (b, 0, j)),
            pl.BlockSpec((nb, seq, IDX_DIM), lambda b, j: (b, 0, 0)),
            pl.BlockSpec((nb, seq, A_KV_RANK), lambda b, j: (b, 0, 0)),
            pl.BlockSpec((nb, seq // KG, CT_ROWS, KG), lambda b, j: (b, 0, 0, 0)),
            pl.BlockSpec((nb, A_HEADS, A_KV_RANK, QW), lambda b, j: (b, 0, 0, j)),
            pl.BlockSpec((2, KG, hq), lambda b, j: (0, 0, 0)),
            pl.BlockSpec((A_HEADS, A_KV_RANK, A_HEAD_DIM), lambda b, j: (0, 0, 0)),
            pl.BlockSpec((nb, QW, A_WIDTH), lambda b, j: (b, j, 0)),
        ],
        out_specs=pl.BlockSpec((nb, QW, A_WIDTH), lambda b, j: (b, j, 0)),
        out_shape=jax.ShapeDtypeStruct((bsz, seq, A_WIDTH), BF16),
        scratch_shapes=[
            pltpu.VMEM((nb, seq, QW), F32),
            pltpu.VMEM((nb, seq, QW), I16),
            pltpu.VMEM((nb, seq, QW), I16),
            pltpu.VMEM((nb, CT_ROWS, hq), F32),
            pltpu.VMEM((nb, 1, hq), F32),
            pltpu.VMEM((nb, 1, QW), I32),
            pltpu.VMEM((nb, 1, QW), I32),
            pltpu.VMEM((nb, 1, QW), I32),
        ],
        compiler_params=pltpu.CompilerParams(dimension_semantics=("parallel", "arbitrary"),
                                             vmem_limit_bytes=VMEM_LIMIT),
        name="dsa",
    )(iqt, iwt, ik.reshape(bsz, seq, IDX_DIM), c.reshape(bsz, seq, A_KV_RANK), ctb, qlt, bias,
      w_uv.astype(BF16), ga.reshape(bsz, seq, A_WIDTH))


def _hgrn_kernel(qs_ref, logf_ref, v_ref, gb_ref, hg_ref, o_ref, st_ref, bq_ref, oi_ref, *, nb):
    @pl.when(pl.program_id(1) == 0)
    def _():
        st_ref[...] = jnp.zeros(st_ref.shape, F32)

    steps = qs_ref.shape[1]
    nchunk = steps // CHUNK
    dk, dv = B_KEY_DIM, B_VAL_DIM
    r64 = lax.broadcasted_iota(I32, (CHUNK, CHUNK), 0)
    c64 = lax.broadcasted_iota(I32, (CHUNK, CHUNK), 1)
    tri = jnp.where(r64 >= c64, 1.0, 0.0).astype(BF16)
    rt = lax.broadcasted_iota(I32, (CHUNK, LANES), 0)
    cs = lax.broadcasted_iota(I32, (CHUNK, LANES), 1)
    causal = cs <= rt
    nsub = CHUNK // SUB
    zpad_f = jnp.zeros((LANES - CHUNK, LANES), F32)
    nt = (((1,), (1,)), ((), ()))
    units = [(bi, ci, h) for bi in range(nb) for ci in range(nchunk) for h in range(B_HEADS)]

    span = jnp.zeros((1, B_FORGET), F32)
    for bi in range(nb):
        for r0 in range(0, steps, SUB):
            span = jnp.maximum(span, -jnp.sum(logf_ref[bi, r0:r0 + SUB, :], axis=0, keepdims=True))
    safe = jnp.max(span) <= EXP_CAP
    chains = [(bi, h) for bi in range(nb) for h in range(B_HEADS)]
    lf_all, b_all = {}, {}

    def cum_decay():
        for bi in range(nb):
            for ci in range(nchunk):
                lf = logf_ref[bi, ci * CHUNK:(ci + 1) * CHUNK, :]
                hi = lf.astype(BF16)
                lo = (lf - hi.astype(F32)).astype(BF16)
                lf_all[bi, ci] = lf
                b_all[bi, ci] = (jnp.dot(tri, hi, preferred_element_type=F32)
                                 + jnp.dot(tri, lo, preferred_element_type=F32))

    def finish(o, bi, rows, h):
        cols = slice(h * dv, (h + 1) * dv)
        ms = jnp.mean(o * o, axis=-1, keepdims=True)
        z = o * lax.rsqrt(ms + RMS_EPS) * hg_ref[:, cols] * gb_ref[bi, rows, cols].astype(F32)
        o_ref[bi, rows, cols] = z.astype(BF16)

    def state_operands(u):
        bi, ci, h = u
        rows = slice(ci * CHUNK, (ci + 1) * CHUNK)
        cols = slice(h * dk, (h + 1) * dk)
        b = b_all[bi, ci][:, cols]
        kk = 1.0 - jnp.exp2(lf_all[bi, ci][:, cols])
        b_last = b[CHUNK - 1:CHUNK, :]
        k_hat = (kk * jnp.exp2(b_last - b)).astype(BF16)
        k_pad = jnp.concatenate([k_hat, jnp.zeros((LANES - CHUNK, dk), BF16)], axis=0)
        v = v_ref[bi, rows, h * dv:(h + 1) * dv]
        v_t = jnp.concatenate([v.astype(F32), zpad_f], axis=0).T.astype(BF16)
        return b, kk, k_pad, v_t, jnp.exp2(b_last)

    @pl.when(safe)
    def _():
        cum_decay()
        q_in, q_st, kst, k_pad, v_t, e_last = {}, {}, {}, {}, {}, {}
        r_all, lhs, upd = {}, {}, {}

        def operands(u):
            bi, ci, h = u
            b, kk, k_pad[u], v_t[u], e_last[u] = state_operands(u)
            q = qs_ref[bi, ci * CHUNK:(ci + 1) * CHUNK, h * dk:(h + 1) * dk].astype(F32)
            starts = [jnp.zeros((1, dk), F32)] + [b[i * SUB - 1:i * SUB, :] for i in range(1, nsub)]
            ref_rows = jnp.concatenate([jnp.broadcast_to(s_, (SUB, dk)) for s_ in starts], axis=0)
            qi = q * jnp.exp2(b - ref_rows)
            q_in[u] = qi.astype(BF16)
            q_st[u] = (qi * jnp.exp2(ref_rows)).astype(BF16)
            secs = []
            for i in range(nsub):
                n = SUB * (i + 1)
                ks = (kk[:n, :] * jnp.exp2(starts[i] - b[:n, :])).astype(BF16)
                secs.append(ks if n == CHUNK else jnp.concatenate([ks, jnp.zeros((CHUNK - n, dk), BF16)], axis=0))
            kst[u] = jnp.concatenate(secs, axis=0)

        def matmuls(u):
            r_all[u] = lax.dot_general(q_in[u], kst[u], nt, preferred_element_type=F32)
            upd[u] = jnp.dot(v_t[u], k_pad[u], preferred_element_type=F32)

        def scores(u):
            r = r_all[u]
            slabs = []
            for i in range(nsub):
                lane0 = (i * CHUNK // LANES) * LANES
                slab = r[i * SUB:(i + 1) * SUB, lane0:lane0 + LANES]
                if (i * CHUNK) % LANES:
                    slab = pltpu.roll(slab, LANES - (i * CHUNK) % LANES, 1)
                slabs.append(slab)
            a = jnp.concatenate(slabs, axis=0)
            a = jnp.where(causal, a, 0.0).astype(BF16)
            lhs[u] = jnp.concatenate([a, q_st[u]], axis=1)

        for u in units:
            operands(u)
        for u in units:
            matmuls(u)
            scores(u)

        st = {ch: st_ref[ch[0], ch[1]] for ch in chains}
        for ci in range(nchunk):
            for bi, h in chains:
                u = (bi, ci, h)
                w = jnp.concatenate([v_t[u], st[bi, h].astype(BF16)], axis=1)
                o = lax.dot_general(lhs[u], w, nt, preferred_element_type=F32)
                finish(o, bi, slice(ci * CHUNK, (ci + 1) * CHUNK), h)
            for bi, h in chains:
                st[bi, h] = st[bi, h] * e_last[bi, ci, h] + upd[bi, ci, h]
        for bi, h in chains:
            st_ref[bi, h] = st[bi, h]

    @pl.when(jnp.logical_not(safe))
    def _():
        cum_decay()
        for bi in range(nb):
            for ci in range(nchunk):
                rows = slice(ci * CHUNK, (ci + 1) * CHUNK)
                bq_ref[0, bi, rows, :] = b_all[bi, ci]
                bq_ref[1, bi, rows, :] = qs_ref[bi, rows, :].astype(F32)
        row_id = lax.broadcasted_iota(I32, (CHUNK, 1), 0)
        for bi in range(nb):
            for ci in range(nchunk):
                rows = slice(ci * CHUNK, (ci + 1) * CHUNK)
                b = b_all[bi, ci]
                kk = 1.0 - jnp.exp2(lf_all[bi, ci])
                v = v_ref[bi, rows, :].astype(F32)

                def row(t, carry, bi=bi, ci=ci, b=b, kk=kk, v=v):
                    b_t = bq_ref[0, bi, pl.ds(ci * CHUNK + t, 1), :]
                    q_t = bq_ref[1, bi, pl.ds(ci * CHUNK + t, 1), :]
                    w = q_t * kk * jnp.exp2(jnp.minimum(b_t - b, 0.0))
                    outs = []
                    for h in range(B_HEADS):
                        s_h = jnp.sum(w[:, h * dk:(h + 1) * dk], axis=1, keepdims=True)
                        s_h = jnp.where(row_id <= t, s_h, 0.0)
                        outs.append(jnp.sum(s_h * v[:, h * dv:(h + 1) * dv], axis=0, keepdims=True))
                    oi_ref[bi, pl.ds(ci * CHUNK + t, 1), :] = jnp.concatenate(outs, axis=1)
                    return carry

                lax.fori_loop(0, CHUNK, row, 0)
        st = {ch: st_ref[ch[0], ch[1]] for ch in chains}
        for ci in range(nchunk):
            rows = slice(ci * CHUNK, (ci + 1) * CHUNK)
            for bi, h in chains:
                b, kk, k_pad, v_t, e_last = state_operands((bi, ci, h))
                q_st = (qs_ref[bi, rows, h * dk:(h + 1) * dk].astype(F32) * jnp.exp2(b)).astype(BF16)
                o = oi_ref[bi, rows, h * dv:(h + 1) * dv] + lax.dot_general(
                    q_st, st[bi, h].astype(BF16), nt, preferred_element_type=F32)
                finish(o, bi, rows, h)
                st[bi, h] = st[bi, h] * e_last + jnp.dot(v_t, k_pad, preferred_element_type=F32)
        for bi, h in chains:
            st_ref[bi, h] = st[bi, h]


def _hgrn(qs, logf, v, gb, hg, bsz, seq):
    steps = min(HGRN_STEPS, seq)
    nb = HGRN_NB if bsz % HGRN_NB == 0 else 1
    spec = lambda n: pl.BlockSpec((nb, steps, n), lambda b, s: (b, s, 0))
    return pl.pallas_call(
        functools.partial(_hgrn_kernel, nb=nb),
        grid=(bsz // nb, seq // steps),
        in_specs=[spec(B_FORGET), spec(B_FORGET), spec(B_WIDTH), spec(B_WIDTH),
                  pl.BlockSpec((1, B_WIDTH), lambda b, s: (0, 0))],
        out_specs=spec(B_WIDTH),
        out_shape=jax.ShapeDtypeStruct((bsz, seq, B_WIDTH), BF16),
        scratch_shapes=[pltpu.VMEM((nb, B_HEADS, B_VAL_DIM, B_KEY_DIM), F32),
                        pltpu.VMEM((2, nb, steps, B_FORGET), F32),
                        pltpu.VMEM((nb, steps, B_WIDTH), F32)],
        compiler_params=pltpu.CompilerParams(dimension_semantics=("parallel", "arbitrary"),
                                             vmem_limit_bytes=VMEM_LIMIT),
        name="hgrn",
    )(qs.reshape(bsz, seq, B_FORGET), logf.reshape(bsz, seq, B_FORGET), v.reshape(bsz, seq, B_WIDTH),
      gb.reshape(bsz, seq, B_WIDTH), hg.reshape(1, B_WIDTH).astype(F32))


def _out_kernel(za_ref, zb_ref, x_ref, wo_ref, lng_ref, lnb_ref, out_ref):
    for r0 in range(0, x_ref.shape[0], OUT_SUB):
        rows = slice(r0, r0 + OUT_SUB)
        z = jnp.concatenate([za_ref[rows, :], zb_ref[rows, :]], axis=1)
        y = jnp.dot(z, wo_ref[...], preferred_element_type=F32)
        r = DEEPNORM_ALPHA * x_ref[rows, :] + y
        mu = jnp.mean(r, axis=-1, keepdims=True)
        rc = r - mu
        var = jnp.mean(rc * rc, axis=-1, keepdims=True)
        out_ref[rows, :] = rc * lax.rsqrt(var + LN_EPS) * lng_ref[...] + lnb_ref[...]


def _output(za, zb, x2, w_o, ln_g, ln_b):
    t = x2.shape[0]
    tq = min(OUT_TILE, t)
    row = lambda n: pl.BlockSpec((tq, n), lambda i: (i, 0))
    full = lambda shape: pl.BlockSpec(shape, lambda i: (0,) * len(shape))
    d_mix = A_WIDTH + B_WIDTH
    return pl.pallas_call(
        _out_kernel,
        grid=(t // tq,),
        in_specs=[row(A_WIDTH), row(B_WIDTH), row(D_MODEL),
                  full((d_mix, D_MODEL)), full((1, D_MODEL)), full((1, D_MODEL))],
        out_specs=row(D_MODEL),
        out_shape=jax.ShapeDtypeStruct((t, D_MODEL), F32),
        compiler_params=pltpu.CompilerParams(dimension_semantics=("parallel",), vmem_limit_bytes=VMEM_LIMIT),
        name="out",
    )(za, zb, x2, w_o.astype(BF16), ln_g.reshape(1, D_MODEL).astype(F32), ln_b.reshape(1, D_MODEL).astype(F32))


def kernel(x, w_in, w_uk, w_uv, kv_norm_g, rel_bias, lb_logits, hgrn_norm_g, w_o, ln_g, ln_b):
    bsz, seq, d = x.shape
    assert d == D_MODEL and seq % Q_BLOCK == 0 and DEPTH == 1
    t = bsz * seq
    x2 = x.reshape(t, d)
    bias = _bias_tiles(rel_bias)
    ik, c, ga, gb, qs, logf, v, iqt, iwt, qlt, ctb = _project(
        x2, w_in[0], w_uk[0], kv_norm_g[0], lb_logits, bsz, seq)
    za = _dsa(iqt, iwt, ik, c, ctb, qlt, bias, w_uv[0], ga, bsz, seq)
    zb = _hgrn(qs, logf, v, gb, hgrn_norm_g[0], bsz, seq)
    out = _output(za.reshape(t, A_WIDTH), zb.reshape(t, B_WIDTH), x2, w_o[0], ln_g[0], ln_b[0])
    return out.reshape(bsz, seq, d)
```

```python
import functools
import math

import jax
import jax.numpy as jnp
from jax import lax
from jax.experimental import pallas as pl
from jax.experimental.pallas import tpu as pltpu

F32 = jnp.float32
BF16 = jnp.bfloat16
I32 = jnp.int32
I16 = jnp.int16

D_MODEL = 1024
DEPTH = 1
CHUNK = 64
Q_BLOCK = 128
A_HEADS = 4
A_HEAD_DIM = 128
A_WIDTH = A_HEADS * A_HEAD_DIM
A_KV_RANK = 128
IDX_HEADS = 8
IDX_DIM = 64
TOPK_MAX = 256
B_HEADS = 4
B_KEY_DIM = 128
B_VAL_DIM = 128
B_WIDTH = B_HEADS * B_VAL_DIM
B_FORGET = B_HEADS * B_KEY_DIM
REL_BUCKETS = 32
REL_MAX_DIST = 256
DEEPNORM_ALPHA = (2.0 * DEPTH) ** 0.25
LN_EPS = 1e-5
RMS_EPS = 1e-6

_SPLITS = (
    ('a_q', A_WIDTH), ('a_ckv', A_KV_RANK), ('a_iq', IDX_HEADS * IDX_DIM), ('a_ik', IDX_DIM),
    ('a_iw', IDX_HEADS), ('a_gate', A_WIDTH), ('b_q', B_FORGET), ('b_f', B_FORGET),
    ('b_i', B_WIDTH), ('b_gate', B_WIDTH),
)
_OFFSETS = {}
_off = 0
for _name, _n in _SPLITS:
    _OFFSETS[_name] = (_off, _n)
    _off += _n

LANES = 128
QW = 256
KG = 256
DSA_NB = 4
I16_ROWS = 16
CT_ROWS = A_KV_RANK + I16_ROWS
SUB = 16
HGRN_STEPS = 512
HGRN_NB = 2
EXP_CAP = 100.0
NEG_BIG = -1e30
LOG2_E = math.log2(math.e)
INT_MIN = -2 ** 31
PROJ_TILE = 1024
OUT_TILE = 2048
OUT_SUB = 256
OUT_RING = 3
VMEM_LIMIT = 56 * 1024 * 1024


def _col(w, name):
    o, n = _OFFSETS[name]
    return w[:, o:o + n]


def _t5_bucket(rel):
    nb = REL_BUCKETS // 2
    max_exact = nb // 2
    ret = jnp.where(rel > 0, nb, 0).astype(I32)
    n = jnp.abs(rel)
    nf = jnp.maximum(n, max_exact).astype(F32)
    large = max_exact + jnp.floor(jnp.log(nf / max_exact) / math.log(REL_MAX_DIST / max_exact)
                                  * (nb - max_exact)).astype(I32)
    large = jnp.minimum(large, nb - 1)
    return ret + jnp.where(n < max_exact, n, large)


def _bias_kernel(rb_ref, out_ref):
    sl = lax.broadcasted_iota(I32, (KG, QW), 0)
    ql = lax.broadcasted_iota(I32, (KG, QW), 1)
    far = REL_BUCKETS // 2 - 1
    for d in range(2):
        bucket = _t5_bucket(sl - ql - KG * d)
        for h in range(A_HEADS):
            acc = jnp.zeros((KG, QW), F32)
            for bk in range(REL_BUCKETS):
                acc = jnp.where(bucket == bk, (rb_ref[bk, h] - rb_ref[far, h]) * LOG2_E, acc)
            out_ref[d, :, h * QW:(h + 1) * QW] = acc


def _bias_tiles(rel_bias):
    assert KG == QW and KG >= REL_MAX_DIST
    return pl.pallas_call(
        _bias_kernel,
        out_shape=jax.ShapeDtypeStruct((2, KG, A_HEADS * QW), F32),
        in_specs=[pl.BlockSpec(memory_space=pltpu.SMEM)],
        out_specs=pl.BlockSpec(memory_space=pltpu.VMEM),
        name="bias",
    )(rel_bias.astype(F32))


def _proj_kernel(x_ref, wrow_ref, wt_ref, wukt_ref, g_ref, lbl_ref,
                 ik_ref, c_ref, ga_ref, gb_ref, qs_ref, logf_ref, v_ref,
                 iqt_ref, iwt_ref, qlt_ref, ct_ref):
    tq = x_ref.shape[0]
    xb = x_ref[...].astype(BF16)

    def rowdot(lo, n):
        return jnp.dot(xb, wrow_ref[:, lo:lo + n], preferred_element_type=F32)

    h0 = rowdot(0, 256)
    a = h0[:, :A_KV_RANK]
    ms = jnp.mean(a * a, axis=-1, keepdims=True)
    c = a * lax.rsqrt(ms + RMS_EPS) * g_ref[...]
    c_ref[...] = c.astype(BF16)
    ct = c.T
    ones = jnp.ones((CT_ROWS - A_KV_RANK, KG), BF16)
    for i in range(tq // KG):
        ct_ref[0, i] = jnp.concatenate([ct[:, i * KG:(i + 1) * KG].astype(BF16), ones], axis=0)
    ik_ref[...] = h0[:, A_KV_RANK:A_KV_RANK + IDX_DIM].astype(BF16)

    ga_ref[...] = jax.nn.silu(rowdot(256, A_WIDTH)).astype(BF16)
    qs_ref[...] = jax.nn.silu(rowdot(768, B_FORGET)).astype(BF16)
    lbl = lbl_ref[...]
    e = jnp.exp(lbl - jnp.max(lbl, axis=0, keepdims=True))
    lb = e[0:1, :] / jnp.sum(e, axis=0, keepdims=True)
    f = lb + (1.0 - lb) * jax.nn.sigmoid(rowdot(1280, B_FORGET))
    logf_ref[...] = jnp.log2(f)
    v_ref[...] = rowdot(1792, B_WIDTH).astype(BF16)
    gb_ref[...] = jax.nn.silu(rowdot(2304, B_WIDTH)).astype(BF16)

    ht = lax.dot_general(wt_ref[...], xb, (((1,), (1,)), ((), ())), preferred_element_type=F32)
    iqt_ref[0] = ht[A_WIDTH:A_WIDTH + IDX_HEADS * IDX_DIM].astype(BF16)
    o = A_WIDTH + IDX_HEADS * IDX_DIM
    iwt_ref[0] = ht[o:o + IDX_HEADS] * (IDX_HEADS ** -0.5 * IDX_DIM ** -0.5)
    for h in range(A_HEADS):
        aq = ht[h * A_HEAD_DIM:(h + 1) * A_HEAD_DIM].astype(BF16)
        ql = jnp.dot(wukt_ref[h], aq, preferred_element_type=F32) * (A_HEAD_DIM ** -0.5 * LOG2_E)
        qlt_ref[0, h] = ql.astype(BF16)


def _project(x2, w_in, w_uk, kv_g, lb_logits, bsz, seq):
    t = bsz * seq
    tq = min(PROJ_TILE, seq)
    per_b = seq // tq
    pad64 = jnp.zeros((D_MODEL, 64), F32)
    wrow = jnp.concatenate([_col(w_in, 'a_ckv'), _col(w_in, 'a_ik'), pad64, _col(w_in, 'a_gate'),
                            _col(w_in, 'b_q'), _col(w_in, 'b_f'), _col(w_in, 'b_i'),
                            _col(w_in, 'b_gate')], axis=1).astype(BF16)
    pad8 = jnp.zeros((D_MODEL, 8), F32)
    wt = jnp.concatenate([_col(w_in, 'a_q'), _col(w_in, 'a_iq'), _col(w_in, 'a_iw'), pad8],
                         axis=1).T.astype(BF16)
    wukt = jnp.swapaxes(w_uk, 1, 2).astype(BF16)
    nrow, nt = wrow.shape[1], wt.shape[0]
    row = lambda n: pl.BlockSpec((tq, n), lambda i: (i, 0))
    full = lambda shape: pl.BlockSpec(shape, lambda i: (0,) * len(shape))
    out_shape = (
        jax.ShapeDtypeStruct((t, IDX_DIM), BF16),
        jax.ShapeDtypeStruct((t, A_KV_RANK), BF16),
        jax.ShapeDtypeStruct((t, A_WIDTH), BF16),
        jax.ShapeDtypeStruct((t, B_WIDTH), BF16),
        jax.ShapeDtypeStruct((t, B_FORGET), BF16),
        jax.ShapeDtypeStruct((t, B_FORGET), F32),
        jax.ShapeDtypeStruct((t, B_WIDTH), BF16),
        jax.ShapeDtypeStruct((bsz, IDX_HEADS * IDX_DIM, seq), BF16),
        jax.ShapeDtypeStruct((bsz, IDX_HEADS, seq), F32),
        jax.ShapeDtypeStruct((bsz, A_HEADS, A_KV_RANK, seq), BF16),
        jax.ShapeDtypeStruct((bsz, seq // KG, CT_ROWS, KG), BF16),
    )
    out_specs = (
        row(IDX_DIM), row(A_KV_RANK), row(A_WIDTH), row(B_WIDTH), row(B_FORGET), row(B_FORGET), row(B_WIDTH),
        pl.BlockSpec((1, IDX_HEADS * IDX_DIM, tq), lambda i: (i // per_b, 0, i % per_b)),
        pl.BlockSpec((1, IDX_HEADS, tq), lambda i: (i // per_b, 0, i % per_b)),
        pl.BlockSpec((1, A_HEADS, A_KV_RANK, tq), lambda i: (i // per_b, 0, 0, i % per_b)),
        pl.BlockSpec((1, tq // KG, CT_ROWS, KG), lambda i: (i // per_b, i % per_b, 0, 0)),
    )
    return pl.pallas_call(
        _proj_kernel,
        grid=(t // tq,),
        in_specs=[row(D_MODEL), full((D_MODEL, nrow)), full((nt, D_MODEL)),
                  full((A_HEADS, A_KV_RANK, A_HEAD_DIM)), full((1, A_KV_RANK)), full((DEPTH + 1, B_FORGET))],
        out_specs=out_specs,
        out_shape=out_shape,
        compiler_params=pltpu.CompilerParams(dimension_semantics=("parallel",), vmem_limit_bytes=VMEM_LIMIT),
        name="proj",
    )(x2, wrow, wt, wukt, kv_g.reshape(1, A_KV_RANK).astype(F32), lb_logits.astype(F32))


def _chunk_of(pos):
    return lax.shift_right_logical(pos, int(math.log2(CHUNK)))


def _tree_sum(parts):
    while len(parts) > 1:
        parts = [a + b for a, b in zip(parts[::2], parts[1::2])] + ([parts[-1]] if len(parts) % 2 else [])
    return parts[0]


def _order_key(bits):
    return jnp.where(bits < 0, jnp.int32(INT_MIN) - bits, bits)


def _dsa_kernel(iqt_ref, iwt_ref, ik_ref, c_ref, ct_ref, qlt_ref, bias_ref, wuv_ref, ga_ref, o_ref,
                sc_ref, hi_ref, lo_ref, acc_ref, m_ref, thr_ref, nge_ref, ngt_ref, *, k_sel, nb):
    j = pl.program_id(1)
    ngrp = j + 1
    hq = A_HEADS * QW
    batches = range(nb)
    row_iota = lax.broadcasted_iota(I32, (KG, QW), 0)
    lane_iota = lax.broadcasted_iota(I32, (KG, QW), 1)
    q_chunk = _chunk_of(j * QW + lane_iota)

    def rows_of(g):
        return pl.ds(pl.multiple_of(g * KG, KG), KG)

    def score_group(g, diagonal):
        rows = rows_of(g)
        for bi in batches:
            rhs = jnp.concatenate([iqt_ref[bi, h * IDX_DIM:(h + 1) * IDX_DIM, :] for h in range(IDX_HEADS)],
                                  axis=1)
            d = jnp.dot(ik_ref[bi, rows, :], rhs, preferred_element_type=F32)
            sc = iwt_ref[bi, 0:1, :] * jnp.maximum(d[:, :QW], 0.0)
            for h in range(1, IDX_HEADS):
                sc = sc + iwt_ref[bi, h:h + 1, :] * jnp.maximum(d[:, h * QW:(h + 1) * QW], 0.0)
            if diagonal:
                sc = jnp.where(_chunk_of(g * KG + row_iota) <= q_chunk, sc, -jnp.inf)
            sc_ref[bi, rows, :] = sc
            key = _order_key(lax.bitcast_convert_type(sc, I32))
            hi_ref[bi, rows, :] = (key >> 16).astype(I16)
            lo_ref[bi, rows, :] = (key ^ 0x8000).astype(I16)

    def score_inner(g, carry):
        score_group(g, False)
        return carry

    lax.fori_loop(0, j, score_inner, 0)
    score_group(j, True)

    def count16(ref, cands):
        c16 = [c.astype(I16) for c in cands]

        def one(g, accs):
            rows = rows_of(g)
            out = []
            for bi in batches:
                blk = ref[bi, rows, :]
                ind = jnp.where(blk >= c16[bi], jnp.int16(1), jnp.int16(0))
                out.append(accs[bi] + _tree_sum([ind[r:r + I16_ROWS] for r in range(0, KG, I16_ROWS)]))
            return tuple(out)

        def two(p, accs):
            return one(2 * p + 1, one(2 * p, accs))

        accs = tuple(jnp.zeros((I16_ROWS, QW), I16) for _ in batches)
        accs = lax.fori_loop(0, ngrp // 2, two, accs)
        accs = lax.fori_loop(2 * (ngrp // 2), ngrp, one, accs)
        return [jnp.sum(a.astype(I32), axis=0, keepdims=True) for a in accs]

    def bisect16(ref, want, state, first_bit, last_bit):
        def step(i, carry):
            ts, n_ge, n_gt = carry
            cands = [t + lax.shift_left(jnp.int32(1), 15 - i) for t in ts]
            cnts = count16(ref, cands)
            ok = [cnts[bi] >= want[bi] for bi in batches]
            return (tuple(jnp.where(ok[bi], cands[bi], ts[bi]) for bi in batches),
                    tuple(jnp.where(ok[bi], cnts[bi], n_ge[bi]) for bi in batches),
                    tuple(jnp.where(ok[bi], n_gt[bi], cnts[bi]) for bi in batches))
        return lax.fori_loop(first_bit, last_bit, step, state)

    def start16(n_all):
        return (tuple(jnp.full((1, QW), -32768, I32) for _ in batches), tuple(n_all),
                tuple(jnp.zeros((1, QW), I32) for _ in batches))

    want_hi = [jnp.full((1, QW), k_sel, I32) for _ in batches]
    n_rows = [jnp.zeros((1, QW), I32) + ngrp * KG for _ in batches]
    p_hi, n_ge_hi, n_gt_hi = bisect16(hi_ref, want_hi, start16(n_rows), 0, 16)

    def keep_ties(g, carry):
        rows = rows_of(g)
        for bi in batches:
            lo_ref[bi, rows, :] = jnp.where(hi_ref[bi, rows, :] == p_hi[bi].astype(I16), lo_ref[bi, rows, :],
                                            jnp.int16(-32768))
        return carry

    lax.fori_loop(0, ngrp, keep_ties, 0)
    want_lo = [k_sel - n_gt_hi[bi] for bi in batches]
    n_eq_hi = [n_ge_hi[bi] - n_gt_hi[bi] for bi in batches]
    p_lo, n_ge_lo, n_gt_lo = bisect16(lo_ref, want_lo, start16(n_eq_hi), 0, 16)
    for bi in batches:
        thr_ref[bi] = lax.shift_left(p_hi[bi], 16) + (p_lo[bi] + 32768)
        nge_ref[bi] = n_gt_hi[bi] + n_ge_lo[bi]
        ngt_ref[bi] = n_gt_hi[bi] + n_gt_lo[bi]

    def as_float(key):
        return lax.bitcast_convert_type(_order_key(key), F32)

    def count32(cands):
        def body(g, accs):
            rows = rows_of(g)
            out = []
            for bi in batches:
                ind = jnp.where(sc_ref[bi, rows, :] >= cands[bi], 1, 0)
                out.append(accs[bi] + jnp.sum(ind.reshape(KG // 8, 8, QW), axis=0))
            return tuple(out)

        accs = lax.fori_loop(0, ngrp, body, tuple(jnp.zeros((8, QW), I32) for _ in batches))
        return [jnp.sum(a, axis=0, keepdims=True) for a in accs]

    n_ge_f = count32([as_float(thr_ref[bi]) for bi in batches])
    differs = _tree_sum([jnp.where(n_ge_f[bi] != nge_ref[bi], 1, 0) for bi in batches])

    @pl.when(jnp.max(differs) > 0)
    def _():
        def step(i, carry):
            ts, n_ge, n_gt = carry
            cands = [t + lax.shift_left(jnp.int32(1), 31 - i) for t in ts]
            cnts = count32([as_float(c) for c in cands])
            ok = [cnts[bi] >= k_sel for bi in batches]
            return (tuple(jnp.where(ok[bi], cands[bi], ts[bi]) for bi in batches),
                    tuple(jnp.where(ok[bi], cnts[bi], n_ge[bi]) for bi in batches),
                    tuple(jnp.where(ok[bi], n_gt[bi], cnts[bi]) for bi in batches))
        init = (tuple(jnp.full((1, QW), INT_MIN, I32) for _ in batches), tuple(n_rows),
                tuple(jnp.zeros((1, QW), I32) for _ in batches))
        ts, n_ge, n_gt = lax.fori_loop(0, 32, step, init)
        for bi in batches:
            thr_ref[bi] = ts[bi]
            nge_ref[bi] = n_ge[bi]
            ngt_ref[bi] = n_gt[bi]

    thr = [as_float(thr_ref[bi]) for bi in batches]

    excess = _tree_sum([jnp.where((nge_ref[bi] > k_sel) & (thr[bi] > -jnp.inf), 1, 0) for bi in batches])

    @pl.when(jnp.max(excess) > 0)
    def _():
        tri = jnp.where(row_iota >= lane_iota, 1.0, 0.0).astype(BF16)
        need = [(k_sel - ngt_ref[bi]).astype(F32) for bi in batches]

        def demote(g, carry):
            rows = rows_of(g)
            out = []
            for bi in batches:
                blk = sc_ref[bi, rows, :]
                eq = jnp.where(blk == thr[bi], 1.0, 0.0)
                incl = jnp.dot(tri, eq.astype(BF16), preferred_element_type=F32)
                rank = incl - eq + carry[bi]
                sc_ref[bi, rows, :] = jnp.where((blk == thr[bi]) & (rank >= need[bi]), -jnp.inf, blk)
                out.append(carry[bi] + incl[KG - 1:KG, :])
            return tuple(out)

        lax.fori_loop(0, ngrp, demote, tuple(jnp.zeros((1, QW), F32) for _ in batches))

    thr_sel = [jnp.maximum(t, jnp.finfo(F32).min) for t in thr]
    for bi in batches:
        m_ref[bi] = jnp.full((1, hq), NEG_BIG, F32)
        acc_ref[bi] = jnp.zeros((CT_ROWS, hq), F32)

    def attend(g, with_bias):
        half = KG // 2
        chains = [(bi, h) for bi in batches for h in range(A_HEADS)]
        sel_h, c_h = {}, {}
        for bi in batches:
            for u in range(2):
                rows = pl.ds(pl.multiple_of(g * KG + u * half, half), half)
                sel_h[bi, u] = jnp.where(sc_ref[bi, rows, :] >= thr_sel[bi], 0.0, NEG_BIG)
                c_h[bi, u] = c_ref[bi, rows, :]
        s_all = {}
        for bi, h in chains:
            for u in range(2):
                s_all[bi, h, u] = jnp.dot(c_h[bi, u], qlt_ref[bi, h], preferred_element_type=F32)
        p_all = {}
        alpha_all = {}
        for bi, h in chains:
            cols = slice(h * QW, (h + 1) * QW)
            m_loc, p_loc = [], []
            for u in range(2):
                s = s_all[bi, h, u]
                if with_bias:
                    s = s + bias_ref[j - g, u * half:(u + 1) * half, cols]
                s = s + sel_h[bi, u]
                m_u = jnp.max(s, axis=0, keepdims=True)
                m_loc.append(m_u)
                p_loc.append(jnp.exp2(s - m_u).astype(BF16))
            m_old = m_ref[bi, :, cols]
            m_new = jnp.maximum(m_old, jnp.maximum(m_loc[0], m_loc[1]))
            alpha_all[bi, h] = jnp.exp2(m_old - m_new)
            p_all[bi, h] = jnp.concatenate([p_loc[u] * jnp.exp2(m_loc[u] - m_new).astype(BF16) for u in range(2)],
                                           axis=0)
            m_ref[bi, :, cols] = m_new
        for bi, h in chains:
            cols = slice(h * QW, (h + 1) * QW)
            pv = jnp.dot(ct_ref[bi, g], p_all[bi, h], preferred_element_type=F32)
            acc_ref[bi, :, cols] = acc_ref[bi, :, cols] * alpha_all[bi, h] + pv

    def attend_far(g, carry):
        attend(g, False)
        return carry

    lax.fori_loop(0, j - 1, attend_far, 0)

    @pl.when(j >= 1)
    def _():
        attend(j - 1, True)

    attend(j, True)

    for bi in batches:
        ot = acc_ref[bi, :A_KV_RANK, :] * (1.0 / acc_ref[bi, A_KV_RANK:A_KV_RANK + 1, :])
        for h in range(A_HEADS):
            o_h = ot[:, h * QW:(h + 1) * QW].T
            cols = slice(h * A_HEAD_DIM, (h + 1) * A_HEAD_DIM)
            oa = jnp.dot(o_h.astype(BF16), wuv_ref[h], preferred_element_type=F32)
            o_ref[bi, :, cols] = (oa * ga_ref[bi, :, cols].astype(F32)).astype(BF16)


def _dsa(iqt, iwt, ik, c, ctb, qlt, bias, w_uv, ga, bsz, seq):
    assert seq % QW == 0 and QW % Q_BLOCK == 0
    nb = DSA_NB if bsz % DSA_NB == 0 else 1
    k_sel = min(TOPK_MAX, seq // 4)
    hq = A_HEADS * QW
    return pl.pallas_call(
        functools.partial(_dsa_kernel, k_sel=k_sel, nb=nb),
        grid=(bsz // nb, seq // QW),
        in_specs=[
            pl.BlockSpec((nb, IDX_HEADS * IDX_DIM, QW), lambda b, j: (b, 0, j)),
            pl.BlockSpec((nb, IDX_HEADS, QW), lambda b, j: (b, 0, j)),
            pl.BlockSpec((nb, seq, IDX_DIM), lambda b, j: (b, 0, 0)),
            pl.BlockSpec((nb, seq, A_KV_RANK), lambda b, j: (b, 0, 0)),
            pl.BlockSpec((nb, seq // KG, CT_ROWS, KG), lambda b, j: (b, 0, 0, 0)),
            pl.BlockSpec((nb, A_HEADS, A_KV_RANK, QW), lambda b, j: (b, 0, 0, j)),
            pl.BlockSpec((2, KG, hq), lambda b, j: (0, 0, 0)),
            pl.BlockSpec((A_HEADS, A_KV_RANK, A_HEAD_DIM), lambda b, j: (0, 0, 0)),
            pl.BlockSpec((nb, QW, A_WIDTH), lambda b, j: (b, j, 0)),
        ],
        out_specs=pl.BlockSpec((nb, QW, A_WIDTH), lambda b, j: (b, j, 0)),
        out_shape=jax.ShapeDtypeStruct((bsz, seq, A_WIDTH), BF16),
        scratch_shapes=[
            pltpu.VMEM((nb, seq, QW), F32),
            pltpu.VMEM((nb, seq, QW), I16),
            pltpu.VMEM((nb, seq, QW), I16),
            pltpu.VMEM((nb, CT_ROWS, hq), F32),
            pltpu.VMEM((nb, 1, hq), F32),
            pltpu.VMEM((nb, 1, QW), I32),
            pltpu.VMEM((nb, 1, QW), I32),
            pltpu.VMEM((nb, 1, QW), I32),
        ],
        compiler_params=pltpu.CompilerParams(dimension_semantics=("parallel", "arbitrary"),
                                             vmem_limit_bytes=VMEM_LIMIT),
        name="dsa",
    )(iqt, iwt, ik.reshape(bsz, seq, IDX_DIM), c.reshape(bsz, seq, A_KV_RANK), ctb, qlt, bias,
      w_uv.astype(BF16), ga.reshape(bsz, seq, A_WIDTH))


def _hgrn_kernel(qs_ref, logf_ref, v_ref, gb_ref, hg_ref, o_ref, st_ref, bq_ref, oi_ref, *, nb):
    @pl.when(pl.program_id(1) == 0)
    def _():
        st_ref[...] = jnp.zeros(st_ref.shape, F32)

    steps = qs_ref.shape[1]
    nchunk = steps // CHUNK
    dk, dv = B_KEY_DIM, B_VAL_DIM
    r64 = lax.broadcasted_iota(I32, (CHUNK, CHUNK), 0)
    c64 = lax.broadcasted_iota(I32, (CHUNK, CHUNK), 1)
    tri = jnp.where(r64 >= c64, 1.0, 0.0).astype(BF16)
    rt = lax.broadcasted_iota(I32, (CHUNK, LANES), 0)
    cs = lax.broadcasted_iota(I32, (CHUNK, LANES), 1)
    causal = cs <= rt
    nsub = CHUNK // SUB
    zpad_f = jnp.zeros((LANES - CHUNK, LANES), F32)
    nt = (((1,), (1,)), ((), ()))
    units = [(bi, ci, h) for bi in range(nb) for ci in range(nchunk) for h in range(B_HEADS)]

    span = jnp.zeros((1, B_FORGET), F32)
    for bi in range(nb):
        for r0 in range(0, steps, SUB):
            span = jnp.maximum(span, -jnp.sum(logf_ref[bi, r0:r0 + SUB, :], axis=0, keepdims=True))
    safe = jnp.max(span) <= EXP_CAP
    chains = [(bi, h) for bi in range(nb) for h in range(B_HEADS)]
    lf_all, b_all = {}, {}

    def cum_decay():
        for bi in range(nb):
            for ci in range(nchunk):
                lf = logf_ref[bi, ci * CHUNK:(ci + 1) * CHUNK, :]
                hi = lf.astype(BF16)
                lo = (lf - hi.astype(F32)).astype(BF16)
                lf_all[bi, ci] = lf
                b_all[bi, ci] = (jnp.dot(tri, hi, preferred_element_type=F32)
                                 + jnp.dot(tri, lo, preferred_element_type=F32))

    def finish(o, bi, rows, h):
        cols = slice(h * dv, (h + 1) * dv)
        ms = jnp.mean(o * o, axis=-1, keepdims=True)
        z = o * lax.rsqrt(ms + RMS_EPS) * hg_ref[:, cols] * gb_ref[bi, rows, cols].astype(F32)
        o_ref[bi, rows, cols] = z.astype(BF16)

    def state_operands(u):
        bi, ci, h = u
        rows = slice(ci * CHUNK, (ci + 1) * CHUNK)
        cols = slice(h * dk, (h + 1) * dk)
        b = b_all[bi, ci][:, cols]
        kk = 1.0 - jnp.exp2(lf_all[bi, ci][:, cols])
        b_last = b[CHUNK - 1:CHUNK, :]
        k_hat = (kk * jnp.exp2(b_last - b)).astype(BF16)
        k_pad = jnp.concatenate([k_hat, jnp.zeros((LANES - CHUNK, dk), BF16)], axis=0)
        v = v_ref[bi, rows, h * dv:(h + 1) * dv]
        v_t = jnp.concatenate([v.astype(F32), zpad_f], axis=0).T.astype(BF16)
        return b, kk, k_pad, v_t, jnp.exp2(b_last)

    @pl.when(safe)
    def _():
        cum_decay()
        q_in, q_st, kst, k_pad, v_t, e_last = {}, {}, {}, {}, {}, {}
        r_all, lhs, upd = {}, {}, {}

        def operands(u):
            bi, ci, h = u
            b, kk, k_pad[u], v_t[u], e_last[u] = state_operands(u)
            q = qs_ref[bi, ci * CHUNK:(ci + 1) * CHUNK, h * dk:(h + 1) * dk].astype(F32)
            starts = [jnp.zeros((1, dk), F32)] + [b[i * SUB - 1:i * SUB, :] for i in range(1, nsub)]
            ref_rows = jnp.concatenate([jnp.broadcast_to(s_, (SUB, dk)) for s_ in starts], axis=0)
            qi = q * jnp.exp2(b - ref_rows)
            q_in[u] = qi.astype(BF16)
            q_st[u] = (qi * jnp.exp2(ref_rows)).astype(BF16)
            secs = []
            for i in range(nsub):
                n = SUB * (i + 1)
                ks = (kk[:n, :] * jnp.exp2(starts[i] - b[:n, :])).astype(BF16)
                secs.append(ks if n == CHUNK else jnp.concatenate([ks, jnp.zeros((CHUNK - n, dk), BF16)], axis=0))
            kst[u] = jnp.concatenate(secs, axis=0)

        def matmuls(u):
            r_all[u] = lax.dot_general(q_in[u], kst[u], nt, preferred_element_type=F32)
            upd[u] = jnp.dot(v_t[u], k_pad[u], preferred_element_type=F32)

        def scores(u):
            r = r_all[u]
            slabs = []
            for i in range(nsub):
                lane0 = (i * CHUNK // LANES) * LANES
                slab = r[i * SUB:(i + 1) * SUB, lane0:lane0 + LANES]
                if (i * CHUNK) % LANES:
                    slab = pltpu.roll(slab, LANES - (i * CHUNK) % LANES, 1)
                slabs.append(slab)
            a = jnp.concatenate(slabs, axis=0)
            a = jnp.where(causal, a, 0.0).astype(BF16)
            lhs[u] = jnp.concatenate([a, q_st[u]], axis=1)

        for u in units:
            operands(u)
        for u in units:
            matmuls(u)
            scores(u)

        st = {ch: st_ref[ch[0], ch[1]] for ch in chains}
        for ci in range(nchunk):
            for bi, h in chains:
                u = (bi, ci, h)
                w = jnp.concatenate([v_t[u], st[bi, h].astype(BF16)], axis=1)
                o = lax.dot_general(lhs[u], w, nt, preferred_element_type=F32)
                finish(o, bi, slice(ci * CHUNK, (ci + 1) * CHUNK), h)
            for bi, h in chains:
                st[bi, h] = st[bi, h] * e_last[bi, ci, h] + upd[bi, ci, h]
        for bi, h in chains:
            st_ref[bi, h] = st[bi, h]

    @pl.when(jnp.logical_not(safe))
    def _():
        cum_decay()
        for bi in range(nb):
            for ci in range(nchunk):
                rows = slice(ci * CHUNK, (ci + 1) * CHUNK)
                bq_ref[0, bi, rows, :] = b_all[bi, ci]
                bq_ref[1, bi, rows, :] = qs_ref[bi, rows, :].astype(F32)
        row_id = lax.broadcasted_iota(I32, (CHUNK, 1), 0)
        for bi in range(nb):
            for ci in range(nchunk):
                rows = slice(ci * CHUNK, (ci + 1) * CHUNK)
                b = b_all[bi, ci]
                kk = 1.0 - jnp.exp2(lf_all[bi, ci])
                v = v_ref[bi, rows, :].astype(F32)

                def row(t, carry, bi=bi, ci=ci, b=b, kk=kk, v=v):
                    b_t = bq_ref[0, bi, pl.ds(ci * CHUNK + t, 1), :]
                    q_t = bq_ref[1, bi, pl.ds(ci * CHUNK + t, 1), :]
                    w = q_t * kk * jnp.exp2(jnp.minimum(b_t - b, 0.0))
                    outs = []
                    for h in range(B_HEADS):
                        s_h = jnp.sum(w[:, h * dk:(h + 1) * dk], axis=1, keepdims=True)
                        s_h = jnp.where(row_id <= t, s_h, 0.0)
                        outs.append(jnp.sum(s_h * v[:, h * dv:(h + 1) * dv], axis=0, keepdims=True))
                    oi_ref[bi, pl.ds(ci * CHUNK + t, 1), :] = jnp.concatenate(outs, axis=1)
                    return carry

                lax.fori_loop(0, CHUNK, row, 0)
        st = {ch: st_ref[ch[0], ch[1]] for ch in chains}
        for ci in range(nchunk):
            rows = slice(ci * CHUNK, (ci + 1) * CHUNK)
            for bi, h in chains:
                b, kk, k_pad, v_t, e_last = state_operands((bi, ci, h))
                q_st = (qs_ref[bi, rows, h * dk:(h + 1) * dk].astype(F32) * jnp.exp2(b)).astype(BF16)
                o = oi_ref[bi, rows, h * dv:(h + 1) * dv] + lax.dot_general(
                    q_st, st[bi, h].astype(BF16), nt, preferred_element_type=F32)
                finish(o, bi, rows, h)
                st[bi, h] = st[bi, h] * e_last + jnp.dot(v_t, k_pad, preferred_element_type=F32)
        for bi, h in chains:
            st_ref[bi, h] = st[bi, h]


def _hgrn(qs, logf, v, gb, hg, bsz, seq):
    steps = min(HGRN_STEPS, seq)
    nb = HGRN_NB if bsz % HGRN_NB == 0 else 1
    spec = lambda n: pl.BlockSpec((nb, steps, n), lambda b, s: (b, s, 0))
    return pl.pallas_call(
        functools.partial(_hgrn_kernel, nb=nb),
        grid=(bsz // nb, seq // steps),
        in_specs=[spec(B_FORGET), spec(B_FORGET), spec(B_WIDTH), spec(B_WIDTH),
                  pl.BlockSpec((1, B_WIDTH), lambda b, s: (0, 0))],
        out_specs=spec(B_WIDTH),
        out_shape=jax.ShapeDtypeStruct((bsz, seq, B_WIDTH), BF16),
        scratch_shapes=[pltpu.VMEM((nb, B_HEADS, B_VAL_DIM, B_KEY_DIM), F32),
                        pltpu.VMEM((2, nb, steps, B_FORGET), F32),
                        pltpu.VMEM((nb, steps, B_WIDTH), F32)],
        compiler_params=pltpu.CompilerParams(dimension_semantics=("parallel", "arbitrary"),
                                             vmem_limit_bytes=VMEM_LIMIT),
        name="hgrn",
    )(qs.reshape(bsz, seq, B_FORGET), logf.reshape(bsz, seq, B_FORGET), v.reshape(bsz, seq, B_WIDTH),
      gb.reshape(bsz, seq, B_WIDTH), hg.reshape(1, B_WIDTH).astype(F32))


def _out_kernel(za_ref, zb_ref, x_hbm, wo_ref, lng_ref, lnb_ref, out_ref, xbuf_ref, sem_ref, *, n_steps):
    s = pl.program_id(0)
    tq = xbuf_ref.shape[1]

    def x_copy(step):
        slot = lax.rem(step, OUT_RING)
        return pltpu.make_async_copy(x_hbm.at[pl.ds(pl.multiple_of(step * tq, tq), tq), :],
                                     xbuf_ref.at[slot], sem_ref.at[slot])

    @pl.when(s == 0)
    def _():
        for step in range(min(OUT_RING - 1, n_steps)):
            x_copy(step).start()

    @pl.when(s + (OUT_RING - 1) < n_steps)
    def _():
        x_copy(s + (OUT_RING - 1)).start()

    x_copy(s).wait()
    x_ref = xbuf_ref.at[lax.rem(s, OUT_RING)]
    for r0 in range(0, tq, OUT_SUB):
        rows = slice(r0, r0 + OUT_SUB)
        z = jnp.concatenate([za_ref[rows, :], zb_ref[rows, :]], axis=1)
        y = jnp.dot(z, wo_ref[...], preferred_element_type=F32)
        r = DEEPNORM_ALPHA * x_ref[rows, :] + y
        mu = jnp.mean(r, axis=-1, keepdims=True)
        rc = r - mu
        var = jnp.mean(rc * rc, axis=-1, keepdims=True)
        out_ref[rows, :] = rc * lax.rsqrt(var + LN_EPS) * lng_ref[...] + lnb_ref[...]


def _output(za, zb, x2, w_o, ln_g, ln_b):
    t = x2.shape[0]
    tq = min(OUT_TILE, t)
    row = lambda n: pl.BlockSpec((tq, n), lambda i: (i, 0))
    full = lambda shape: pl.BlockSpec(shape, lambda i: (0,) * len(shape))
    d_mix = A_WIDTH + B_WIDTH
    return pl.pallas_call(
        functools.partial(_out_kernel, n_steps=t // tq),
        grid=(t // tq,),
        in_specs=[row(A_WIDTH), row(B_WIDTH), pl.BlockSpec(memory_space=pl.ANY),
                  full((d_mix, D_MODEL)), full((1, D_MODEL)), full((1, D_MODEL))],
        out_specs=row(D_MODEL),
        out_shape=jax.ShapeDtypeStruct((t, D_MODEL), F32),
        scratch_shapes=[pltpu.VMEM((OUT_RING, tq, D_MODEL), F32), pltpu.SemaphoreType.DMA((OUT_RING,))],
        compiler_params=pltpu.CompilerParams(dimension_semantics=("arbitrary",), vmem_limit_bytes=VMEM_LIMIT),
        name="out",
    )(za, zb, x2, w_o.astype(BF16), ln_g.reshape(1, D_MODEL).astype(F32), ln_b.reshape(1, D_MODEL).astype(F32))


def kernel(x, w_in, w_uk, w_uv, kv_norm_g, rel_bias, lb_logits, hgrn_norm_g, w_o, ln_g, ln_b):
    bsz, seq, d = x.shape
    assert d == D_MODEL and seq % Q_BLOCK == 0 and DEPTH == 1
    t = bsz * seq
    x2 = x.reshape(t, d)
    bias = _bias_tiles(rel_bias)
    ik, c, ga, gb, qs, logf, v, iqt, iwt, qlt, ctb = _project(
        x2, w_in[0], w_uk[0], kv_norm_g[0], lb_logits, bsz, seq)
    za = _dsa(iqt, iwt, ik, c, ctb, qlt, bias, w_uv[0], ga, bsz, seq)
    zb = _hgrn(qs, logf, v, gb, hgrn_norm_g[0], bsz, seq)
    out = _output(za.reshape(t, A_WIDTH), zb.reshape(t, B_WIDTH), x2, w_o[0], ln_g[0], ln_b[0])
    return out.reshape(bsz, seq, d)
```
